```python
import jax, jax.numpy as jnp
from jax import lax
import numpy as np

D_MODEL = 2048
BATCH = 4
SEQ = 4096
DEPTH = 1
DEC_BATCH = 16
DEC_SEQ = 64
PAST_LEN = 1024

CHUNK = 64
N_HEADS = 8
HEAD_DIM = 128
D_SB = N_HEADS * HEAD_DIM
POOL_WINDOWS = (2, 4, 8, 16)
N_POOL_GROUPS = len(POOL_WINDOWS)
D_POOL = 1024
POOL_GROUP = D_POOL // N_POOL_GROUPS
POOL_OUT = D_MODEL // N_POOL_GROUPS
POOL_PAST = max(POOL_WINDOWS) - 1
D_IN = 3 * D_SB + D_POOL + 2 * D_MODEL
D_FF = -(-8 * D_MODEL // (3 * 256)) * 256
DN_ALPHA = (2 * DEPTH) ** 0.25
DN_BETA = (8 * DEPTH) ** -0.25
Q_BLOCK = 128
LN_EPS = 1e-5

kernel_name = "stickbreak_pool_hybrid_stream_step"


def ln_plain(x):
    xf = x.astype(jnp.float32)
    mu = jnp.mean(xf, axis=-1, keepdims=True)
    xc = xf - mu
    var = jnp.mean(xc * xc, axis=-1, keepdims=True)
    return (xc * lax.rsqrt(var + LN_EPS)).astype(x.dtype)


def layer_norm(x, g, b):
    xf = x.astype(jnp.float32)
    mu = jnp.mean(xf, axis=-1, keepdims=True)
    xc = xf - mu
    var = jnp.mean(xc * xc, axis=-1, keepdims=True)
    y = xc * lax.rsqrt(var + LN_EPS) * g.astype(jnp.float32) + b.astype(jnp.float32)
    return y.astype(x.dtype)


def stick_breaking(q, k, v, q_pos, k_pos):
    z = jnp.einsum('bhqd,bhkd->bhqk', q, k).astype(jnp.float32) * (HEAD_DIM ** -0.5)
    mask = k_pos[None, :] < q_pos[:, None]
    log_beta = jax.nn.log_sigmoid(z)
    log_1mb = jnp.where(mask, jax.nn.log_sigmoid(-z), 0.0)
    after = lax.cumsum(log_1mb, axis=3, reverse=True) - log_1mb
    w = jnp.where(mask, jnp.exp(log_beta + after), 0.0)
    return jnp.einsum('bhqk,bhkd->bhqd', w.astype(v.dtype), v)


def multiscale_pool(p, past, pos0, w_pool, pool_scale):
    B, T, _ = p.shape
    full = jnp.concatenate([past, p], axis=1)
    ff = full.astype(jnp.float32)
    cs = jnp.concatenate([jnp.zeros((B, 1, D_POOL), jnp.float32), jnp.cumsum(ff, axis=1)], axis=1)
    pos = pos0 + jnp.arange(T)
    pf = p.astype(jnp.float32)
    outs = []
    for g, win in enumerate(POOL_WINDOWS):
        sl = slice(g * POOL_GROUP, (g + 1) * POOL_GROUP)
        hi = cs[:, POOL_PAST + 1:POOL_PAST + 1 + T, sl]
        lo = cs[:, POOL_PAST + 1 - win:POOL_PAST + 1 - win + T, sl]
        cnt = jnp.minimum(win, pos + 1).astype(jnp.float32)
        outs.append((hi - lo) / cnt[None, :, None] - pf[..., sl])
    d = jnp.stack(outs, axis=2).astype(p.dtype)
    y = jnp.einsum('btgc,gce->btge', d, w_pool).reshape(B, T, D_MODEL) * pool_scale
    return y, full[:, -POOL_PAST:, :]


def token_mixer(u, k_past, v_past, pool_past, pos0, w_in, w_sb_out, w_pool, pool_scale, w_out):
    B, T, _ = u.shape
    proj = u @ w_in
    q, k, v, p, ga, gb = jnp.split(
        proj, [D_SB, 2 * D_SB, 3 * D_SB, 3 * D_SB + D_POOL, 3 * D_SB + D_POOL + D_MODEL], axis=-1)
    heads = lambda t: t.reshape(B, T, N_HEADS, HEAD_DIM).transpose(0, 2, 1, 3)
    q, k, v = heads(q), heads(k), heads(v)
    if k_past is None:
        k_all, v_all = k, v
    else:
        k_all = jnp.concatenate([k_past, k], axis=2)
        v_all = jnp.concatenate([v_past, v], axis=2)
    q_pos = pos0 + jnp.arange(T)
    k_pos = jnp.arange(k_all.shape[2])
    if T % Q_BLOCK == 0:
        nb = T // Q_BLOCK
        qb = q.reshape(B, N_HEADS, nb, Q_BLOCK, HEAD_DIM).transpose(2, 0, 1, 3, 4)
        pb = q_pos.reshape(nb, Q_BLOCK)
        ob = lax.map(lambda a: stick_breaking(a[0], k_all, v_all, a[1], k_pos), (qb, pb))
        o = ob.transpose(1, 2, 0, 3, 4).reshape(B, N_HEADS, T, HEAD_DIM)
    else:
        o = stick_breaking(q, k_all, v_all, q_pos, k_pos)
    y_a = o.transpose(0, 2, 1, 3).reshape(B, T, D_SB) @ w_sb_out
    if pool_past is None:
        pool_past = jnp.zeros((B, POOL_PAST, D_POOL), p.dtype)
    y_b, pool_new = multiscale_pool(p, pool_past, pos0, w_pool, pool_scale)
    m = jax.nn.sigmoid(ga) * y_a + jax.nn.sigmoid(gb) * y_b
    return m @ w_out, k, v, pool_new


def block(x, c, k_past, v_past, pool_past, pos0, w_ada, b_ada, w_in, w_sb_out, w_pool,
          pool_scale, w_out, ln1_g, ln1_b, w_gate, w_up, w_down, ln2_g, ln2_b):
    B = x.shape[0]
    mod = (jax.nn.silu(c) @ w_ada + b_ada).reshape(B, 6, 1, D_MODEL)
    sh1, sc1, g1, sh2, sc2, g2 = (mod[:, i] for i in range(6))
    u = ln_plain(x) * (1 + sc1) + sh1
    mix, k_new, v_new, pool_new = token_mixer(u, k_past, v_past, pool_past, pos0, w_in,
                                              w_sb_out, w_pool, pool_scale, w_out)
    x = layer_norm(DN_ALPHA * x + g1 * mix, ln1_g, ln1_b)
    u = ln_plain(x) * (1 + sc2) + sh2
    f = (jax.nn.silu(u @ w_gate) * (u @ w_up)) @ w_down
    x = layer_norm(DN_ALPHA * x + g2 * f, ln2_g, ln2_b)
    return x, k_new, v_new, pool_new


def setup_inputs(seed: int = 0) -> dict:
    key = jax.random.key(seed)
    ks = jax.random.split(key, 24)
    nrm = lambda k, s: jax.random.normal(k, s, jnp.float32)
    w_in = nrm(ks[0], (DEPTH, D_MODEL, D_IN)) * D_MODEL ** -0.5
    w_in = w_in.at[..., 2 * D_SB:3 * D_SB].multiply(DN_BETA)
    return {
        "x_prompt": nrm(ks[1], (BATCH, SEQ, D_MODEL)),
        "x_sample": nrm(ks[2], (DEC_BATCH, DEC_SEQ, D_MODEL)),
        "c_prompt": nrm(ks[3], (BATCH, D_MODEL)),
        "c_sample": nrm(ks[4], (DEC_BATCH, D_MODEL)),
        "cache_k": nrm(ks[5], (DEPTH, DEC_BATCH, N_HEADS, PAST_LEN, HEAD_DIM)),
        "cache_v": nrm(ks[6], (DEPTH, DEC_BATCH, N_HEADS, PAST_LEN, HEAD_DIM)) * DN_BETA,
        "state_pool": nrm(ks[7], (DEPTH, DEC_BATCH, POOL_PAST, D_POOL)),
        "w_ada": nrm(ks[8], (DEPTH, D_MODEL, 6 * D_MODEL)) * D_MODEL ** -0.5,
        "b_ada": nrm(ks[9], (DEPTH, 6 * D_MODEL)) * 0.02,
        "w_in": w_in,
        "w_sb_out": nrm(ks[10], (DEPTH, D_SB, D_MODEL)) * D_SB ** -0.5,
        "w_pool": nrm(ks[11], (DEPTH, N_POOL_GROUPS, POOL_GROUP, POOL_OUT)) * POOL_GROUP ** -0.5,
        "pool_scale": 1.0 + 0.02 * nrm(ks[12], (DEPTH, D_MODEL)),
        "w_out": nrm(ks[13], (DEPTH, D_MODEL, D_MODEL)) * D_MODEL ** -0.5 * DN_BETA,
        "ln1_g": 1.0 + 0.02 * nrm(ks[14], (DEPTH, D_MODEL)),
        "ln1_b": 0.02 * nrm(ks[15], (DEPTH, D_MODEL)),
        "w_gate": nrm(ks[16], (DEPTH, D_MODEL, D_FF)) * D_MODEL ** -0.5,
        "w_up": nrm(ks[17], (DEPTH, D_MODEL, D_FF)) * D_MODEL ** -0.5,
        "w_down": nrm(ks[18], (DEPTH, D_FF, D_MODEL)) * D_FF ** -0.5 * DN_BETA,
        "ln2_g": 1.0 + 0.02 * nrm(ks[19], (DEPTH, D_MODEL)),
        "ln2_b": 0.02 * nrm(ks[20], (DEPTH, D_MODEL)),
    }


def reference(x_prompt, x_sample, c_prompt, c_sample, cache_k, cache_v, state_pool,
              w_ada, b_ada, w_in, w_sb_out, w_pool, pool_scale, w_out, ln1_g, ln1_b,
              w_gate, w_up, w_down, ln2_g, ln2_b):
    xp, xs = x_prompt, x_sample
    kp_l, vp_l, pp_l, ksm_l, vsm_l, psm_l = [], [], [], [], [], []
    for l in range(DEPTH):
        wl = (w_ada[l], b_ada[l], w_in[l], w_sb_out[l], w_pool[l], pool_scale[l], w_out[l],
              ln1_g[l], ln1_b[l], w_gate[l], w_up[l], w_down[l], ln2_g[l], ln2_b[l])
        xp, kp, vp, pp = block(xp, c_prompt, None, None, None, 0, *wl)
        xs, ksm, vsm, psm = block(xs, c_sample, cache_k[l], cache_v[l], state_pool[l], PAST_LEN, *wl)
        kp_l.append(kp); vp_l.append(vp); pp_l.append(pp)
        ksm_l.append(ksm); vsm_l.append(vsm); psm_l.append(psm)
    return (xp, xs, jnp.stack(kp_l), jnp.stack(vp_l), jnp.stack(pp_l),
            jnp.stack(ksm_l), jnp.stack(vsm_l), jnp.stack(psm_l))
```

```python
import functools

import jax
import jax.numpy as jnp
from jax import lax
from jax.experimental import pallas as pl
from jax.experimental.pallas import tpu as pltpu

D_MODEL = 2048
N_HEADS = 8
HEAD_DIM = 128
D_SB = N_HEADS * HEAD_DIM
POOL_WINDOWS = (2, 4, 8, 16)
N_POOL_GROUPS = len(POOL_WINDOWS)
D_POOL = 1024
POOL_GROUP = D_POOL // N_POOL_GROUPS
POOL_OUT = D_MODEL // N_POOL_GROUPS
POOL_PAST = max(POOL_WINDOWS) - 1
POOL_HALO = POOL_PAST + 1
D_IN = 3 * D_SB + D_POOL + 2 * D_MODEL
DEPTH = 1
DN_ALPHA = (2 * DEPTH) ** 0.25
LN_EPS = 1e-5
SB_SCALE = HEAD_DIM ** -0.5

F32_EXP_UNDERFLOW = -110.0

V7X_VMEM_BYTES = 64 * 1024 * 1024
VMEM_LIMIT = V7X_VMEM_BYTES - 8 * 1024 * 1024

ROW_TILE = 512
PROJ_COLS = 1024
ADA_COLS = 1024
FF_COLS = 512
ATTN_BLOCK = 256
CACHE_BLOCK = 256

BF16 = jnp.bfloat16
F32 = jnp.float32


def _params(*sem):
    return pltpu.CompilerParams(dimension_semantics=sem, vmem_limit_bytes=VMEM_LIMIT)


def _dot(a, b):
    return jnp.dot(a, b, preferred_element_type=F32)


def _normalize(x):
    mu = jnp.mean(x, axis=-1, keepdims=True)
    xc = x - mu
    var = jnp.mean(xc * xc, axis=-1, keepdims=True)
    return xc * lax.rsqrt(var + LN_EPS)


def _modulated_ln(x_ref, mod_ref, shift_idx, scale_idx):
    x = x_ref[...]
    u = _normalize(x) * (1.0 + mod_ref[:, scale_idx:scale_idx + 1, :]) + mod_ref[:, shift_idx:shift_idx + 1, :]
    return u.reshape(x.shape[0] * x.shape[1], x.shape[2])


def _residual_ln(x_ref, mod_ref, gate_idx, branch, g_ref, b_ref):
    x = x_ref[...]
    y = DN_ALPHA * x + mod_ref[:, gate_idx:gate_idx + 1, :] * branch.reshape(x.shape)
    return _normalize(y) * g_ref[...] + b_ref[...]


def _adaln_kernel(c_ref, w_ref, b_ref, o_ref):
    a = jax.nn.silu(c_ref[...]).astype(BF16)
    o_ref[...] = _dot(a, w_ref[...].astype(BF16)) + b_ref[...]


def _adaln(c, w_ada, b_ada):
    n = c.shape[0]
    cols = w_ada.shape[1]
    return pl.pallas_call(
        _adaln_kernel,
        grid=(cols // ADA_COLS,),
        in_specs=[
            pl.BlockSpec((n, D_MODEL), lambda j: (0, 0)),
            pl.BlockSpec((D_MODEL, ADA_COLS), lambda j: (0, j)),
            pl.BlockSpec((1, ADA_COLS), lambda j: (0, j)),
        ],
        out_specs=pl.BlockSpec((n, ADA_COLS), lambda j: (0, j)),
        out_shape=jax.ShapeDtypeStruct((n, cols), F32),
        compiler_params=_params("arbitrary"),
        name="adaln",
    )(c, w_ada, b_ada.reshape(1, cols))


_Q_COL, _K_COL, _V_COL, _P_COL, _GA_COL, _GB_COL = 0, 1, 2, 3, 4, 6


def _proj_kernel(x_ref, mod_ref, w_ref, q_ref, k_ref, v_ref, p_ref, ga_ref, gb_ref, u_ref):
    j = pl.program_id(2)
    seqs, rows = x_ref.shape[0], x_ref.shape[1]

    @pl.when(j == 0)
    def _():
        u_ref[...] = _modulated_ln(x_ref, mod_ref, 0, 1).astype(BF16)

    res = _dot(u_ref[...], w_ref[...])

    def store_heads(ref):
        for s in range(seqs):
            for h in range(N_HEADS):
                ref[s, h] = res[s * rows:(s + 1) * rows, h * HEAD_DIM:(h + 1) * HEAD_DIM]

    @pl.when(j == _Q_COL)
    def _():
        q_ref[...] = res.astype(BF16).reshape(q_ref.shape)

    @pl.when(j == _K_COL)
    def _():
        store_heads(k_ref)

    @pl.when(j == _V_COL)
    def _():
        store_heads(v_ref)

    @pl.when(j == _P_COL)
    def _():
        p_ref[...] = res.reshape(p_ref.shape)

    @pl.when((j >= _GA_COL) & (j < _GB_COL))
    def _():
        ga_ref[...] = res.reshape(ga_ref.shape)

    @pl.when(j >= _GB_COL)
    def _():
        gb_ref[...] = res.reshape(gb_ref.shape)


def _proj(x, mod, w_in, seqs, rows):
    B, T, _ = x.shape
    grid = (B // seqs, T // rows, D_IN // PROJ_COLS)
    row_block = lambda width: (seqs, rows, width)
    head_block = (None, seqs, N_HEADS, rows, HEAD_DIM)
    head_shape = jax.ShapeDtypeStruct((DEPTH, B, N_HEADS, T, HEAD_DIM), F32)
    half = lambda first: (lambda b, t, j: (b, t, jnp.clip(j - first, 0, 1)))
    return pl.pallas_call(
        _proj_kernel,
        grid=grid,
        in_specs=[
            pl.BlockSpec(row_block(D_MODEL), lambda b, t, j: (b, t, 0)),
            pl.BlockSpec((seqs, 6, D_MODEL), lambda b, t, j: (b, 0, 0)),
            pl.BlockSpec((D_MODEL, PROJ_COLS), lambda b, t, j: (0, j)),
        ],
        out_specs=[
            pl.BlockSpec(row_block(D_SB), lambda b, t, j: (b, t, 0)),
            pl.BlockSpec(head_block, lambda b, t, j: (0, b, 0, t, 0)),
            pl.BlockSpec(head_block, lambda b, t, j: (0, b, 0, t, 0)),
            pl.BlockSpec(row_block(D_POOL), lambda b, t, j: (b, t, 0)),
            pl.BlockSpec(row_block(PROJ_COLS), half(_GA_COL)),
            pl.BlockSpec(row_block(PROJ_COLS), half(_GB_COL)),
        ],
        out_shape=[
            jax.ShapeDtypeStruct((B, T, D_SB), BF16),
            head_shape,
            head_shape,
            jax.ShapeDtypeStruct((B, T, D_POOL), F32),
            jax.ShapeDtypeStruct((B, T, D_MODEL), F32),
            jax.ShapeDtypeStruct((B, T, D_MODEL), F32),
        ],
        scratch_shapes=[pltpu.VMEM((seqs * rows, D_MODEL), BF16)],
        compiler_params=_params("parallel", "parallel", "arbitrary"),
        name="proj",
    )(x, mod, w_in)


def _strict_lower(n):
    r = lax.broadcasted_iota(jnp.int32, (n, n), 0)
    c = lax.broadcasted_iota(jnp.int32, (n, n), 1)
    return jnp.where(r > c, 1.0, 0.0).astype(BF16)


def _sb_block(q, k_blk, v_blk, tri, carry_ref, acc_ref, diagonal):
    z = lax.dot_general(q, k_blk, (((1,), (1,)), ((), ())), preferred_element_type=F32) * SB_SCALE
    softplus_tail = jnp.log1p(jnp.exp(-jnp.abs(z)))
    log_beta = jnp.minimum(z, 0.0) - softplus_tail
    log_1mb = -jnp.maximum(z, 0.0) - softplus_tail
    if diagonal:
        row = lax.broadcasted_iota(jnp.int32, z.shape, 0)
        col = lax.broadcasted_iota(jnp.int32, z.shape, 1)
        mask = col < row
        log_1mb = jnp.where(mask, log_1mb, 0.0)
    hi = log_1mb.astype(BF16)
    lo = (log_1mb - hi.astype(F32)).astype(BF16)
    after = _dot(hi, tri) + _dot(lo, tri)
    carry = carry_ref[...]
    w = jnp.exp(log_beta + after + carry)
    if diagonal:
        w = jnp.where(mask, w, 0.0)
    acc_ref[...] += _dot(w.astype(BF16), v_blk)
    carry = carry + after[:, :1] + log_1mb[:, :1]
    carry_ref[...] = carry
    return jnp.max(carry)


def _walk_past(n_blocks, bound, block_fn):
    def cond(state):
        j, bound = state
        return (j >= 0) & (bound > F32_EXP_UNDERFLOW)

    def body(state):
        j, _ = state
        return j - 1, block_fn(j)

    lax.while_loop(cond, body, (n_blocks - 1, bound))


def _attn_prompt_kernel(q_ref, k_ref, v_ref, o_ref, kb_ref, vb_ref, carry_ref, acc_ref):
    qi = pl.program_id(2)
    blk = q_ref.shape[0]

    @pl.when(qi == 0)
    def _():
        kb_ref[...] = k_ref[...].astype(BF16)
        vb_ref[...] = v_ref[...].astype(BF16)

    q = q_ref[...]
    tri = _strict_lower(blk)
    carry_ref[...] = jnp.zeros_like(carry_ref)
    acc_ref[...] = jnp.zeros_like(acc_ref)

    def block(j, diagonal):
        rows = pl.ds(pl.multiple_of(j * blk, blk), blk)
        return _sb_block(q, kb_ref[rows, :], vb_ref[rows, :], tri, carry_ref, acc_ref, diagonal)

    bound = block(qi, True)
    _walk_past(qi, bound, lambda j: block(j, False))
    o_ref[...] = acc_ref[...].astype(o_ref.dtype)


def _attn_prompt(q, k, v):
    B, T, _ = q.shape
    blk = ATTN_BLOCK
    kv_spec = pl.BlockSpec((None, None, None, T, HEAD_DIM), lambda b, h, i: (0, b, h, 0, 0))
    q_spec = pl.BlockSpec((None, blk, HEAD_DIM), lambda b, h, i: (b, i, h))
    return pl.pallas_call(
        _attn_prompt_kernel,
        grid=(B, N_HEADS, T // blk),
        in_specs=[q_spec, kv_spec, kv_spec],
        out_specs=q_spec,
        out_shape=jax.ShapeDtypeStruct((B, T, D_SB), BF16),
        scratch_shapes=[
            pltpu.VMEM((T, HEAD_DIM), BF16),
            pltpu.VMEM((T, HEAD_DIM), BF16),
            pltpu.VMEM((blk, 1), F32),
            pltpu.VMEM((blk, HEAD_DIM), F32),
        ],
        compiler_params=_params("parallel", "parallel", "arbitrary"),
        name="attn_prompt",
    )(q, k, v)


def _attn_sample_kernel(q_ref, k_ref, v_ref, ck_ref, cv_ref, o_ref, carry_ref, acc_ref):
    q = q_ref[...]
    carry_ref[...] = jnp.zeros_like(carry_ref)
    acc_ref[...] = jnp.zeros_like(acc_ref)
    bound = _sb_block(q, k_ref[...].astype(BF16), v_ref[...].astype(BF16), _strict_lower(q.shape[0]),
                      carry_ref, acc_ref, True)
    tri = _strict_lower(CACHE_BLOCK)

    def block(j):
        rows = pl.ds(pl.multiple_of(j * CACHE_BLOCK, CACHE_BLOCK), CACHE_BLOCK)
        return _sb_block(q, ck_ref[rows, :].astype(BF16), cv_ref[rows, :].astype(BF16), tri,
                         carry_ref, acc_ref, False)

    _walk_past(ck_ref.shape[0] // CACHE_BLOCK, bound, block)
    o_ref[...] = acc_ref[...].astype(o_ref.dtype)


def _attn_sample(q, k, v, cache_k, cache_v):
    B, T, _ = q.shape
    past = cache_k.shape[3]
    new_spec = pl.BlockSpec((None, None, None, T, HEAD_DIM), lambda b, h: (0, b, h, 0, 0))
    cache_spec = pl.BlockSpec((None, None, None, past, HEAD_DIM), lambda b, h: (0, b, h, 0, 0))
    q_spec = pl.BlockSpec((None, T, HEAD_DIM), lambda b, h: (b, 0, h))
    return pl.pallas_call(
        _attn_sample_kernel,
        grid=(B, N_HEADS),
        in_specs=[q_spec, new_spec, new_spec, cache_spec, cache_spec],
        out_specs=q_spec,
        out_shape=jax.ShapeDtypeStruct((B, T, D_SB), BF16),
        scratch_shapes=[pltpu.VMEM((T, 1), F32), pltpu.VMEM((T, HEAD_DIM), F32)],
        compiler_params=_params("parallel", "parallel"),
        name="attn_sample",
    )(q, k, v, cache_k, cache_v)


def _gate_kernel(o_ref, p_ref, halo_ref, ga_ref, gb_ref, wsb_ref, wpool_ref, pscale_ref, m_ref,
                 *, pos0, zero_first_halo):
    seqs, rows = p_ref.shape[0], p_ref.shape[1]
    t0 = pl.program_id(1) * rows
    y_a = _dot(o_ref[...].reshape(seqs * rows, D_SB), wsb_ref[...])

    halo = halo_ref[...]
    if zero_first_halo:
        halo = jnp.where(t0 == 0, 0.0, halo)
    pos = pos0 + t0 + lax.broadcasted_iota(jnp.int32, (rows, POOL_GROUP), 0)

    y_b = []
    for g, win in enumerate(POOL_WINDOWS):
        cols = slice(g * POOL_GROUP, (g + 1) * POOL_GROUP)
        cnt = jnp.minimum(win, pos + 1).astype(F32)
        diffs = []
        for s in range(seqs):
            p = p_ref[s, :, cols]
            acc = jnp.concatenate([halo[s, :, cols], p], axis=0)
            shift = 1
            while shift < win:
                acc = acc + pltpu.roll(acc, shift, 0)
                shift *= 2
            diffs.append(acc[POOL_HALO:, :] / cnt - p)
        d = jnp.concatenate(diffs, axis=0).astype(BF16)
        y_b.append(_dot(d, wpool_ref[g]))
    y_b = jnp.concatenate(y_b, axis=1) * pscale_ref[...]

    gate_a = jax.nn.sigmoid(ga_ref[...]).reshape(seqs * rows, D_MODEL)
    gate_b = jax.nn.sigmoid(gb_ref[...]).reshape(seqs * rows, D_MODEL)
    m_ref[...] = (gate_a * y_a + gate_b * y_b).astype(BF16).reshape(m_ref.shape)


def _gate(o, p, halo_src, ga, gb, w_sb_out, w_pool, pool_scale, seqs, rows, pos0, halo_from_p):
    B, T, _ = p.shape
    row_block = lambda width: pl.BlockSpec((seqs, rows, width), lambda b, t: (b, t, 0))
    if halo_from_p:
        per_tile = rows // POOL_HALO
        halo_spec = pl.BlockSpec((seqs, POOL_HALO, D_POOL),
                                 lambda b, t: (b, jnp.maximum(t * per_tile - 1, 0), 0))
    else:
        halo_spec = pl.BlockSpec((seqs, POOL_HALO, D_POOL), lambda b, t: (b, 0, 0))
    const = lambda shape: pl.BlockSpec(shape, lambda b, t: (0,) * len(shape))
    return pl.pallas_call(
        functools.partial(_gate_kernel, pos0=pos0, zero_first_halo=halo_from_p),
        grid=(B // seqs, T // rows),
        in_specs=[
            row_block(D_SB), row_block(D_POOL), halo_spec, row_block(D_MODEL), row_block(D_MODEL),
            const((D_SB, D_MODEL)), const((N_POOL_GROUPS, POOL_GROUP, POOL_OUT)), const((1, D_MODEL)),
        ],
        out_specs=row_block(D_MODEL),
        out_shape=jax.ShapeDtypeStruct((B, T, D_MODEL), BF16),
        compiler_params=_params("parallel", "arbitrary"),
        name="gate",
    )(o, p, halo_src, ga, gb, w_sb_out, w_pool, pool_scale)


def _mix_kernel(m_ref, x_ref, mod_ref, w_ref, g_ref, b_ref, o_ref):
    seqs, rows = x_ref.shape[0], x_ref.shape[1]
    mix = _dot(m_ref[...].reshape(seqs * rows, D_MODEL), w_ref[...])
    o_ref[...] = _residual_ln(x_ref, mod_ref, 2, mix, g_ref, b_ref)


def _mix(m, x, mod, w_out, ln_g, ln_b, seqs, rows):
    B, T, _ = x.shape
    row_block = pl.BlockSpec((seqs, rows, D_MODEL), lambda b, t: (b, t, 0))
    vec = pl.BlockSpec((1, D_MODEL), lambda b, t: (0, 0))
    return pl.pallas_call(
        _mix_kernel,
        grid=(B // seqs, T // rows),
        in_specs=[
            row_block, row_block,
            pl.BlockSpec((seqs, 6, D_MODEL), lambda b, t: (b, 0, 0)),
            pl.BlockSpec((D_MODEL, D_MODEL), lambda b, t: (0, 0)),
            vec, vec,
        ],
        out_specs=row_block,
        out_shape=jax.ShapeDtypeStruct((B, T, D_MODEL), F32),
        compiler_params=_params("parallel", "arbitrary"),
        name="mix",
    )(m, x, mod, w_out, ln_g, ln_b)


def _ffn_kernel(x_ref, mod_ref, wg_ref, wu_ref, wd_ref, g_ref, b_ref, o_ref, u_ref, acc_ref):
    j = pl.program_id(2)

    @pl.when(j == 0)
    def _():
        u_ref[...] = _modulated_ln(x_ref, mod_ref, 3, 4).astype(BF16)
        acc_ref[...] = jnp.zeros_like(acc_ref)

    u = u_ref[...]
    h = jax.nn.silu(_dot(u, wg_ref[...])) * _dot(u, wu_ref[...])
    acc_ref[...] += _dot(h.astype(BF16), wd_ref[...])

    @pl.when(j == pl.num_programs(2) - 1)
    def _():
        o_ref[...] = _residual_ln(x_ref, mod_ref, 5, acc_ref[...], g_ref, b_ref)


def _ffn(x, mod, w_gate, w_up, w_down, ln_g, ln_b, seqs, rows):
    B, T, _ = x.shape
    d_ff = w_gate.shape[1]
    row_block = pl.BlockSpec((seqs, rows, D_MODEL), lambda b, t, j: (b, t, 0))
    vec = pl.BlockSpec((1, D_MODEL), lambda b, t, j: (0, 0))
    return pl.pallas_call(
        _ffn_kernel,
        grid=(B // seqs, T // rows, d_ff // FF_COLS),
        in_specs=[
            row_block,
            pl.BlockSpec((seqs, 6, D_MODEL), lambda b, t, j: (b, 0, 0)),
            pl.BlockSpec((D_MODEL, FF_COLS), lambda b, t, j: (0, j)),
            pl.BlockSpec((D_MODEL, FF_COLS), lambda b, t, j: (0, j)),
            pl.BlockSpec((FF_COLS, D_MODEL), lambda b, t, j: (j, 0)),
            vec, vec,
        ],
        out_specs=row_block,
        out_shape=jax.ShapeDtypeStruct((B, T, D_MODEL), F32),
        scratch_shapes=[pltpu.VMEM((seqs * rows, D_MODEL), BF16), pltpu.VMEM((seqs * rows, D_MODEL), F32)],
        compiler_params=_params("parallel", "parallel", "arbitrary"),
        name="ffn",
    )(x, mod, w_gate, w_up, w_down, ln_g, ln_b)


def _tile(batch, seq):
    if seq >= ROW_TILE:
        return 1, ROW_TILE
    return min(batch, ROW_TILE // seq), seq


def _layer(x, mod, attn_fn, halo_src, pos0, halo_from_p, w):
    B, T, _ = x.shape
    seqs, rows = _tile(B, T)
    q, k, v, p, ga, gb = _proj(x, mod, w["w_in"], seqs, rows)
    o = attn_fn(q, k, v)
    m = _gate(o, p, p if halo_from_p else halo_src, ga, gb, w["w_sb_out"], w["w_pool"], w["pool_scale"],
              seqs, rows, pos0, halo_from_p)
    x = _mix(m, x, mod, w["w_out"], w["ln1_g"], w["ln1_b"], seqs, rows)
    x = _ffn(x, mod, w["w_gate"], w["w_up"], w["w_down"], w["ln2_g"], w["ln2_b"], seqs, rows)
    return x, k, v, p[None, :, T - POOL_PAST:, :]


def kernel(x_prompt, x_sample, c_prompt, c_sample, cache_k, cache_v, state_pool, w_ada, b_ada, w_in, w_sb_out, w_pool, pool_scale, w_out, ln1_g, ln1_b, w_gate, w_up, w_down, ln2_g, ln2_b):
    assert w_in.shape[0] == DEPTH
    n_prompt = c_prompt.shape[0]
    mod = _adaln(jnp.concatenate([c_prompt, c_sample], axis=0), w_ada[0], b_ada[0])
    mod = mod.reshape(mod.shape[0], 6, D_MODEL)
    w = {
        "w_in": w_in[0].astype(BF16), "w_sb_out": w_sb_out[0].astype(BF16), "w_pool": w_pool[0].astype(BF16),
        "w_out": w_out[0].astype(BF16), "w_gate": w_gate[0].astype(BF16), "w_up": w_up[0].astype(BF16),
        "w_down": w_down[0].astype(BF16), "pool_scale": pool_scale, "ln1_g": ln1_g, "ln1_b": ln1_b,
        "ln2_g": ln2_g, "ln2_b": ln2_b,
    }
    y_p, k_p, v_p, pool_p = _layer(x_prompt, mod[:n_prompt], _attn_prompt, None, 0, True, w)

    past_len = cache_k.shape[3]
    pool_halo = jnp.pad(state_pool[0], ((0, 0), (POOL_HALO - POOL_PAST, 0), (0, 0)))
    attn_sample = lambda q, k, v: _attn_sample(q, k, v, cache_k, cache_v)
    y_s, k_s, v_s, pool_s = _layer(x_sample, mod[n_prompt:], attn_sample, pool_halo, past_len, False, w)
    return y_p, y_s, k_p, v_p, pool_p, k_s, v_s, pool_s
```

```python
import functools
import math

import jax
import jax.numpy as jnp
from jax import lax
from jax.experimental import pallas as pl
from jax.experimental.pallas import tpu as pltpu

D_MODEL = 2048
N_HEADS = 8
HEAD_DIM = 128
D_SB = N_HEADS * HEAD_DIM
POOL_WINDOWS = (2, 4, 8, 16)
N_POOL_GROUPS = len(POOL_WINDOWS)
D_POOL = 1024
POOL_GROUP = D_POOL // N_POOL_GROUPS
POOL_OUT = D_MODEL // N_POOL_GROUPS
POOL_PAST = max(POOL_WINDOWS) - 1
POOL_HALO = POOL_PAST + 1
D_IN = 3 * D_SB + D_POOL + 2 * D_MODEL
DEPTH = 1
DN_ALPHA = (2 * DEPTH) ** 0.25
LN_EPS = 1e-5

LOG2_E = math.log2(math.e)
SB_SCALE_LOG2 = HEAD_DIM ** -0.5 * LOG2_E
UNDERFLOW_LOG2 = -110.0 * LOG2_E

V7X_VMEM_BYTES = 64 * 1024 * 1024
VMEM_LIMIT = V7X_VMEM_BYTES - 8 * 1024 * 1024
LANES = 128

ROW_TILE = 512
PROJ_COLS = 1024
ADA_COLS = 1024
FF_COLS = 512
Q_BLOCK = 128
PAST_BLOCK = 2 * Q_BLOCK

BF16 = jnp.bfloat16
F32 = jnp.float32


def _params(*sem):
    return pltpu.CompilerParams(dimension_semantics=sem, vmem_limit_bytes=VMEM_LIMIT)


def _dot(a, b):
    return jnp.dot(a, b, preferred_element_type=F32)


def _normalize(x):
    mu = jnp.mean(x, axis=-1, keepdims=True)
    xc = x - mu
    var = jnp.mean(xc * xc, axis=-1, keepdims=True)
    return xc * lax.rsqrt(var + LN_EPS)


def _modulated_ln(x_ref, mod_ref, shift_idx, scale_idx):
    x = x_ref[...]
    u = _normalize(x) * (1.0 + mod_ref[:, scale_idx:scale_idx + 1, :]) + mod_ref[:, shift_idx:shift_idx + 1, :]
    return u.reshape(x.shape[0] * x.shape[1], x.shape[2])


def _residual_ln(x_ref, mod_ref, gate_idx, branch, g_ref, b_ref):
    x = x_ref[...]
    y = DN_ALPHA * x + mod_ref[:, gate_idx:gate_idx + 1, :] * branch.reshape(x.shape)
    return _normalize(y) * g_ref[...] + b_ref[...]


def _adaln_kernel(c_ref, w_ref, b_ref, o_ref):
    a = jax.nn.silu(c_ref[...]).astype(BF16)
    o_ref[...] = _dot(a, w_ref[...].astype(BF16)) + b_ref[...]


def _adaln(c, w_ada, b_ada):
    n = c.shape[0]
    cols = w_ada.shape[1]
    return pl.pallas_call(
        _adaln_kernel,
        grid=(cols // ADA_COLS,),
        in_specs=[
            pl.BlockSpec((n, D_MODEL), lambda j: (0, 0)),
            pl.BlockSpec((D_MODEL, ADA_COLS), lambda j: (0, j)),
            pl.BlockSpec((1, ADA_COLS), lambda j: (0, j)),
        ],
        out_specs=pl.BlockSpec((n, ADA_COLS), lambda j: (0, j)),
        out_shape=jax.ShapeDtypeStruct((n, cols), F32),
        compiler_params=_params("arbitrary"),
        name="adaln",
    )(c, w_ada, b_ada.reshape(1, cols))


_Q_COL, _K_COL, _V_COL, _P_COL, _GA_COL, _GB_COL = 0, 1, 2, 3, 4, 6


def _proj_kernel(x_ref, mod_ref, w_ref, q_ref, k_ref, v_ref, kb_ref, vb_ref, p_ref, ga_ref, gb_ref, u_ref):
    j = pl.program_id(2)
    seqs, rows = x_ref.shape[0], x_ref.shape[1]

    @pl.when(j == 0)
    def _():
        u_ref[...] = _modulated_ln(x_ref, mod_ref, 0, 1).astype(BF16)

    res = _dot(u_ref[...], w_ref[...])

    def store_heads(ref):
        for s in range(seqs):
            for h in range(N_HEADS):
                ref[s, h] = res[s * rows:(s + 1) * rows, h * HEAD_DIM:(h + 1) * HEAD_DIM]

    @pl.when(j == _Q_COL)
    def _():
        q_ref[...] = res.astype(BF16).reshape(q_ref.shape)

    @pl.when(j == _K_COL)
    def _():
        store_heads(k_ref)
        kb_ref[...] = res.astype(BF16).reshape(kb_ref.shape)

    @pl.when(j == _V_COL)
    def _():
        store_heads(v_ref)
        vb_ref[...] = res.astype(BF16).reshape(vb_ref.shape)

    @pl.when(j == _P_COL)
    def _():
        p_ref[...] = res.reshape(p_ref.shape)

    @pl.when((j >= _GA_COL) & (j < _GB_COL))
    def _():
        ga_ref[...] = res.reshape(ga_ref.shape)

    @pl.when(j >= _GB_COL)
    def _():
        gb_ref[...] = res.reshape(gb_ref.shape)


def _proj(x, mod, w_in, seqs, rows):
    B, T, _ = x.shape
    grid = (B // seqs, T // rows, D_IN // PROJ_COLS)
    row_block = lambda width: (seqs, rows, width)
    head_block = (None, seqs, N_HEADS, rows, HEAD_DIM)
    head_shape = jax.ShapeDtypeStruct((DEPTH, B, N_HEADS, T, HEAD_DIM), F32)
    half = lambda first: (lambda b, t, j: (b, t, jnp.clip(j - first, 0, 1)))
    return pl.pallas_call(
        _proj_kernel,
        grid=grid,
        in_specs=[
            pl.BlockSpec(row_block(D_MODEL), lambda b, t, j: (b, t, 0)),
            pl.BlockSpec((seqs, 6, D_MODEL), lambda b, t, j: (b, 0, 0)),
            pl.BlockSpec((D_MODEL, PROJ_COLS), lambda b, t, j: (0, j)),
        ],
        out_specs=[
            pl.BlockSpec(row_block(D_SB), lambda b, t, j: (b, t, 0)),
            pl.BlockSpec(head_block, lambda b, t, j: (0, b, 0, t, 0)),
            pl.BlockSpec(head_block, lambda b, t, j: (0, b, 0, t, 0)),
            pl.BlockSpec(row_block(D_SB), lambda b, t, j: (b, t, 0)),
            pl.BlockSpec(row_block(D_SB), lambda b, t, j: (b, t, 0)),
            pl.BlockSpec(row_block(D_POOL), lambda b, t, j: (b, t, 0)),
            pl.BlockSpec(row_block(PROJ_COLS), half(_GA_COL)),
            pl.BlockSpec(row_block(PROJ_COLS), half(_GB_COL)),
        ],
        out_shape=[
            jax.ShapeDtypeStruct((B, T, D_SB), BF16),
            head_shape,
            head_shape,
            jax.ShapeDtypeStruct((B, T, D_SB), BF16),
            jax.ShapeDtypeStruct((B, T, D_SB), BF16),
            jax.ShapeDtypeStruct((B, T, D_POOL), F32),
            jax.ShapeDtypeStruct((B, T, D_MODEL), F32),
            jax.ShapeDtypeStruct((B, T, D_MODEL), F32),
        ],
        scratch_shapes=[pltpu.VMEM((seqs * rows, D_MODEL), BF16)],
        compiler_params=_params("parallel", "parallel", "arbitrary"),
        name="proj",
    )(x, mod, w_in)


def _suffix_sum_matrix(n):
    r = lax.broadcasted_iota(jnp.int32, (n, LANES + n), 0)
    c = lax.broadcasted_iota(jnp.int32, (n, LANES + n), 1)
    return jnp.where((c < LANES) | (r > c - LANES), 1.0, 0.0).astype(BF16)


def _sb_blocks(qs, k_blks, v_blks, sums, carry_ref, acc_ref, diagonal):
    heads = len(qs)
    tq, n = qs[0].shape[0], k_blks[0].shape[0]
    if diagonal:
        row = lax.broadcasted_iota(jnp.int32, (tq, n), 0)
        col = lax.broadcasted_iota(jnp.int32, (tq, n), 1)
        mask = col < row
    zs = [lax.dot_general(q, k, (((1,), (1,)), ((), ())), preferred_element_type=F32) * SB_SCALE_LOG2
          for q, k in zip(qs, k_blks)]
    log_betas, stacked = [], []
    for z in zs:
        nz = -z
        softplus_tail = jnp.log2(1.0 + jnp.exp2(jnp.minimum(z, nz)))
        log_betas.append(jnp.minimum(z, 0.0) - softplus_tail)
        log_1mb = jnp.minimum(nz, 0.0) - softplus_tail
        if diagonal:
            log_1mb = jnp.where(mask, log_1mb, 0.0)
        hi = log_1mb.astype(BF16)
        lo = (log_1mb - hi.astype(F32)).astype(BF16)
        stacked.append(jnp.concatenate([hi, lo], axis=0))
    rs = [_dot(s, sums) for s in stacked]
    ws = []
    for h in range(heads):
        r = rs[h][:tq] + rs[h][tq:]
        total, after = r[:, :LANES], r[:, LANES:]
        carry = carry_ref[h]
        w = []
        for c in range(0, n, LANES):
            width = min(LANES, n - c)
            w.append(jnp.exp2(log_betas[h][:, c:c + width] + after[:, c:c + width] + carry[:, :width]))
        w = w[0] if len(w) == 1 else jnp.concatenate(w, axis=1)
        if diagonal:
            w = jnp.where(mask, w, 0.0)
        ws.append(w.astype(BF16))
        carry_ref[h] = carry + total
    pvs = [_dot(w, v) for w, v in zip(ws, v_blks)]
    for h in range(heads):
        acc_ref[h] += pvs[h]


def _walk_past(n_blocks, carry_ref, wide_fn, narrow_fn):
    def cond(state):
        j, bound = state
        return (j >= 2) & (bound > UNDERFLOW_LOG2)

    def body(state):
        j, _ = state
        wide_fn(pl.multiple_of((j - 2) * Q_BLOCK, Q_BLOCK))
        return j - 2, jnp.max(carry_ref[...])

    j, bound = lax.while_loop(cond, body, (n_blocks, jnp.float32(0.0)))
    if narrow_fn is not None:
        pl.when((j == 1) & (bound > UNDERFLOW_LOG2))(narrow_fn)


def _head_cols(h):
    return slice(h * HEAD_DIM, (h + 1) * HEAD_DIM)


def _attn_prompt_kernel(q_ref, k_ref, v_ref, o_ref, carry_ref, acc_ref):
    qi = pl.program_id(2)
    heads = q_ref.shape[1] // HEAD_DIM
    carry_ref[...] = jnp.zeros_like(carry_ref)
    acc_ref[...] = jnp.zeros_like(acc_ref)

    def visit(start, n, sums, diagonal):
        rows = pl.ds(start, n)
        hs = [_head_cols(h) for h in range(heads)]
        _sb_blocks([q_ref[:, c] for c in hs], [k_ref[rows, c] for c in hs], [v_ref[rows, c] for c in hs], sums,
                   carry_ref, acc_ref, diagonal)

    visit(pl.multiple_of(qi * Q_BLOCK, Q_BLOCK), Q_BLOCK, _suffix_sum_matrix(Q_BLOCK), True)
    wide_sums = _suffix_sum_matrix(PAST_BLOCK)
    _walk_past(qi, carry_ref,
               lambda start: visit(start, PAST_BLOCK, wide_sums, False),
               lambda: visit(0, Q_BLOCK, _suffix_sum_matrix(Q_BLOCK), False))
    for h in range(heads):
        o_ref[:, _head_cols(h)] = acc_ref[h].astype(o_ref.dtype)


def _attn_prompt(q, k, v, heads):
    B, T, _ = q.shape
    width = heads * HEAD_DIM
    q_spec = pl.BlockSpec((None, Q_BLOCK, width), lambda b, g, i: (b, i, g))
    kv_spec = pl.BlockSpec((None, T, width), lambda b, g, i: (b, 0, g))
    return pl.pallas_call(
        _attn_prompt_kernel,
        grid=(B, N_HEADS // heads, T // Q_BLOCK),
        in_specs=[q_spec, kv_spec, kv_spec],
        out_specs=q_spec,
        out_shape=jax.ShapeDtypeStruct((B, T, D_SB), BF16),
        scratch_shapes=[pltpu.VMEM((heads, Q_BLOCK, LANES), F32), pltpu.VMEM((heads, Q_BLOCK, HEAD_DIM), F32)],
        compiler_params=_params("parallel", "parallel", "arbitrary"),
        name="attn_prompt",
    )(q, k, v)


def _attn_sample_kernel(q_ref, k_ref, v_ref, ck_ref, cv_ref, o_ref, carry_ref, acc_ref):
    tq = q_ref.shape[0]
    carry_ref[...] = jnp.zeros_like(carry_ref)
    acc_ref[...] = jnp.zeros_like(acc_ref)
    hs = [_head_cols(h) for h in range(N_HEADS)]
    qs = [q_ref[:, c] for c in hs]
    _sb_blocks(qs, [k_ref[:, c] for c in hs], [v_ref[:, c] for c in hs], _suffix_sum_matrix(tq),
               carry_ref, acc_ref, True)
    wide_sums = _suffix_sum_matrix(PAST_BLOCK)

    def visit(start):
        rows = pl.ds(start, PAST_BLOCK)
        _sb_blocks(qs, [ck_ref[h, rows, :].astype(BF16) for h in range(N_HEADS)],
                   [cv_ref[h, rows, :].astype(BF16) for h in range(N_HEADS)], wide_sums, carry_ref, acc_ref, False)

    past = ck_ref.shape[1]
    assert past % PAST_BLOCK == 0
    _walk_past(past // Q_BLOCK, carry_ref, visit, None)
    for h in range(N_HEADS):
        o_ref[:, _head_cols(h)] = acc_ref[h].astype(o_ref.dtype)


def _attn_sample(q, k, v, cache_k, cache_v):
    B, T, _ = q.shape
    past = cache_k.shape[3]
    new_spec = pl.BlockSpec((None, T, D_SB), lambda b: (b, 0, 0))
    cache_spec = pl.BlockSpec((None, None, N_HEADS, past, HEAD_DIM), lambda b: (0, b, 0, 0, 0))
    return pl.pallas_call(
        _attn_sample_kernel,
        grid=(B,),
        in_specs=[new_spec, new_spec, new_spec, cache_spec, cache_spec],
        out_specs=new_spec,
        out_shape=jax.ShapeDtypeStruct((B, T, D_SB), BF16),
        scratch_shapes=[pltpu.VMEM((N_HEADS, T, LANES), F32), pltpu.VMEM((N_HEADS, T, HEAD_DIM), F32)],
        compiler_params=_params("parallel"),
        name="attn_sample",
    )(q, k, v, cache_k, cache_v)


def _gate_kernel(o_ref, p_ref, halo_ref, ga_ref, gb_ref, wsb_ref, wpool_ref, pscale_ref, m_ref,
                 *, pos0, zero_first_halo):
    seqs, rows = p_ref.shape[0], p_ref.shape[1]
    t0 = pl.program_id(1) * rows
    y_a = _dot(o_ref[...].reshape(seqs * rows, D_SB), wsb_ref[...])

    halo = halo_ref[...]
    if zero_first_halo:
        halo = jnp.where(t0 == 0, 0.0, halo)
    pos = pos0 + t0 + lax.broadcasted_iota(jnp.int32, (rows, POOL_GROUP), 0)

    y_b = []
    for g, win in enumerate(POOL_WINDOWS):
        cols = slice(g * POOL_GROUP, (g + 1) * POOL_GROUP)
        cnt = jnp.minimum(win, pos + 1).astype(F32)
        diffs = []
        for s in range(seqs):
            p = p_ref[s, :, cols]
            acc = jnp.concatenate([halo[s, :, cols], p], axis=0)
            shift = 1
            while shift < win:
                acc = acc + pltpu.roll(acc, shift, 0)
                shift *= 2
            diffs.append(acc[POOL_HALO:, :] / cnt - p)
        d = jnp.concatenate(diffs, axis=0).astype(BF16)
        y_b.append(_dot(d, wpool_ref[g]))
    y_b = jnp.concatenate(y_b, axis=1) * pscale_ref[...]

    gate_a = jax.nn.sigmoid(ga_ref[...]).reshape(seqs * rows, D_MODEL)
    gate_b = jax.nn.sigmoid(gb_ref[...]).reshape(seqs * rows, D_MODEL)
    m_ref[...] = (gate_a * y_a + gate_b * y_b).astype(BF16).reshape(m_ref.shape)


def _gate(o, p, halo_src, ga, gb, w_sb_out, w_pool, pool_scale, seqs, rows, pos0, halo_from_p):
    B, T, _ = p.shape
    row_block = lambda width: pl.BlockSpec((seqs, rows, width), lambda b, t: (b, t, 0))
    if halo_from_p:
        per_tile = rows // POOL_HALO
        halo_spec = pl.BlockSpec((seqs, POOL_HALO, D_POOL),
                                 lambda b, t: (b, jnp.maximum(t * per_tile - 1, 0), 0))
    else:
        halo_spec = pl.BlockSpec((seqs, POOL_HALO, D_POOL), lambda b, t: (b, 0, 0))
    const = lambda shape: pl.BlockSpec(shape, lambda b, t: (0,) * len(shape))
    return pl.pallas_call(
        functools.partial(_gate_kernel, pos0=pos0, zero_first_halo=halo_from_p),
        grid=(B // seqs, T // rows),
        in_specs=[
            row_block(D_SB), row_block(D_POOL), halo_spec, row_block(D_MODEL), row_block(D_MODEL),
            const((D_SB, D_MODEL)), const((N_POOL_GROUPS, POOL_GROUP, POOL_OUT)), const((1, D_MODEL)),
        ],
        out_specs=row_block(D_MODEL),
        out_shape=jax.ShapeDtypeStruct((B, T, D_MODEL), BF16),
        compiler_params=_params("parallel", "arbitrary"),
        name="gate",
    )(o, p, halo_src, ga, gb, w_sb_out, w_pool, pool_scale)


def _mix_kernel(m_ref, x_ref, mod_ref, w_ref, g_ref, b_ref, o_ref):
    seqs, rows = x_ref.shape[0], x_ref.shape[1]
    mix = _dot(m_ref[...].reshape(seqs * rows, D_MODEL), w_ref[...])
    o_ref[...] = _residual_ln(x_ref, mod_ref, 2, mix, g_ref, b_ref)


def _mix(m, x, mod, w_out, ln_g, ln_b, seqs, rows):
    B, T, _ = x.shape
    row_block = pl.BlockSpec((seqs, rows, D_MODEL), lambda b, t: (b, t, 0))
    vec = pl.BlockSpec((1, D_MODEL), lambda b, t: (0, 0))
    return pl.pallas_call(
        _mix_kernel,
        grid=(B // seqs, T // rows),
        in_specs=[
            row_block, row_block,
            pl.BlockSpec((seqs, 6, D_MODEL), lambda b, t: (b, 0, 0)),
            pl.BlockSpec((D_MODEL, D_MODEL), lambda b, t: (0, 0)),
            vec, vec,
        ],
        out_specs=row_block,
        out_shape=jax.ShapeDtypeStruct((B, T, D_MODEL), F32),
        compiler_params=_params("parallel", "arbitrary"),
        name="mix",
    )(m, x, mod, w_out, ln_g, ln_b)


def _ffn_kernel(x_ref, mod_ref, wg_ref, wu_ref, wd_ref, g_ref, b_ref, o_ref, u_ref, acc_ref):
    j = pl.program_id(2)

    @pl.when(j == 0)
    def _():
        u_ref[...] = _modulated_ln(x_ref, mod_ref, 3, 4).astype(BF16)
        acc_ref[...] = jnp.zeros_like(acc_ref)

    u = u_ref[...]
    h = jax.nn.silu(_dot(u, wg_ref[...])) * _dot(u, wu_ref[...])
    acc_ref[...] += _dot(h.astype(BF16), wd_ref[...])

    @pl.when(j == pl.num_programs(2) - 1)
    def _():
        o_ref[...] = _residual_ln(x_ref, mod_ref, 5, acc_ref[...], g_ref, b_ref)


def _ffn(x, mod, w_gate, w_up, w_down, ln_g, ln_b, seqs, rows):
    B, T, _ = x.shape
    d_ff = w_gate.shape[1]
    row_block = pl.BlockSpec((seqs, rows, D_MODEL), lambda b, t, j: (b, t, 0))
    vec = pl.BlockSpec((1, D_MODEL), lambda b, t, j: (0, 0))
    return pl.pallas_call(
        _ffn_kernel,
        grid=(B // seqs, T // rows, d_ff // FF_COLS),
        in_specs=[
            row_block,
            pl.BlockSpec((seqs, 6, D_MODEL), lambda b, t, j: (b, 0, 0)),
            pl.BlockSpec((D_MODEL, FF_COLS), lambda b, t, j: (0, j)),
            pl.BlockSpec((D_MODEL, FF_COLS), lambda b, t, j: (0, j)),
            pl.BlockSpec((FF_COLS, D_MODEL), lambda b, t, j: (j, 0)),
            vec, vec,
        ],
        out_specs=row_block,
        out_shape=jax.ShapeDtypeStruct((B, T, D_MODEL), F32),
        scratch_shapes=[pltpu.VMEM((seqs * rows, D_MODEL), BF16), pltpu.VMEM((seqs * rows, D_MODEL), F32)],
        compiler_params=_params("parallel", "parallel", "arbitrary"),
        name="ffn",
    )(x, mod, w_gate, w_up, w_down, ln_g, ln_b)


def _tile(batch, seq):
    if seq >= ROW_TILE:
        return 1, ROW_TILE
    return min(batch, ROW_TILE // seq), seq


def _layer(x, mod, attn_fn, halo_src, pos0, halo_from_p, w):
    B, T, _ = x.shape
    seqs, rows = _tile(B, T)
    q, k, v, k_bf, v_bf, p, ga, gb = _proj(x, mod, w["w_in"], seqs, rows)
    o = attn_fn(q, k_bf, v_bf)
    m = _gate(o, p, p if halo_from_p else halo_src, ga, gb, w["w_sb_out"], w["w_pool"], w["pool_scale"],
              seqs, rows, pos0, halo_from_p)
    x = _mix(m, x, mod, w["w_out"], w["ln1_g"], w["ln1_b"], seqs, rows)
    x = _ffn(x, mod, w["w_gate"], w["w_up"], w["w_down"], w["ln2_g"], w["ln2_b"], seqs, rows)
    return x, k, v, p[None, :, T - POOL_PAST:, :]


def kernel(x_prompt, x_sample, c_prompt, c_sample, cache_k, cache_v, state_pool, w_ada, b_ada, w_in, w_sb_out, w_pool, pool_scale, w_out, ln1_g, ln1_b, w_gate, w_up, w_down, ln2_g, ln2_b):
    assert w_in.shape[0] == DEPTH
    n_prompt = c_prompt.shape[0]
    mod = _adaln(jnp.concatenate([c_prompt, c_sample], axis=0), w_ada[0], b_ada[0])
    mod = mod.reshape(mod.shape[0], 6, D_MODEL)
    w = {
        "w_in": w_in[0].astype(BF16), "w_sb_out": w_sb_out[0].astype(BF16), "w_pool": w_pool[0].astype(BF16),
        "w_out": w_out[0].astype(BF16), "w_gate": w_gate[0].astype(BF16), "w_up": w_up[0].astype(BF16),
        "w_down": w_down[0].astype(BF16), "pool_scale": pool_scale, "ln1_g": ln1_g, "ln1_b": ln1_b,
        "ln2_g": ln2_g, "ln2_b": ln2_b,
    }
    attn_prompt = functools.partial(_attn_prompt, heads=N_HEADS)
    y_p, k_p, v_p, pool_p = _layer(x_prompt, mod[:n_prompt], attn_prompt, None, 0, True, w)

    pool_halo = jnp.pad(state_pool[0], ((0, 0), (POOL_HALO - POOL_PAST, 0), (0, 0)))
    attn_sample = lambda q, k, v: _attn_sample(q, k, v, cache_k, cache_v)
    y_s, k_s, v_s, pool_s = _layer(x_sample, mod[n_prompt:], attn_sample, pool_halo, cache_k.shape[3], False, w)
    return y_p, y_s, k_p, v_p, pool_p, k_s, v_s, pool_s
```

```python
import functools
import math

import jax
import jax.numpy as jnp
from jax import lax
from jax.experimental import pallas as pl
from jax.experimental.pallas import tpu as pltpu

D_MODEL = 2048
N_HEADS = 8
HEAD_DIM = 128
D_SB = N_HEADS * HEAD_DIM
POOL_WINDOWS = (2, 4, 8, 16)
N_POOL_GROUPS = len(POOL_WINDOWS)
D_POOL = 1024
POOL_GROUP = D_POOL // N_POOL_GROUPS
POOL_OUT = D_MODEL // N_POOL_GROUPS
POOL_PAST = max(POOL_WINDOWS) - 1
POOL_HALO = POOL_PAST + 1
D_IN = 3 * D_SB + D_POOL + 2 * D_MODEL
DEPTH = 1
DN_ALPHA = (2 * DEPTH) ** 0.25
LN_EPS = 1e-5

LOG2_E = math.log2(math.e)
SB_SCALE_LOG2 = HEAD_DIM ** -0.5 * LOG2_E
UNDERFLOW_LOG2 = -110.0 * LOG2_E

V7X_VMEM_BYTES = 64 * 1024 * 1024
VMEM_LIMIT = V7X_VMEM_BYTES - 8 * 1024 * 1024
LANES = 128

ROW_TILE = 512
ROW_CHUNKS = 2
PROJ_COLS = 1024
ADA_COLS = 1024
FF_COLS = 512
Q_BLOCK = 128
PAST_BLOCK = 2 * Q_BLOCK

BF16 = jnp.bfloat16
F32 = jnp.float32


def _params(*sem):
    return pltpu.CompilerParams(dimension_semantics=sem, vmem_limit_bytes=VMEM_LIMIT)


def _dot(a, b):
    return jnp.dot(a, b, preferred_element_type=F32)


def _normalize(x):
    mu = jnp.mean(x, axis=-1, keepdims=True)
    xc = x - mu
    var = jnp.mean(xc * xc, axis=-1, keepdims=True)
    return xc * lax.rsqrt(var + LN_EPS)


def _modulated_ln(x_ref, mod_ref, shift_idx, scale_idx):
    x = x_ref[...]
    u = _normalize(x) * (1.0 + mod_ref[:, scale_idx:scale_idx + 1, :]) + mod_ref[:, shift_idx:shift_idx + 1, :]
    return u.reshape(x.shape[0] * x.shape[1], x.shape[2])


def _row_chunks(seqs, rows):
    if seqs >= ROW_CHUNKS:
        n = seqs // ROW_CHUNKS
        return [(slice(i * n, (i + 1) * n), slice(0, rows)) for i in range(ROW_CHUNKS)]
    n = rows // ROW_CHUNKS
    return [(slice(0, seqs), slice(i * n, (i + 1) * n)) for i in range(ROW_CHUNKS)]


def _residual_ln(x_ref, mod_ref, gate_idx, branch, g_ref, b_ref):
    x = x_ref[...]
    y = DN_ALPHA * x + mod_ref[:, gate_idx:gate_idx + 1, :] * branch.reshape(x.shape)
    return _normalize(y) * g_ref[...] + b_ref[...]


def _adaln_kernel(c_ref, w_ref, b_ref, o_ref):
    a = jax.nn.silu(c_ref[...]).astype(BF16)
    o_ref[...] = _dot(a, w_ref[...].astype(BF16)) + b_ref[...]


def _adaln(c, w_ada, b_ada):
    n = c.shape[0]
    cols = w_ada.shape[1]
    return pl.pallas_call(
        _adaln_kernel,
        grid=(cols // ADA_COLS,),
        in_specs=[
            pl.BlockSpec((n, D_MODEL), lambda j: (0, 0)),
            pl.BlockSpec((D_MODEL, ADA_COLS), lambda j: (0, j)),
            pl.BlockSpec((1, ADA_COLS), lambda j: (0, j)),
        ],
        out_specs=pl.BlockSpec((n, ADA_COLS), lambda j: (0, j)),
        out_shape=jax.ShapeDtypeStruct((n, cols), F32),
        compiler_params=_params("arbitrary"),
        name="adaln",
    )(c, w_ada, b_ada.reshape(1, cols))


_Q_COL, _K_COL, _V_COL, _P_COL, _GA_COL, _GB_COL = 0, 1, 2, 3, 4, 6


def _proj_kernel(x_ref, mod_ref, w_ref, q_ref, k_ref, v_ref, kb_ref, vb_ref, p_ref, ga_ref, gb_ref, u_ref):
    j = pl.program_id(2)
    seqs, rows = x_ref.shape[0], x_ref.shape[1]

    @pl.when(j == 0)
    def _():
        u_ref[...] = _modulated_ln(x_ref, mod_ref, 0, 1).astype(BF16)

    def columns():
        return _dot(u_ref[...], w_ref[...])

    def store_heads(ref, bf_ref):
        res = columns()
        for s in range(seqs):
            for h in range(N_HEADS):
                ref[s, h] = res[s * rows:(s + 1) * rows, h * HEAD_DIM:(h + 1) * HEAD_DIM]
        bf_ref[...] = res.astype(BF16).reshape(bf_ref.shape)

    @pl.when(j == _Q_COL)
    def _():
        q_ref[...] = columns().astype(BF16).reshape(q_ref.shape)

    @pl.when(j == _K_COL)
    def _():
        store_heads(k_ref, kb_ref)

    @pl.when(j == _V_COL)
    def _():
        store_heads(v_ref, vb_ref)

    @pl.when(j == _P_COL)
    def _():
        p_ref[...] = columns().reshape(p_ref.shape)

    @pl.when((j >= _GA_COL) & (j < _GB_COL))
    def _():
        ga_ref[...] = columns().astype(BF16).reshape(ga_ref.shape)

    @pl.when(j >= _GB_COL)
    def _():
        gb_ref[...] = columns().astype(BF16).reshape(gb_ref.shape)


def _proj(x, mod, w_in, seqs, rows):
    B, T, _ = x.shape
    grid = (B // seqs, T // rows, D_IN // PROJ_COLS)
    row_block = lambda width: (seqs, rows, width)
    head_block = (None, seqs, N_HEADS, rows, HEAD_DIM)
    head_shape = jax.ShapeDtypeStruct((DEPTH, B, N_HEADS, T, HEAD_DIM), F32)
    half = lambda first: (lambda b, t, j: (b, t, jnp.clip(j - first, 0, 1)))
    return pl.pallas_call(
        _proj_kernel,
        grid=grid,
        in_specs=[
            pl.BlockSpec(row_block(D_MODEL), lambda b, t, j: (b, t, 0)),
            pl.BlockSpec((seqs, 6, D_MODEL), lambda b, t, j: (b, 0, 0)),
            pl.BlockSpec((D_MODEL, PROJ_COLS), lambda b, t, j: (0, j)),
        ],
        out_specs=[
            pl.BlockSpec(row_block(D_SB), lambda b, t, j: (b, t, 0)),
            pl.BlockSpec(head_block, lambda b, t, j: (0, b, 0, t, 0)),
            pl.BlockSpec(head_block, lambda b, t, j: (0, b, 0, t, 0)),
            pl.BlockSpec(row_block(D_SB), lambda b, t, j: (b, t, 0)),
            pl.BlockSpec(row_block(D_SB), lambda b, t, j: (b, t, 0)),
            pl.BlockSpec(row_block(D_POOL), lambda b, t, j: (b, t, 0)),
            pl.BlockSpec(row_block(PROJ_COLS), half(_GA_COL)),
            pl.BlockSpec(row_block(PROJ_COLS), half(_GB_COL)),
        ],
        out_shape=[
            jax.ShapeDtypeStruct((B, T, D_SB), BF16),
            head_shape,
            head_shape,
            jax.ShapeDtypeStruct((B, T, D_SB), BF16),
            jax.ShapeDtypeStruct((B, T, D_SB), BF16),
            jax.ShapeDtypeStruct((B, T, D_POOL), F32),
            jax.ShapeDtypeStruct((B, T, D_MODEL), BF16),
            jax.ShapeDtypeStruct((B, T, D_MODEL), BF16),
        ],
        scratch_shapes=[pltpu.VMEM((seqs * rows, D_MODEL), BF16)],
        compiler_params=_params("parallel", "parallel", "arbitrary"),
        name="proj",
    )(x, mod, w_in)


def _suffix_sum_matrix(n):
    r = lax.broadcasted_iota(jnp.int32, (n, LANES + n), 0)
    c = lax.broadcasted_iota(jnp.int32, (n, LANES + n), 1)
    return jnp.where((c < LANES) | (r > c - LANES), 1.0, 0.0).astype(BF16)


def _sb_blocks(qs, k_blks, v_blks, sums, carry_ref, acc_ref, diagonal):
    heads = len(qs)
    tq, n = qs[0].shape[0], k_blks[0].shape[0]
    if diagonal:
        row = lax.broadcasted_iota(jnp.int32, (tq, n), 0)
        col = lax.broadcasted_iota(jnp.int32, (tq, n), 1)
        mask = col < row
    zs = [lax.dot_general(q, k, (((1,), (1,)), ((), ())), preferred_element_type=F32) * SB_SCALE_LOG2
          for q, k in zip(qs, k_blks)]
    log_betas, stacked = [], []
    for z in zs:
        nz = -z
        softplus_tail = jnp.log2(1.0 + jnp.exp2(jnp.minimum(z, nz)))
        log_betas.append(jnp.minimum(z, 0.0) - softplus_tail)
        log_1mb = jnp.minimum(nz, 0.0) - softplus_tail
        if diagonal:
            log_1mb = jnp.where(mask, log_1mb, 0.0)
        hi = log_1mb.astype(BF16)
        lo = (log_1mb - hi.astype(F32)).astype(BF16)
        stacked.append(jnp.concatenate([hi, lo], axis=0))
    rs = [_dot(s, sums) for s in stacked]
    ws = []
    for h in range(heads):
        r = rs[h][:tq] + rs[h][tq:]
        total, after = r[:, :LANES], r[:, LANES:]
        carry = carry_ref[h]
        w = []
        for c in range(0, n, LANES):
            width = min(LANES, n - c)
            w.append(jnp.exp2(log_betas[h][:, c:c + width] + after[:, c:c + width] + carry[:, :width]))
        w = w[0] if len(w) == 1 else jnp.concatenate(w, axis=1)
        if diagonal:
            w = jnp.where(mask, w, 0.0)
        ws.append(w.astype(BF16))
        carry_ref[h] = carry + total
    pvs = [_dot(w, v) for w, v in zip(ws, v_blks)]
    for h in range(heads):
        acc_ref[h] += pvs[h]


def _walk_past(n_blocks, carry_ref, wide_fn, narrow_fn):
    def cond(state):
        j, bound = state
        return (j >= 2) & (bound > UNDERFLOW_LOG2)

    def body(state):
        j, _ = state
        wide_fn(pl.multiple_of((j - 2) * Q_BLOCK, Q_BLOCK))
        return j - 2, jnp.max(carry_ref[...])

    j, bound = lax.while_loop(cond, body, (n_blocks, jnp.float32(0.0)))
    if narrow_fn is not None:
        pl.when((j == 1) & (bound > UNDERFLOW_LOG2))(narrow_fn)


def _head_cols(h):
    return slice(h * HEAD_DIM, (h + 1) * HEAD_DIM)


def _attn_prompt_kernel(q_ref, k_ref, v_ref, o_ref, carry_ref, acc_ref):
    qi = pl.program_id(2)
    heads = q_ref.shape[1] // HEAD_DIM
    carry_ref[...] = jnp.zeros_like(carry_ref)
    acc_ref[...] = jnp.zeros_like(acc_ref)

    def visit(start, n, sums, diagonal):
        rows = pl.ds(start, n)
        hs = [_head_cols(h) for h in range(heads)]
        _sb_blocks([q_ref[:, c] for c in hs], [k_ref[rows, c] for c in hs], [v_ref[rows, c] for c in hs], sums,
                   carry_ref, acc_ref, diagonal)

    visit(pl.multiple_of(qi * Q_BLOCK, Q_BLOCK), Q_BLOCK, _suffix_sum_matrix(Q_BLOCK), True)
    wide_sums = _suffix_sum_matrix(PAST_BLOCK)
    _walk_past(qi, carry_ref,
               lambda start: visit(start, PAST_BLOCK, wide_sums, False),
               lambda: visit(0, Q_BLOCK, _suffix_sum_matrix(Q_BLOCK), False))
    for h in range(heads):
        o_ref[:, _head_cols(h)] = acc_ref[h].astype(o_ref.dtype)


def _attn_prompt(q, k, v, heads):
    B, T, _ = q.shape
    width = heads * HEAD_DIM
    q_spec = pl.BlockSpec((None, Q_BLOCK, width), lambda b, g, i: (b, i, g))
    kv_spec = pl.BlockSpec((None, T, width), lambda b, g, i: (b, 0, g))
    return pl.pallas_call(
        _attn_prompt_kernel,
        grid=(B, N_HEADS // heads, T // Q_BLOCK),
        in_specs=[q_spec, kv_spec, kv_spec],
        out_specs=q_spec,
        out_shape=jax.ShapeDtypeStruct((B, T, D_SB), BF16),
        scratch_shapes=[pltpu.VMEM((heads, Q_BLOCK, LANES), F32), pltpu.VMEM((heads, Q_BLOCK, HEAD_DIM), F32)],
        compiler_params=_params("parallel", "parallel", "arbitrary"),
        name="attn_prompt",
    )(q, k, v)


def _attn_sample_kernel(q_ref, k_ref, v_ref, ck_ref, cv_ref, o_ref, carry_ref, acc_ref):
    tq = q_ref.shape[0]
    carry_ref[...] = jnp.zeros_like(carry_ref)
    acc_ref[...] = jnp.zeros_like(acc_ref)
    hs = [_head_cols(h) for h in range(N_HEADS)]
    qs = [q_ref[:, c] for c in hs]
    _sb_blocks(qs, [k_ref[:, c] for c in hs], [v_ref[:, c] for c in hs], _suffix_sum_matrix(tq),
               carry_ref, acc_ref, True)
    wide_sums = _suffix_sum_matrix(PAST_BLOCK)

    def visit(start):
        rows = pl.ds(start, PAST_BLOCK)
        _sb_blocks(qs, [ck_ref[h, rows, :].astype(BF16) for h in range(N_HEADS)],
                   [cv_ref[h, rows, :].astype(BF16) for h in range(N_HEADS)], wide_sums, carry_ref, acc_ref, False)

    past = ck_ref.shape[1]
    assert past % PAST_BLOCK == 0
    _walk_past(past // Q_BLOCK, carry_ref, visit, None)
    for h in range(N_HEADS):
        o_ref[:, _head_cols(h)] = acc_ref[h].astype(o_ref.dtype)


def _attn_sample(q, k, v, cache_k, cache_v):
    B, T, _ = q.shape
    past = cache_k.shape[3]
    new_spec = pl.BlockSpec((None, T, D_SB), lambda b: (b, 0, 0))
    cache_spec = pl.BlockSpec((None, None, N_HEADS, past, HEAD_DIM), lambda b: (0, b, 0, 0, 0))
    return pl.pallas_call(
        _attn_sample_kernel,
        grid=(B,),
        in_specs=[new_spec, new_spec, new_spec, cache_spec, cache_spec],
        out_specs=new_spec,
        out_shape=jax.ShapeDtypeStruct((B, T, D_SB), BF16),
        scratch_shapes=[pltpu.VMEM((N_HEADS, T, LANES), F32), pltpu.VMEM((N_HEADS, T, HEAD_DIM), F32)],
        compiler_params=_params("parallel"),
        name="attn_sample",
    )(q, k, v, cache_k, cache_v)


def _gate_kernel(o_ref, p_ref, halo_ref, ga_ref, gb_ref, wsb_ref, wpool_ref, pscale_ref, m_ref,
                 *, pos0, zero_first_halo):
    seqs, rows = p_ref.shape[0], p_ref.shape[1]
    t0 = pl.program_id(1) * rows
    y_a = _dot(o_ref[...].reshape(seqs * rows, D_SB), wsb_ref[...])

    halo = halo_ref[...]
    if zero_first_halo:
        halo = jnp.where(t0 == 0, 0.0, halo)
    pos = pos0 + t0 + lax.broadcasted_iota(jnp.int32, (rows, POOL_GROUP), 0)

    y_b = []
    for g, win in enumerate(POOL_WINDOWS):
        cols = slice(g * POOL_GROUP, (g + 1) * POOL_GROUP)
        cnt = jnp.minimum(win, pos + 1).astype(F32)
        diffs = []
        for s in range(seqs):
            p = p_ref[s, :, cols]
            acc = jnp.concatenate([halo[s, :, cols], p], axis=0)
            shift = 1
            while shift < win:
                acc = acc + pltpu.roll(acc, shift, 0)
                shift *= 2
            diffs.append(acc[POOL_HALO:, :] / cnt - p)
        d = jnp.concatenate(diffs, axis=0).astype(BF16)
        y_b.append(_dot(d, wpool_ref[g]))
    y_b = jnp.concatenate(y_b, axis=1) * pscale_ref[...]

    gate_a = jax.nn.sigmoid(ga_ref[...].astype(F32)).reshape(seqs * rows, D_MODEL)
    gate_b = jax.nn.sigmoid(gb_ref[...].astype(F32)).reshape(seqs * rows, D_MODEL)
    m_ref[...] = (gate_a * y_a + gate_b * y_b).astype(BF16).reshape(m_ref.shape)


def _gate(o, p, halo_src, ga, gb, w_sb_out, w_pool, pool_scale, seqs, rows, pos0, halo_from_p):
    B, T, _ = p.shape
    row_block = lambda width: pl.BlockSpec((seqs, rows, width), lambda b, t: (b, t, 0))
    if halo_from_p:
        per_tile = rows // POOL_HALO
        halo_spec = pl.BlockSpec((seqs, POOL_HALO, D_POOL),
                                 lambda b, t: (b, jnp.maximum(t * per_tile - 1, 0), 0))
    else:
        halo_spec = pl.BlockSpec((seqs, POOL_HALO, D_POOL), lambda b, t: (b, 0, 0))
    const = lambda shape: pl.BlockSpec(shape, lambda b, t: (0,) * len(shape))
    return pl.pallas_call(
        functools.partial(_gate_kernel, pos0=pos0, zero_first_halo=halo_from_p),
        grid=(B // seqs, T // rows),
        in_specs=[
            row_block(D_SB), row_block(D_POOL), halo_spec, row_block(D_MODEL), row_block(D_MODEL),
            const((D_SB, D_MODEL)), const((N_POOL_GROUPS, POOL_GROUP, POOL_OUT)), const((1, D_MODEL)),
        ],
        out_specs=row_block(D_MODEL),
        out_shape=jax.ShapeDtypeStruct((B, T, D_MODEL), BF16),
        compiler_params=_params("parallel", "arbitrary"),
        name="gate",
    )(o, p, halo_src, ga, gb, w_sb_out, w_pool, pool_scale)


def _mix_kernel(m_ref, x_ref, mod_ref, w_ref, g_ref, b_ref, o_ref):
    seqs, rows = x_ref.shape[0], x_ref.shape[1]
    for s, r in _row_chunks(seqs, rows):
        mix = _dot(m_ref[s, r, :].reshape(-1, D_MODEL), w_ref[...])
        o_ref[s, r, :] = _residual_ln(x_ref.at[s, r, :], mod_ref.at[s], 2, mix, g_ref, b_ref)


def _mix(m, x, mod, w_out, ln_g, ln_b, seqs, rows):
    B, T, _ = x.shape
    row_block = pl.BlockSpec((seqs, rows, D_MODEL), lambda b, t: (b, t, 0))
    vec = pl.BlockSpec((1, D_MODEL), lambda b, t: (0, 0))
    return pl.pallas_call(
        _mix_kernel,
        grid=(B // seqs, T // rows),
        in_specs=[
            row_block, row_block,
            pl.BlockSpec((seqs, 6, D_MODEL), lambda b, t: (b, 0, 0)),
            pl.BlockSpec((D_MODEL, D_MODEL), lambda b, t: (0, 0)),
            vec, vec,
        ],
        out_specs=row_block,
        out_shape=jax.ShapeDtypeStruct((B, T, D_MODEL), F32),
        compiler_params=_params("parallel", "arbitrary"),
        name="mix",
    )(m, x, mod, w_out, ln_g, ln_b)


def _ffn_kernel(x_ref, mod_ref, wg_ref, wu_ref, wd_ref, g_ref, b_ref, o_ref, u_ref, acc_ref):
    j = pl.program_id(2)

    @pl.when(j == 0)
    def _():
        u_ref[...] = _modulated_ln(x_ref, mod_ref, 3, 4).astype(BF16)
        acc_ref[...] = jnp.zeros_like(acc_ref)

    u = u_ref[...]
    h = jax.nn.silu(_dot(u, wg_ref[...])) * _dot(u, wu_ref[...])
    acc_ref[...] += _dot(h.astype(BF16), wd_ref[...])

    @pl.when(j == pl.num_programs(2) - 1)
    def _():
        o_ref[...] = _residual_ln(x_ref, mod_ref, 5, acc_ref[...], g_ref, b_ref)


def _ffn(x, mod, w_gate, w_up, w_down, ln_g, ln_b, seqs, rows):
    B, T, _ = x.shape
    d_ff = w_gate.shape[1]
    row_block = pl.BlockSpec((seqs, rows, D_MODEL), lambda b, t, j: (b, t, 0))
    vec = pl.BlockSpec((1, D_MODEL), lambda b, t, j: (0, 0))
    return pl.pallas_call(
        _ffn_kernel,
        grid=(B // seqs, T // rows, d_ff // FF_COLS),
        in_specs=[
            row_block,
            pl.BlockSpec((seqs, 6, D_MODEL), lambda b, t, j: (b, 0, 0)),
            pl.BlockSpec((D_MODEL, FF_COLS), lambda b, t, j: (0, j)),
            pl.BlockSpec((D_MODEL, FF_COLS), lambda b, t, j: (0, j)),
            pl.BlockSpec((FF_COLS, D_MODEL), lambda b, t, j: (j, 0)),
            vec, vec,
        ],
        out_specs=row_block,
        out_shape=jax.ShapeDtypeStruct((B, T, D_MODEL), F32),
        scratch_shapes=[pltpu.VMEM((seqs * rows, D_MODEL), BF16), pltpu.VMEM((seqs * rows, D_MODEL), F32)],
        compiler_params=_params("parallel", "parallel", "arbitrary"),
        name="ffn",
    )(x, mod, w_gate, w_up, w_down, ln_g, ln_b)


def _tile(batch, seq):
    if seq >= ROW_TILE:
        return 1, ROW_TILE
    return min(batch, ROW_TILE // seq), seq


def _layer(x, mod, attn_fn, halo_src, pos0, halo_from_p, w):
    B, T, _ = x.shape
    seqs, rows = _tile(B, T)
    q, k, v, k_bf, v_bf, p, ga, gb = _proj(x, mod, w["w_in"], seqs, rows)
    o = attn_fn(q, k_bf, v_bf)
    m = _gate(o, p, p if halo_from_p else halo_src, ga, gb, w["w_sb_out"], w["w_pool"], w["pool_scale"],
              seqs, rows, pos0, halo_from_p)
    x = _mix(m, x, mod, w["w_out"], w["ln1_g"], w["ln1_b"], seqs, rows)
    x = _ffn(x, mod, w["w_gate"], w["w_up"], w["w_down"], w["ln2_g"], w["ln2_b"], seqs, rows)
    return x, k, v, p[None, :, T - POOL_PAST:, :]


def kernel(x_prompt, x_sample, c_prompt, c_sample, cache_k, cache_v, state_pool, w_ada, b_ada, w_in, w_sb_out, w_pool, pool_scale, w_out, ln1_g, ln1_b, w_gate, w_up, w_down, ln2_g, ln2_b):
    assert w_in.shape[0] == DEPTH
    n_prompt = c_prompt.shape[0]
    mod = _adaln(jnp.concatenate([c_prompt, c_sample], axis=0), w_ada[0], b_ada[0])
    mod = mod.reshape(mod.shape[0], 6, D_MODEL)
    w = {
        "w_in": w_in[0].astype(BF16), "w_sb_out": w_sb_out[0].astype(BF16), "w_pool": w_pool[0].astype(BF16),
        "w_out": w_out[0].astype(BF16), "w_gate": w_gate[0].astype(BF16), "w_up": w_up[0].astype(BF16),
        "w_down": w_down[0].astype(BF16), "pool_scale": pool_scale, "ln1_g": ln1_g, "ln1_b": ln1_b,
        "ln2_g": ln2_g, "ln2_b": ln2_b,
    }
    attn_prompt = functools.partial(_attn_prompt, heads=N_HEADS)
    y_p, k_p, v_p, pool_p = _layer(x_prompt, mod[:n_prompt], attn_prompt, None, 0, True, w)

    pool_halo = jnp.pad(state_pool[0], ((0, 0), (POOL_HALO - POOL_PAST, 0), (0, 0)))
    attn_sample = lambda q, k, v: _attn_sample(q, k, v, cache_k, cache_v)
    y_s, k_s, v_s, pool_s = _layer(x_sample, mod[n_prompt:], attn_sample, pool_halo, cache_k.shape[3], False, w)
    return y_p, y_s, k_p, v_p, pool_p, k_s, v_s, pool_s
```

```python
import functools
import math

import jax
import jax.numpy as jnp
from jax import lax
from jax.experimental import pallas as pl
from jax.experimental.pallas import tpu as pltpu

D_MODEL = 2048
N_HEADS = 8
HEAD_DIM = 128
D_SB = N_HEADS * HEAD_DIM
POOL_WINDOWS = (2, 4, 8, 16)
N_POOL_GROUPS = len(POOL_WINDOWS)
D_POOL = 1024
POOL_GROUP = D_POOL // N_POOL_GROUPS
POOL_OUT = D_MODEL // N_POOL_GROUPS
POOL_PAST = max(POOL_WINDOWS) - 1
POOL_HALO = POOL_PAST + 1
D_IN = 3 * D_SB + D_POOL + 2 * D_MODEL
DEPTH = 1
DN_ALPHA = (2 * DEPTH) ** 0.25
LN_EPS = 1e-5

LOG2_E = math.log2(math.e)
SB_SCALE_LOG2 = HEAD_DIM ** -0.5 * LOG2_E
UNDERFLOW_LOG2 = -110.0 * LOG2_E

V7X_VMEM_BYTES = 64 * 1024 * 1024
VMEM_LIMIT = V7X_VMEM_BYTES - 8 * 1024 * 1024
LANES = 128

ROW_TILE = 512
ROW_CHUNKS = 2
PROJ_ROWS = 1024
PROJ_COLS = 512
ADA_COLS = 1024
FF_COLS = 512
Q_BLOCK = 128
PAST_BLOCK = 2 * Q_BLOCK

BF16 = jnp.bfloat16
F32 = jnp.float32


def _params(*sem):
    return pltpu.CompilerParams(dimension_semantics=sem, vmem_limit_bytes=VMEM_LIMIT)


def _dot(a, b):
    return jnp.dot(a, b, preferred_element_type=F32)


def _sigmoid(x):
    return 0.5 * jnp.tanh(0.5 * x) + 0.5


def _normalize(x):
    mu = jnp.mean(x, axis=-1, keepdims=True)
    xc = x - mu
    var = jnp.mean(xc * xc, axis=-1, keepdims=True)
    return xc * lax.rsqrt(var + LN_EPS)


def _modulated_ln(x_ref, mod_ref, shift_idx, scale_idx):
    x = x_ref[...]
    u = _normalize(x) * (1.0 + mod_ref[:, scale_idx:scale_idx + 1, :]) + mod_ref[:, shift_idx:shift_idx + 1, :]
    return u.reshape(x.shape[0] * x.shape[1], x.shape[2])


def _row_chunks(seqs, rows):
    if seqs >= ROW_CHUNKS:
        n = seqs // ROW_CHUNKS
        return [(slice(i * n, (i + 1) * n), slice(0, rows)) for i in range(ROW_CHUNKS)]
    n = rows // ROW_CHUNKS
    return [(slice(0, seqs), slice(i * n, (i + 1) * n)) for i in range(ROW_CHUNKS)]


def _residual_ln(x_ref, mod_ref, gate_idx, branch, g_ref, b_ref):
    x = x_ref[...]
    y = DN_ALPHA * x + mod_ref[:, gate_idx:gate_idx + 1, :] * branch.reshape(x.shape)
    return _normalize(y) * g_ref[...] + b_ref[...]


def _adaln_kernel(c_ref, w_ref, b_ref, o_ref):
    a = jax.nn.silu(c_ref[...]).astype(BF16)
    o_ref[...] = _dot(a, w_ref[...].astype(BF16)) + b_ref[...]


def _adaln(c, w_ada, b_ada):
    n = c.shape[0]
    cols = w_ada.shape[1]
    return pl.pallas_call(
        _adaln_kernel,
        grid=(cols // ADA_COLS,),
        in_specs=[
            pl.BlockSpec((n, D_MODEL), lambda j: (0, 0)),
            pl.BlockSpec((D_MODEL, ADA_COLS), lambda j: (0, j)),
            pl.BlockSpec((1, ADA_COLS), lambda j: (0, j)),
        ],
        out_specs=pl.BlockSpec((n, ADA_COLS), lambda j: (0, j)),
        out_shape=jax.ShapeDtypeStruct((n, cols), F32),
        compiler_params=_params("arbitrary"),
        name="adaln",
    )(c, w_ada, b_ada.reshape(1, cols))


def _proj_parts():
    parts, first = {}, 0
    for name, width in (("q", D_SB), ("k", D_SB), ("v", D_SB), ("p", D_POOL), ("ga", D_MODEL), ("gb", D_MODEL)):
        parts[name] = (first, width // PROJ_COLS)
        first += width // PROJ_COLS
    return parts


def _proj_kernel(x_ref, mod_ref, w_ref, q_ref, k_ref, v_ref, kb_ref, vb_ref, p_ref, ga_ref, gb_ref, u_ref):
    j = pl.program_id(2)
    seqs, rows = x_ref.shape[0], x_ref.shape[1]
    parts = _proj_parts()

    @pl.when(j == 0)
    def _():
        u_ref[...] = _modulated_ln(x_ref, mod_ref, 0, 1).astype(BF16)

    def columns():
        return _dot(u_ref[...], w_ref[...])

    def owns(name):
        first, count = parts[name]
        return (j >= first) & (j < first + count)

    def store_heads(ref, bf_ref):
        res = columns()
        for s in range(seqs):
            for h in range(PROJ_COLS // HEAD_DIM):
                ref[s, h] = res[s * rows:(s + 1) * rows, h * HEAD_DIM:(h + 1) * HEAD_DIM]
        bf_ref[...] = res.astype(BF16).reshape(bf_ref.shape)

    @pl.when(owns("q"))
    def _():
        q_ref[...] = columns().astype(BF16).reshape(q_ref.shape)

    @pl.when(owns("k"))
    def _():
        store_heads(k_ref, kb_ref)

    @pl.when(owns("v"))
    def _():
        store_heads(v_ref, vb_ref)

    @pl.when(owns("p"))
    def _():
        p_ref[...] = columns().reshape(p_ref.shape)

    @pl.when(owns("ga"))
    def _():
        ga_ref[...] = columns().astype(BF16).reshape(ga_ref.shape)

    @pl.when(owns("gb"))
    def _():
        gb_ref[...] = columns().astype(BF16).reshape(gb_ref.shape)


def _proj(x, mod, w_in, seqs, rows):
    B, T, _ = x.shape
    grid = (B // seqs, T // rows, D_IN // PROJ_COLS)
    parts = _proj_parts()

    def tile(name):
        first, count = parts[name]
        return lambda j: jnp.clip(j - first, 0, count - 1)

    def rows_spec(name):
        t_of = tile(name)
        return pl.BlockSpec((seqs, rows, PROJ_COLS), lambda b, t, j: (b, t, t_of(j)))

    def heads_spec(name):
        t_of = tile(name)
        return pl.BlockSpec((None, seqs, PROJ_COLS // HEAD_DIM, rows, HEAD_DIM), lambda b, t, j: (0, b, t_of(j), t, 0))

    head_shape = jax.ShapeDtypeStruct((DEPTH, B, N_HEADS, T, HEAD_DIM), F32)
    return pl.pallas_call(
        _proj_kernel,
        grid=grid,
        in_specs=[
            pl.BlockSpec((seqs, rows, D_MODEL), lambda b, t, j: (b, t, 0)),
            pl.BlockSpec((seqs, 6, D_MODEL), lambda b, t, j: (b, 0, 0)),
            pl.BlockSpec((D_MODEL, PROJ_COLS), lambda b, t, j: (0, j)),
        ],
        out_specs=[
            rows_spec("q"), heads_spec("k"), heads_spec("v"), rows_spec("k"), rows_spec("v"),
            rows_spec("p"), rows_spec("ga"), rows_spec("gb"),
        ],
        out_shape=[
            jax.ShapeDtypeStruct((B, T, D_SB), BF16),
            head_shape,
            head_shape,
            jax.ShapeDtypeStruct((B, T, D_SB), BF16),
            jax.ShapeDtypeStruct((B, T, D_SB), BF16),
            jax.ShapeDtypeStruct((B, T, D_POOL), F32),
            jax.ShapeDtypeStruct((B, T, D_MODEL), BF16),
            jax.ShapeDtypeStruct((B, T, D_MODEL), BF16),
        ],
        scratch_shapes=[pltpu.VMEM((seqs * rows, D_MODEL), BF16)],
        compiler_params=_params("parallel", "parallel", "arbitrary"),
        name="proj",
    )(x, mod, w_in)


def _suffix_sum_matrix(n):
    r = lax.broadcasted_iota(jnp.int32, (n, LANES + n), 0)
    c = lax.broadcasted_iota(jnp.int32, (n, LANES + n), 1)
    return jnp.where((c < LANES) | (r > c - LANES), 1.0, 0.0).astype(BF16)


def _sb_blocks(qs, k_blks, v_blks, sums, carry_ref, acc_ref, diagonal):
    heads = len(qs)
    tq, n = qs[0].shape[0], k_blks[0].shape[0]
    if diagonal:
        row = lax.broadcasted_iota(jnp.int32, (tq, n), 0)
        col = lax.broadcasted_iota(jnp.int32, (tq, n), 1)
        mask = col < row
    zs = [lax.dot_general(q, k, (((1,), (1,)), ((), ())), preferred_element_type=F32) * SB_SCALE_LOG2
          for q, k in zip(qs, k_blks)]
    log_betas, stacked = [], []
    for z in zs:
        nz = -z
        softplus_tail = jnp.log2(1.0 + jnp.exp2(jnp.minimum(z, nz)))
        log_betas.append(jnp.minimum(z, 0.0) - softplus_tail)
        log_1mb = jnp.minimum(nz, 0.0) - softplus_tail
        if diagonal:
            log_1mb = jnp.where(mask, log_1mb, 0.0)
        hi = log_1mb.astype(BF16)
        lo = (log_1mb - hi.astype(F32)).astype(BF16)
        stacked.append(jnp.concatenate([hi, lo], axis=0))
    rs = [_dot(s, sums) for s in stacked]
    ws = []
    for h in range(heads):
        r = rs[h][:tq] + rs[h][tq:]
        total, after = r[:, :LANES], r[:, LANES:]
        carry = carry_ref[h]
        w = []
        for c in range(0, n, LANES):
            width = min(LANES, n - c)
            w.append(jnp.exp2(log_betas[h][:, c:c + width] + after[:, c:c + width] + carry[:, :width]))
        w = w[0] if len(w) == 1 else jnp.concatenate(w, axis=1)
        if diagonal:
            w = jnp.where(mask, w, 0.0)
        ws.append(w.astype(BF16))
        carry_ref[h] = carry + total
    pvs = [_dot(w, v) for w, v in zip(ws, v_blks)]
    for h in range(heads):
        acc_ref[h] += pvs[h]


def _walk_past(n_blocks, carry_ref, wide_fn, narrow_fn):
    def cond(state):
        j, bound = state
        return (j >= 2) & (bound > UNDERFLOW_LOG2)

    def body(state):
        j, _ = state
        wide_fn(pl.multiple_of((j - 2) * Q_BLOCK, Q_BLOCK))
        return j - 2, jnp.max(carry_ref[...])

    j, bound = lax.while_loop(cond, body, (n_blocks, jnp.float32(0.0)))
    if narrow_fn is not None:
        pl.when((j == 1) & (bound > UNDERFLOW_LOG2))(narrow_fn)


def _head_cols(h):
    return slice(h * HEAD_DIM, (h + 1) * HEAD_DIM)


def _attn_prompt_kernel(q_ref, k_ref, v_ref, o_ref, carry_ref, acc_ref):
    qi = pl.program_id(2)
    heads = q_ref.shape[1] // HEAD_DIM
    carry_ref[...] = jnp.zeros_like(carry_ref)
    acc_ref[...] = jnp.zeros_like(acc_ref)

    def visit(start, n, sums, diagonal):
        rows = pl.ds(start, n)
        hs = [_head_cols(h) for h in range(heads)]
        _sb_blocks([q_ref[:, c] for c in hs], [k_ref[rows, c] for c in hs], [v_ref[rows, c] for c in hs], sums,
                   carry_ref, acc_ref, diagonal)

    visit(pl.multiple_of(qi * Q_BLOCK, Q_BLOCK), Q_BLOCK, _suffix_sum_matrix(Q_BLOCK), True)
    wide_sums = _suffix_sum_matrix(PAST_BLOCK)
    _walk_past(qi, carry_ref,
               lambda start: visit(start, PAST_BLOCK, wide_sums, False),
               lambda: visit(0, Q_BLOCK, _suffix_sum_matrix(Q_BLOCK), False))
    for h in range(heads):
        o_ref[:, _head_cols(h)] = acc_ref[h].astype(o_ref.dtype)


def _attn_prompt(q, k, v, heads):
    B, T, _ = q.shape
    width = heads * HEAD_DIM
    q_spec = pl.BlockSpec((None, Q_BLOCK, width), lambda b, g, i: (b, i, g))
    kv_spec = pl.BlockSpec((None, T, width), lambda b, g, i: (b, 0, g))
    return pl.pallas_call(
        _attn_prompt_kernel,
        grid=(B, N_HEADS // heads, T // Q_BLOCK),
        in_specs=[q_spec, kv_spec, kv_spec],
        out_specs=q_spec,
        out_shape=jax.ShapeDtypeStruct((B, T, D_SB), BF16),
        scratch_shapes=[pltpu.VMEM((heads, Q_BLOCK, LANES), F32), pltpu.VMEM((heads, Q_BLOCK, HEAD_DIM), F32)],
        compiler_params=_params("parallel", "parallel", "arbitrary"),
        name="attn_prompt",
    )(q, k, v)


def _attn_sample_kernel(q_ref, k_ref, v_ref, ck_ref, cv_ref, o_ref, carry_ref, acc_ref):
    tq = q_ref.shape[0]
    carry_ref[...] = jnp.zeros_like(carry_ref)
    acc_ref[...] = jnp.zeros_like(acc_ref)
    hs = [_head_cols(h) for h in range(N_HEADS)]
    qs = [q_ref[:, c] for c in hs]
    _sb_blocks(qs, [k_ref[:, c] for c in hs], [v_ref[:, c] for c in hs], _suffix_sum_matrix(tq),
               carry_ref, acc_ref, True)
    wide_sums = _suffix_sum_matrix(PAST_BLOCK)

    def visit(start):
        rows = pl.ds(start, PAST_BLOCK)
        _sb_blocks(qs, [ck_ref[h, rows, :].astype(BF16) for h in range(N_HEADS)],
                   [cv_ref[h, rows, :].astype(BF16) for h in range(N_HEADS)], wide_sums, carry_ref, acc_ref, False)

    past = ck_ref.shape[1]
    assert past % PAST_BLOCK == 0
    _walk_past(past // Q_BLOCK, carry_ref, visit, None)
    for h in range(N_HEADS):
        o_ref[:, _head_cols(h)] = acc_ref[h].astype(o_ref.dtype)


def _attn_sample(q, k, v, cache_k, cache_v):
    B, T, _ = q.shape
    past = cache_k.shape[3]
    new_spec = pl.BlockSpec((None, T, D_SB), lambda b: (b, 0, 0))
    cache_spec = pl.BlockSpec((None, None, N_HEADS, past, HEAD_DIM), lambda b: (0, b, 0, 0, 0))
    return pl.pallas_call(
        _attn_sample_kernel,
        grid=(B,),
        in_specs=[new_spec, new_spec, new_spec, cache_spec, cache_spec],
        out_specs=new_spec,
        out_shape=jax.ShapeDtypeStruct((B, T, D_SB), BF16),
        scratch_shapes=[pltpu.VMEM((N_HEADS, T, LANES), F32), pltpu.VMEM((N_HEADS, T, HEAD_DIM), F32)],
        compiler_params=_params("parallel"),
        name="attn_sample",
    )(q, k, v, cache_k, cache_v)


def _gate_kernel(o_ref, p_ref, halo_ref, ga_ref, gb_ref, wsb_ref, wpool_ref, pscale_ref, m_ref,
                 *, pos0, zero_first_halo):
    seqs, rows = p_ref.shape[0], p_ref.shape[1]
    t0 = pl.program_id(1) * rows
    o = o_ref[...].reshape(seqs * rows, D_SB)
    halo = halo_ref[...]
    if zero_first_halo:
        halo = jnp.where(t0 == 0, 0.0, halo)
    pos = pos0 + t0 + lax.broadcasted_iota(jnp.int32, (rows, POOL_GROUP), 0)

    for g, win in enumerate(POOL_WINDOWS):
        cols = slice(g * POOL_GROUP, (g + 1) * POOL_GROUP)
        out_cols = slice(g * POOL_OUT, (g + 1) * POOL_OUT)
        cnt = jnp.minimum(win, pos + 1).astype(F32)
        diffs = []
        for s in range(seqs):
            p = p_ref[s, :, cols]
            acc = jnp.concatenate([halo[s, :, cols], p], axis=0)
            shift = 1
            while shift < win:
                acc = acc + pltpu.roll(acc, shift, 0)
                shift *= 2
            diffs.append(acc[POOL_HALO:, :] / cnt - p)
        d = jnp.concatenate(diffs, axis=0).astype(BF16)
        y_a = _dot(o, wsb_ref[:, out_cols])
        y_b = _dot(d, wpool_ref[g]) * pscale_ref[:, out_cols]
        gate_a = _sigmoid(ga_ref[:, :, out_cols].astype(F32)).reshape(seqs * rows, POOL_OUT)
        gate_b = _sigmoid(gb_ref[:, :, out_cols].astype(F32)).reshape(seqs * rows, POOL_OUT)
        m_ref[:, :, out_cols] = (gate_a * y_a + gate_b * y_b).astype(BF16).reshape(seqs, rows, POOL_OUT)


def _gate(o, p, halo_src, ga, gb, w_sb_out, w_pool, pool_scale, seqs, rows, pos0, halo_from_p):
    B, T, _ = p.shape
    row_block = lambda width: pl.BlockSpec((seqs, rows, width), lambda b, t: (b, t, 0))
    if halo_from_p:
        per_tile = rows // POOL_HALO
        halo_spec = pl.BlockSpec((seqs, POOL_HALO, D_POOL),
                                 lambda b, t: (b, jnp.maximum(t * per_tile - 1, 0), 0))
    else:
        halo_spec = pl.BlockSpec((seqs, POOL_HALO, D_POOL), lambda b, t: (b, 0, 0))
    const = lambda shape: pl.BlockSpec(shape, lambda b, t: (0,) * len(shape))
    return pl.pallas_call(
        functools.partial(_gate_kernel, pos0=pos0, zero_first_halo=halo_from_p),
        grid=(B // seqs, T // rows),
        in_specs=[
            row_block(D_SB), row_block(D_POOL), halo_spec, row_block(D_MODEL), row_block(D_MODEL),
            const((D_SB, D_MODEL)), const((N_POOL_GROUPS, POOL_GROUP, POOL_OUT)), const((1, D_MODEL)),
        ],
        out_specs=row_block(D_MODEL),
        out_shape=jax.ShapeDtypeStruct((B, T, D_MODEL), BF16),
        compiler_params=_params("parallel", "arbitrary"),
        name="gate",
    )(o, p, halo_src, ga, gb, w_sb_out, w_pool, pool_scale)


def _mix_kernel(m_ref, x_ref, mod_ref, w_ref, g_ref, b_ref, o_ref):
    seqs, rows = x_ref.shape[0], x_ref.shape[1]
    for s, r in _row_chunks(seqs, rows):
        mix = _dot(m_ref[s, r, :].reshape(-1, D_MODEL), w_ref[...])
        o_ref[s, r, :] = _residual_ln(x_ref.at[s, r, :], mod_ref.at[s], 2, mix, g_ref, b_ref)


def _mix(m, x, mod, w_out, ln_g, ln_b, seqs, rows):
    B, T, _ = x.shape
    row_block = pl.BlockSpec((seqs, rows, D_MODEL), lambda b, t: (b, t, 0))
    vec = pl.BlockSpec((1, D_MODEL), lambda b, t: (0, 0))
    return pl.pallas_call(
        _mix_kernel,
        grid=(B // seqs, T // rows),
        in_specs=[
            row_block, row_block,
            pl.BlockSpec((seqs, 6, D_MODEL), lambda b, t: (b, 0, 0)),
            pl.BlockSpec((D_MODEL, D_MODEL), lambda b, t: (0, 0)),
            vec, vec,
        ],
        out_specs=row_block,
        out_shape=jax.ShapeDtypeStruct((B, T, D_MODEL), F32),
        compiler_params=_params("parallel", "arbitrary"),
        name="mix",
    )(m, x, mod, w_out, ln_g, ln_b)


def _ffn_kernel(x_ref, mod_ref, wg_ref, wu_ref, wd_ref, g_ref, b_ref, o_ref, u_ref, acc_ref):
    j = pl.program_id(2)

    @pl.when(j == 0)
    def _():
        u_ref[...] = _modulated_ln(x_ref, mod_ref, 3, 4).astype(BF16)
        acc_ref[...] = jnp.zeros_like(acc_ref)

    u = u_ref[...]
    h = jax.nn.silu(_dot(u, wg_ref[...])) * _dot(u, wu_ref[...])
    acc_ref[...] += _dot(h.astype(BF16), wd_ref[...])

    @pl.when(j == pl.num_programs(2) - 1)
    def _():
        o_ref[...] = _residual_ln(x_ref, mod_ref, 5, acc_ref[...], g_ref, b_ref)


def _ffn(x, mod, w_gate, w_up, w_down, ln_g, ln_b, seqs, rows):
    B, T, _ = x.shape
    d_ff = w_gate.shape[1]
    row_block = pl.BlockSpec((seqs, rows, D_MODEL), lambda b, t, j: (b, t, 0))
    vec = pl.BlockSpec((1, D_MODEL), lambda b, t, j: (0, 0))
    return pl.pallas_call(
        _ffn_kernel,
        grid=(B // seqs, T // rows, d_ff // FF_COLS),
        in_specs=[
            row_block,
            pl.BlockSpec((seqs, 6, D_MODEL), lambda b, t, j: (b, 0, 0)),
            pl.BlockSpec((D_MODEL, FF_COLS), lambda b, t, j: (0, j)),
            pl.BlockSpec((D_MODEL, FF_COLS), lambda b, t, j: (0, j)),
            pl.BlockSpec((FF_COLS, D_MODEL), lambda b, t, j: (j, 0)),
            vec, vec,
        ],
        out_specs=row_block,
        out_shape=jax.ShapeDtypeStruct((B, T, D_MODEL), F32),
        scratch_shapes=[pltpu.VMEM((seqs * rows, D_MODEL), BF16), pltpu.VMEM((seqs * rows, D_MODEL), F32)],
        compiler_params=_params("parallel", "parallel", "arbitrary"),
        name="ffn",
    )(x, mod, w_gate, w_up, w_down, ln_g, ln_b)


def _tile(batch, seq, row_tile=ROW_TILE):
    if seq >= row_tile:
        return 1, row_tile
    return min(batch, row_tile // seq), seq


def _layer(x, mod, attn_fn, halo_src, pos0, halo_from_p, w):
    B, T, _ = x.shape
    seqs, rows = _tile(B, T)
    q, k, v, k_bf, v_bf, p, ga, gb = _proj(x, mod, w["w_in"], *_tile(B, T, PROJ_ROWS))
    o = attn_fn(q, k_bf, v_bf)
    m = _gate(o, p, p if halo_from_p else halo_src, ga, gb, w["w_sb_out"], w["w_pool"], w["pool_scale"],
              seqs, rows, pos0, halo_from_p)
    x = _mix(m, x, mod, w["w_out"], w["ln1_g"], w["ln1_b"], seqs, rows)
    x = _ffn(x, mod, w["w_gate"], w["w_up"], w["w_down"], w["ln2_g"], w["ln2_b"], seqs, rows)
    return x, k, v, p[None, :, T - POOL_PAST:, :]


def kernel(x_prompt, x_sample, c_prompt, c_sample, cache_k, cache_v, state_pool, w_ada, b_ada, w_in, w_sb_out, w_pool, pool_scale, w_out, ln1_g, ln1_b, w_gate, w_up, w_down, ln2_g, ln2_b):
    assert w_in.shape[0] == DEPTH
    n_prompt = c_prompt.shape[0]
    mod = _adaln(jnp.concatenate([c_prompt, c_sample], axis=0), w_ada[0], b_ada[0])
    mod = mod.reshape(mod.shape[0], 6, D_MODEL)
    w = {
        "w_in": w_in[0].astype(BF16), "w_sb_out": w_sb_out[0].astype(BF16), "w_pool": w_pool[0].astype(BF16),
        "w_out": w_out[0].astype(BF16), "w_gate": w_gate[0].astype(BF16), "w_up": w_up[0].astype(BF16),
        "w_down": w_down[0].astype(BF16), "pool_scale": pool_scale, "ln1_g": ln1_g, "ln1_b": ln1_b,
        "ln2_g": ln2_g, "ln2_b": ln2_b,
    }
    attn_prompt = functools.partial(_attn_prompt, heads=N_HEADS)
    y_p, k_p, v_p, pool_p = _layer(x_prompt, mod[:n_prompt], attn_prompt, None, 0, True, w)

    pool_halo = jnp.pad(state_pool[0], ((0, 0), (POOL_HALO - POOL_PAST, 0), (0, 0)))
    attn_sample = lambda q, k, v: _attn_sample(q, k, v, cache_k, cache_v)
    y_s, k_s, v_s, pool_s = _layer(x_sample, mod[n_prompt:], attn_sample, pool_halo, cache_k.shape[3], False, w)
    return y_p, y_s, k_p, v_p, pool_p, k_s, v_s, pool_s
```

```python
import functools
import math

import jax
import jax.numpy as jnp
from jax import lax
from jax.experimental import pallas as pl
from jax.experimental.pallas import tpu as pltpu

D_MODEL = 2048
N_HEADS = 8
HEAD_DIM = 128
D_SB = N_HEADS * HEAD_DIM
POOL_WINDOWS = (2, 4, 8, 16)
N_POOL_GROUPS = len(POOL_WINDOWS)
D_POOL = 1024
POOL_GROUP = D_POOL // N_POOL_GROUPS
POOL_OUT = D_MODEL // N_POOL_GROUPS
POOL_PAST = max(POOL_WINDOWS) - 1
POOL_HALO = POOL_PAST + 1
D_IN = 3 * D_SB + D_POOL + 2 * D_MODEL
DEPTH = 1
DN_ALPHA = (2 * DEPTH) ** 0.25
LN_EPS = 1e-5

LOG2_E = math.log2(math.e)
SB_SCALE_LOG2 = HEAD_DIM ** -0.5 * LOG2_E
UNDERFLOW_LOG2 = -110.0 * LOG2_E

V7X_VMEM_BYTES = 64 * 1024 * 1024
VMEM_LIMIT = V7X_VMEM_BYTES - 6 * 1024 * 1024
LANES = 128

ROW_TILE = 512
ROW_CHUNKS = 2
PROJ_COLS = 2048
ADA_COLS = 1024
FF_COLS = 512
Q_BLOCK = 128
PAST_BLOCK = 2 * Q_BLOCK

BF16 = jnp.bfloat16
F32 = jnp.float32


def _params(*sem):
    return pltpu.CompilerParams(dimension_semantics=sem, vmem_limit_bytes=VMEM_LIMIT)


def _dot(a, b):
    return jnp.dot(a, b, preferred_element_type=F32)


def _normalize(x):
    mu = jnp.mean(x, axis=-1, keepdims=True)
    xc = x - mu
    var = jnp.mean(xc * xc, axis=-1, keepdims=True)
    return xc * lax.rsqrt(var + LN_EPS)


def _modulated_ln(x_ref, mod_ref, shift_idx, scale_idx):
    x = x_ref[...]
    u = _normalize(x) * (1.0 + mod_ref[:, scale_idx:scale_idx + 1, :]) + mod_ref[:, shift_idx:shift_idx + 1, :]
    return u.reshape(x.shape[0] * x.shape[1], x.shape[2])


def _row_chunks(seqs, rows):
    if seqs >= ROW_CHUNKS:
        n = seqs // ROW_CHUNKS
        return [(slice(i * n, (i + 1) * n), slice(0, rows)) for i in range(ROW_CHUNKS)]
    n = rows // ROW_CHUNKS
    return [(slice(0, seqs), slice(i * n, (i + 1) * n)) for i in range(ROW_CHUNKS)]


def _residual_ln(x_ref, mod_ref, gate_idx, branch, g_ref, b_ref):
    x = x_ref[...]
    y = DN_ALPHA * x + mod_ref[:, gate_idx:gate_idx + 1, :] * branch.reshape(x.shape)
    return _normalize(y) * g_ref[...] + b_ref[...]


def _adaln_kernel(c_ref, w_ref, b_ref, o_ref):
    a = jax.nn.silu(c_ref[...]).astype(BF16)
    o_ref[...] = _dot(a, w_ref[...].astype(BF16)) + b_ref[...]


def _adaln(c, w_ada, b_ada):
    n = c.shape[0]
    cols = w_ada.shape[1]
    return pl.pallas_call(
        _adaln_kernel,
        grid=(cols // ADA_COLS,),
        in_specs=[
            pl.BlockSpec((n, D_MODEL), lambda j: (0, 0)),
            pl.BlockSpec((D_MODEL, ADA_COLS), lambda j: (0, j)),
            pl.BlockSpec((1, ADA_COLS), lambda j: (0, j)),
        ],
        out_specs=pl.BlockSpec((n, ADA_COLS), lambda j: (0, j)),
        out_shape=jax.ShapeDtypeStruct((n, cols), F32),
        compiler_params=_params("arbitrary"),
        name="adaln",
    )(c, w_ada, b_ada.reshape(1, cols))


_QK_COL, _VP_COL, _GA_COL, _GB_COL = range(4)


def _proj_kernel(x_ref, mod_ref, w_ref, q_ref, k_ref, v_ref, kb_ref, vb_ref, p_ref, ga_ref, gb_ref, u_ref):
    j = pl.program_id(2)
    seqs, rows = x_ref.shape[0], x_ref.shape[1]

    @pl.when(j == 0)
    def _():
        u_ref[...] = _modulated_ln(x_ref, mod_ref, 0, 1).astype(BF16)

    def columns():
        return _dot(u_ref[...], w_ref[...])

    def store_heads(ref, bf_ref, res):
        for s in range(seqs):
            for h in range(N_HEADS):
                ref[s, h] = res[s * rows:(s + 1) * rows, h * HEAD_DIM:(h + 1) * HEAD_DIM]
        bf_ref[...] = res.astype(BF16).reshape(bf_ref.shape)

    @pl.when(j == _QK_COL)
    def _():
        res = columns()
        q_ref[...] = res[:, :D_SB].astype(BF16).reshape(q_ref.shape)
        store_heads(k_ref, kb_ref, res[:, D_SB:])

    @pl.when(j == _VP_COL)
    def _():
        res = columns()
        store_heads(v_ref, vb_ref, res[:, :D_SB])
        p_ref[...] = res[:, D_SB:].reshape(p_ref.shape)

    @pl.when(j == _GA_COL)
    def _():
        ga_ref[...] = columns().astype(BF16).reshape(ga_ref.shape)

    @pl.when(j == _GB_COL)
    def _():
        gb_ref[...] = columns().astype(BF16).reshape(gb_ref.shape)


def _proj(x, mod, w_in, seqs, rows):
    B, T, _ = x.shape
    assert 2 * D_SB == D_SB + D_POOL == D_MODEL == PROJ_COLS
    rows_spec = lambda width: pl.BlockSpec((seqs, rows, width), lambda b, t, j: (b, t, 0))
    heads_spec = pl.BlockSpec((None, seqs, N_HEADS, rows, HEAD_DIM), lambda b, t, j: (0, b, 0, t, 0))
    head_shape = jax.ShapeDtypeStruct((DEPTH, B, N_HEADS, T, HEAD_DIM), F32)
    return pl.pallas_call(
        _proj_kernel,
        grid=(B // seqs, T // rows, D_IN // PROJ_COLS),
        in_specs=[
            rows_spec(D_MODEL),
            pl.BlockSpec((seqs, 6, D_MODEL), lambda b, t, j: (b, 0, 0)),
            pl.BlockSpec((D_MODEL, PROJ_COLS), lambda b, t, j: (0, j)),
        ],
        out_specs=[
            rows_spec(D_SB), heads_spec, heads_spec, rows_spec(D_SB), rows_spec(D_SB),
            rows_spec(D_POOL), rows_spec(D_MODEL), rows_spec(D_MODEL),
        ],
        out_shape=[
            jax.ShapeDtypeStruct((B, T, D_SB), BF16),
            head_shape,
            head_shape,
            jax.ShapeDtypeStruct((B, T, D_SB), BF16),
            jax.ShapeDtypeStruct((B, T, D_SB), BF16),
            jax.ShapeDtypeStruct((B, T, D_POOL), F32),
            jax.ShapeDtypeStruct((B, T, D_MODEL), BF16),
            jax.ShapeDtypeStruct((B, T, D_MODEL), BF16),
        ],
        scratch_shapes=[pltpu.VMEM((seqs * rows, D_MODEL), BF16)],
        compiler_params=_params("parallel", "parallel", "arbitrary"),
        name="proj",
    )(x, mod, w_in)


def _suffix_sum_matrix(n):
    r = lax.broadcasted_iota(jnp.int32, (n, n), 0)
    c = lax.broadcasted_iota(jnp.int32, (n, n), 1)
    return jnp.where(r >= c, 1.0, 0.0).astype(BF16)


def _sb_blocks(qs, k_blks, v_blks, sums, carry_ref, acc_ref, diagonal):
    heads = len(qs)
    tq, n = qs[0].shape[0], k_blks[0].shape[0]
    if diagonal:
        row = lax.broadcasted_iota(jnp.int32, (tq, n), 0)
        col = lax.broadcasted_iota(jnp.int32, (tq, n), 1)
        mask = col < row
    zs = [lax.dot_general(q, k, (((1,), (1,)), ((), ())), preferred_element_type=F32) * SB_SCALE_LOG2
          for q, k in zip(qs, k_blks)]
    terms = []
    for z in zs:
        softplus_tail = jnp.log2(1.0 + jnp.exp2(jnp.minimum(z, -z)))
        log_1mb = jnp.minimum(z, 0.0) - softplus_tail - z
        if diagonal:
            log_1mb = jnp.where(mask, log_1mb, 0.0)
        hi = log_1mb.astype(BF16)
        terms += [hi, (log_1mb - hi.astype(F32)).astype(BF16)]
    sums_out = _dot(jnp.concatenate(terms, axis=0), sums)
    ws = []
    for h in range(heads):
        incl = sums_out[2 * h * tq:(2 * h + 1) * tq] + sums_out[(2 * h + 1) * tq:(2 * h + 2) * tq]
        carry = carry_ref[h]
        w = []
        for c in range(0, n, LANES):
            width = min(LANES, n - c)
            w.append(jnp.exp2(zs[h][:, c:c + width] + incl[:, c:c + width] + carry[:, :width]))
        w = w[0] if len(w) == 1 else jnp.concatenate(w, axis=1)
        if diagonal:
            w = jnp.where(mask, w, 0.0)
        ws.append(w.astype(BF16))
        carry_ref[h] = carry + jnp.broadcast_to(incl[:, :1], carry.shape)
    pvs = [_dot(w, v) for w, v in zip(ws, v_blks)]
    for h in range(heads):
        acc_ref[h] += pvs[h]


def _walk_past(n_blocks, carry_ref, wide_fn, narrow_fn):
    def cond(state):
        j, bound = state
        return (j >= 2) & (bound > UNDERFLOW_LOG2)

    def body(state):
        j, _ = state
        wide_fn(pl.multiple_of((j - 2) * Q_BLOCK, Q_BLOCK))
        return j - 2, jnp.max(carry_ref[...])

    j, bound = lax.while_loop(cond, body, (n_blocks, jnp.float32(0.0)))
    if narrow_fn is not None:
        pl.when((j == 1) & (bound > UNDERFLOW_LOG2))(narrow_fn)


def _head_cols(h):
    return slice(h * HEAD_DIM, (h + 1) * HEAD_DIM)


def _attn_prompt_kernel(q_ref, k_ref, v_ref, o_ref, carry_ref, acc_ref):
    qi = pl.program_id(2)
    heads = q_ref.shape[1] // HEAD_DIM
    carry_ref[...] = jnp.zeros_like(carry_ref)
    acc_ref[...] = jnp.zeros_like(acc_ref)

    def visit(start, n, sums, diagonal):
        rows = pl.ds(start, n)
        hs = [_head_cols(h) for h in range(heads)]
        _sb_blocks([q_ref[:, c] for c in hs], [k_ref[rows, c] for c in hs], [v_ref[rows, c] for c in hs], sums,
                   carry_ref, acc_ref, diagonal)

    visit(pl.multiple_of(qi * Q_BLOCK, Q_BLOCK), Q_BLOCK, _suffix_sum_matrix(Q_BLOCK), True)
    wide_sums = _suffix_sum_matrix(PAST_BLOCK)
    _walk_past(qi, carry_ref,
               lambda start: visit(start, PAST_BLOCK, wide_sums, False),
               lambda: visit(0, Q_BLOCK, _suffix_sum_matrix(Q_BLOCK), False))
    for h in range(heads):
        o_ref[:, _head_cols(h)] = acc_ref[h].astype(o_ref.dtype)


def _attn_prompt(q, k, v, heads):
    B, T, _ = q.shape
    width = heads * HEAD_DIM
    q_spec = pl.BlockSpec((None, Q_BLOCK, width), lambda b, g, i: (b, i, g))
    kv_spec = pl.BlockSpec((None, T, width), lambda b, g, i: (b, 0, g))
    return pl.pallas_call(
        _attn_prompt_kernel,
        grid=(B, N_HEADS // heads, T // Q_BLOCK),
        in_specs=[q_spec, kv_spec, kv_spec],
        out_specs=q_spec,
        out_shape=jax.ShapeDtypeStruct((B, T, D_SB), BF16),
        scratch_shapes=[pltpu.VMEM((heads, Q_BLOCK, LANES), F32), pltpu.VMEM((heads, Q_BLOCK, HEAD_DIM), F32)],
        compiler_params=_params("parallel", "parallel", "arbitrary"),
        name="attn_prompt",
    )(q, k, v)


def _attn_sample_kernel(q_ref, k_ref, v_ref, ck_ref, cv_ref, o_ref, carry_ref, acc_ref):
    tq = q_ref.shape[0]
    carry_ref[...] = jnp.zeros_like(carry_ref)
    acc_ref[...] = jnp.zeros_like(acc_ref)
    hs = [_head_cols(h) for h in range(N_HEADS)]
    qs = [q_ref[:, c] for c in hs]
    _sb_blocks(qs, [k_ref[:, c] for c in hs], [v_ref[:, c] for c in hs], _suffix_sum_matrix(tq),
               carry_ref, acc_ref, True)
    wide_sums = _suffix_sum_matrix(PAST_BLOCK)

    def visit(start):
        rows = pl.ds(start, PAST_BLOCK)
        _sb_blocks(qs, [ck_ref[h, rows, :].astype(BF16) for h in range(N_HEADS)],
                   [cv_ref[h, rows, :].astype(BF16) for h in range(N_HEADS)], wide_sums, carry_ref, acc_ref, False)

    past = ck_ref.shape[1]
    assert past % PAST_BLOCK == 0
    _walk_past(past // Q_BLOCK, carry_ref, visit, None)
    for h in range(N_HEADS):
        o_ref[:, _head_cols(h)] = acc_ref[h].astype(o_ref.dtype)


def _attn_sample(q, k, v, cache_k, cache_v):
    B, T, _ = q.shape
    past = cache_k.shape[3]
    new_spec = pl.BlockSpec((None, T, D_SB), lambda b: (b, 0, 0))
    cache_spec = pl.BlockSpec((None, None, N_HEADS, past, HEAD_DIM), lambda b: (0, b, 0, 0, 0))
    return pl.pallas_call(
        _attn_sample_kernel,
        grid=(B,),
        in_specs=[new_spec, new_spec, new_spec, cache_spec, cache_spec],
        out_specs=new_spec,
        out_shape=jax.ShapeDtypeStruct((B, T, D_SB), BF16),
        scratch_shapes=[pltpu.VMEM((N_HEADS, T, LANES), F32), pltpu.VMEM((N_HEADS, T, HEAD_DIM), F32)],
        compiler_params=_params("parallel"),
        name="attn_sample",
    )(q, k, v, cache_k, cache_v)


def _gate_kernel(o_ref, p_ref, halo_ref, ga_ref, gb_ref, wsb_ref, wpool_ref, pscale_ref, m_ref,
                 *, pos0, zero_first_halo):
    seqs, rows = p_ref.shape[0], p_ref.shape[1]
    t0 = pl.program_id(1) * rows
    y_a = _dot(o_ref[...].reshape(seqs * rows, D_SB), wsb_ref[...])

    halo = halo_ref[...]
    if zero_first_halo:
        halo = jnp.where(t0 == 0, 0.0, halo)
    pos = pos0 + t0 + lax.broadcasted_iota(jnp.int32, (rows, POOL_GROUP), 0)

    y_b = []
    for g, win in enumerate(POOL_WINDOWS):
        cols = slice(g * POOL_GROUP, (g + 1) * POOL_GROUP)
        cnt = jnp.minimum(win, pos + 1).astype(F32)
        diffs = []
        for s in range(seqs):
            p = p_ref[s, :, cols]
            acc = jnp.concatenate([halo[s, :, cols], p], axis=0)
            shift = 1
            while shift < win:
                acc = acc + pltpu.roll(acc, shift, 0)
                shift *= 2
            diffs.append(acc[POOL_HALO:, :] / cnt - p)
        d = jnp.concatenate(diffs, axis=0).astype(BF16)
        y_b.append(_dot(d, wpool_ref[g]))
    y_b = jnp.concatenate(y_b, axis=1) * pscale_ref[...]

    gate_a = jax.nn.sigmoid(ga_ref[...].astype(F32)).reshape(seqs * rows, D_MODEL)
    gate_b = jax.nn.sigmoid(gb_ref[...].astype(F32)).reshape(seqs * rows, D_MODEL)
    m_ref[...] = (gate_a * y_a + gate_b * y_b).astype(BF16).reshape(m_ref.shape)


def _gate(o, p, halo_src, ga, gb, w_sb_out, w_pool, pool_scale, seqs, rows, pos0, halo_from_p):
    B, T, _ = p.shape
    row_block = lambda width: pl.BlockSpec((seqs, rows, width), lambda b, t: (b, t, 0))
    if halo_from_p:
        per_tile = rows // POOL_HALO
        halo_spec = pl.BlockSpec((seqs, POOL_HALO, D_POOL),
                                 lambda b, t: (b, jnp.maximum(t * per_tile - 1, 0), 0))
    else:
        halo_spec = pl.BlockSpec((seqs, POOL_HALO, D_POOL), lambda b, t: (b, 0, 0))
    const = lambda shape: pl.BlockSpec(shape, lambda b, t: (0,) * len(shape))
    return pl.pallas_call(
        functools.partial(_gate_kernel, pos0=pos0, zero_first_halo=halo_from_p),
        grid=(B // seqs, T // rows),
        in_specs=[
            row_block(D_SB), row_block(D_POOL), halo_spec, row_block(D_MODEL), row_block(D_MODEL),
            const((D_SB, D_MODEL)), const((N_POOL_GROUPS, POOL_GROUP, POOL_OUT)), const((1, D_MODEL)),
        ],
        out_specs=row_block(D_MODEL),
        out_shape=jax.ShapeDtypeStruct((B, T, D_MODEL), BF16),
        compiler_params=_params("parallel", "arbitrary"),
        name="gate",
    )(o, p, halo_src, ga, gb, w_sb_out, w_pool, pool_scale)


def _mix_kernel(m_ref, x_ref, mod_ref, w_ref, g_ref, b_ref, o_ref):
    seqs, rows = x_ref.shape[0], x_ref.shape[1]
    for s, r in _row_chunks(seqs, rows):
        mix = _dot(m_ref[s, r, :].reshape(-1, D_MODEL), w_ref[...])
        o_ref[s, r, :] = _residual_ln(x_ref.at[s, r, :], mod_ref.at[s], 2, mix, g_ref, b_ref)


def _mix(m, x, mod, w_out, ln_g, ln_b, seqs, rows):
    B, T, _ = x.shape
    row_block = pl.BlockSpec((seqs, rows, D_MODEL), lambda b, t: (b, t, 0))
    vec = pl.BlockSpec((1, D_MODEL), lambda b, t: (0, 0))
    return pl.pallas_call(
        _mix_kernel,
        grid=(B // seqs, T // rows),
        in_specs=[
            row_block, row_block,
            pl.BlockSpec((seqs, 6, D_MODEL), lambda b, t: (b, 0, 0)),
            pl.BlockSpec((D_MODEL, D_MODEL), lambda b, t: (0, 0)),
            vec, vec,
        ],
        out_specs=row_block,
        out_shape=jax.ShapeDtypeStruct((B, T, D_MODEL), F32),
        compiler_params=_params("parallel", "arbitrary"),
        name="mix",
    )(m, x, mod, w_out, ln_g, ln_b)


def _ffn_kernel(x_ref, mod_ref, wg_ref, wu_ref, wd_ref, g_ref, b_ref, o_ref, u_ref, acc_ref):
    j = pl.program_id(2)

    @pl.when(j == 0)
    def _():
        u_ref[...] = _modulated_ln(x_ref, mod_ref, 3, 4).astype(BF16)
        acc_ref[...] = jnp.zeros_like(acc_ref)

    u = u_ref[...]
    h = jax.nn.silu(_dot(u, wg_ref[...])) * _dot(u, wu_ref[...])
    acc_ref[...] += _dot(h.astype(BF16), wd_ref[...])

    @pl.when(j == pl.num_programs(2) - 1)
    def _():
        o_ref[...] = _residual_ln(x_ref, mod_ref, 5, acc_ref[...], g_ref, b_ref)


def _ffn(x, mod, w_gate, w_up, w_down, ln_g, ln_b, seqs, rows):
    B, T, _ = x.shape
    d_ff = w_gate.shape[1]
    row_block = pl.BlockSpec((seqs, rows, D_MODEL), lambda b, t, j: (b, t, 0))
    vec = pl.BlockSpec((1, D_MODEL), lambda b, t, j: (0, 0))
    return pl.pallas_call(
        _ffn_kernel,
        grid=(B // seqs, T // rows, d_ff // FF_COLS),
        in_specs=[
            row_block,
            pl.BlockSpec((seqs, 6, D_MODEL), lambda b, t, j: (b, 0, 0)),
            pl.BlockSpec((D_MODEL, FF_COLS), lambda b, t, j: (0, j)),
            pl.BlockSpec((D_MODEL, FF_COLS), lambda b, t, j: (0, j)),
            pl.BlockSpec((FF_COLS, D_MODEL), lambda b, t, j: (j, 0)),
            vec, vec,
        ],
        out_specs=row_block,
        out_shape=jax.ShapeDtypeStruct((B, T, D_MODEL), F32),
        scratch_shapes=[pltpu.VMEM((seqs * rows, D_MODEL), BF16), pltpu.VMEM((seqs * rows, D_MODEL), F32)],
        compiler_params=_params("parallel", "parallel", "arbitrary"),
        name="ffn",
    )(x, mod, w_gate, w_up, w_down, ln_g, ln_b)


def _tile(batch, seq, row_tile=ROW_TILE):
    if seq >= row_tile:
        return 1, row_tile
    return min(batch, row_tile // seq), seq


def _layer(x, mod, attn_fn, halo_src, pos0, halo_from_p, w):
    B, T, _ = x.shape
    seqs, rows = _tile(B, T)
    q, k, v, k_bf, v_bf, p, ga, gb = _proj(x, mod, w["w_in"], seqs, rows)
    o = attn_fn(q, k_bf, v_bf)
    m = _gate(o, p, p if halo_from_p else halo_src, ga, gb, w["w_sb_out"], w["w_pool"], w["pool_scale"],
              seqs, rows, pos0, halo_from_p)
    x = _mix(m, x, mod, w["w_out"], w["ln1_g"], w["ln1_b"], seqs, rows)
    x = _ffn(x, mod, w["w_gate"], w["w_up"], w["w_down"], w["ln2_g"], w["ln2_b"], seqs, rows)
    return x, k, v, p[None, :, T - POOL_PAST:, :]


def kernel(x_prompt, x_sample, c_prompt, c_sample, cache_k, cache_v, state_pool, w_ada, b_ada, w_in, w_sb_out, w_pool, pool_scale, w_out, ln1_g, ln1_b, w_gate, w_up, w_down, ln2_g, ln2_b):
    assert w_in.shape[0] == DEPTH
    n_prompt = c_prompt.shape[0]
    mod = _adaln(jnp.concatenate([c_prompt, c_sample], axis=0), w_ada[0], b_ada[0])
    mod = mod.reshape(mod.shape[0], 6, D_MODEL)
    w = {
        "w_in": w_in[0].astype(BF16), "w_sb_out": w_sb_out[0].astype(BF16), "w_pool": w_pool[0].astype(BF16),
        "w_out": w_out[0].astype(BF16), "w_gate": w_gate[0].astype(BF16), "w_up": w_up[0].astype(BF16),
        "w_down": w_down[0].astype(BF16), "pool_scale": pool_scale, "ln1_g": ln1_g, "ln1_b": ln1_b,
        "ln2_g": ln2_g, "ln2_b": ln2_b,
    }
    attn_prompt = functools.partial(_attn_prompt, heads=N_HEADS)
    y_p, k_p, v_p, pool_p = _layer(x_prompt, mod[:n_prompt], attn_prompt, None, 0, True, w)

    pool_halo = jnp.pad(state_pool[0], ((0, 0), (POOL_HALO - POOL_PAST, 0), (0, 0)))
    attn_sample = lambda q, k, v: _attn_sample(q, k, v, cache_k, cache_v)
    y_s, k_s, v_s, pool_s = _layer(x_sample, mod[n_prompt:], attn_sample, pool_halo, cache_k.shape[3], False, w)
    return y_p, y_s, k_p, v_p, pool_p, k_s, v_s, pool_s
```

```python
import functools
import math

import jax
import jax.numpy as jnp
from jax import lax
from jax.experimental import pallas as pl
from jax.experimental.pallas import tpu as pltpu

D_MODEL = 2048
N_HEADS = 8
HEAD_DIM = 128
D_SB = N_HEADS * HEAD_DIM
POOL_WINDOWS = (2, 4, 8, 16)
N_POOL_GROUPS = len(POOL_WINDOWS)
D_POOL = 1024
POOL_GROUP = D_POOL // N_POOL_GROUPS
POOL_OUT = D_MODEL // N_POOL_GROUPS
POOL_PAST = max(POOL_WINDOWS) - 1
POOL_HALO = POOL_PAST + 1
D_IN = 3 * D_SB + D_POOL + 2 * D_MODEL
DEPTH = 1
DN_ALPHA = (2 * DEPTH) ** 0.25
LN_EPS = 1e-5

LOG2_E = math.log2(math.e)
SB_SCALE_LOG2 = HEAD_DIM ** -0.5 * LOG2_E
UNDERFLOW_LOG2 = -110.0 * LOG2_E

V7X_VMEM_BYTES = 64 * 1024 * 1024
VMEM_LIMIT = V7X_VMEM_BYTES - 6 * 1024 * 1024
LANES = 128

ROW_TILE = 512
ROW_CHUNKS = 2
PROJ_COLS = 2048
ADA_COLS = 1024
FF_COLS = 512
Q_BLOCK = 128
PAST_BLOCK = 2 * Q_BLOCK

BF16 = jnp.bfloat16
F32 = jnp.float32


def _params(*sem):
    return pltpu.CompilerParams(dimension_semantics=sem, vmem_limit_bytes=VMEM_LIMIT)


def _dot(a, b):
    return jnp.dot(a, b, preferred_element_type=F32)


def _normalize(x):
    mu = jnp.mean(x, axis=-1, keepdims=True)
    xc = x - mu
    var = jnp.mean(xc * xc, axis=-1, keepdims=True)
    return xc * lax.rsqrt(var + LN_EPS)


def _modulated_ln(x_ref, mod_ref, shift_idx, scale_idx):
    x = x_ref[...]
    u = _normalize(x) * (1.0 + mod_ref[:, scale_idx:scale_idx + 1, :]) + mod_ref[:, shift_idx:shift_idx + 1, :]
    return u.reshape(x.shape[0] * x.shape[1], x.shape[2])


def _row_chunks(seqs, rows):
    if seqs >= ROW_CHUNKS:
        n = seqs // ROW_CHUNKS
        return [(slice(i * n, (i + 1) * n), slice(0, rows)) for i in range(ROW_CHUNKS)]
    n = rows // ROW_CHUNKS
    return [(slice(0, seqs), slice(i * n, (i + 1) * n)) for i in range(ROW_CHUNKS)]


def _residual_ln(x_ref, mod_ref, gate_idx, branch, g_ref, b_ref):
    x = x_ref[...]
    y = DN_ALPHA * x + mod_ref[:, gate_idx:gate_idx + 1, :] * branch.reshape(x.shape)
    return _normalize(y) * g_ref[...] + b_ref[...]


def _adaln_kernel(c_ref, w_ref, b_ref, o_ref):
    a = jax.nn.silu(c_ref[...]).astype(BF16)
    o_ref[...] = _dot(a, w_ref[...].astype(BF16)) + b_ref[...]


def _adaln(c, w_ada, b_ada):
    n = c.shape[0]
    cols = w_ada.shape[1]
    return pl.pallas_call(
        _adaln_kernel,
        grid=(cols // ADA_COLS,),
        in_specs=[
            pl.BlockSpec((n, D_MODEL), lambda j: (0, 0)),
            pl.BlockSpec((D_MODEL, ADA_COLS), lambda j: (0, j)),
            pl.BlockSpec((1, ADA_COLS), lambda j: (0, j)),
        ],
        out_specs=pl.BlockSpec((n, ADA_COLS), lambda j: (0, j)),
        out_shape=jax.ShapeDtypeStruct((n, cols), F32),
        compiler_params=_params("arbitrary"),
        name="adaln",
    )(c, w_ada, b_ada.reshape(1, cols))


_QK_COL, _VP_COL, _GA_COL, _GB_COL = range(4)


def _proj_kernel(x_ref, mod_ref, w_ref, q_ref, k_ref, v_ref, kb_ref, vb_ref, p_ref, ga_ref, gb_ref, u_ref):
    j = pl.program_id(2)
    seqs, rows = x_ref.shape[0], x_ref.shape[1]

    @pl.when(j == 0)
    def _():
        u_ref[...] = _modulated_ln(x_ref, mod_ref, 0, 1).astype(BF16)

    def columns():
        return _dot(u_ref[...], w_ref[...])

    def store_heads(ref, bf_ref, res):
        for s in range(seqs):
            for h in range(N_HEADS):
                ref[s, h] = res[s * rows:(s + 1) * rows, h * HEAD_DIM:(h + 1) * HEAD_DIM]
        bf_ref[...] = res.astype(BF16).reshape(bf_ref.shape)

    @pl.when(j == _QK_COL)
    def _():
        res = columns()
        q_ref[...] = res[:, :D_SB].astype(BF16).reshape(q_ref.shape)
        store_heads(k_ref, kb_ref, res[:, D_SB:])

    @pl.when(j == _VP_COL)
    def _():
        res = columns()
        store_heads(v_ref, vb_ref, res[:, :D_SB])
        p_ref[...] = res[:, D_SB:].reshape(p_ref.shape)

    @pl.when(j == _GA_COL)
    def _():
        ga_ref[...] = columns().astype(BF16).reshape(ga_ref.shape)

    @pl.when(j == _GB_COL)
    def _():
        gb_ref[...] = columns().astype(BF16).reshape(gb_ref.shape)


def _proj(x, mod, w_in, seqs, rows):
    B, T, _ = x.shape
    assert 2 * D_SB == D_SB + D_POOL == D_MODEL == PROJ_COLS
    rows_spec = lambda width: pl.BlockSpec((seqs, rows, width), lambda b, t, j: (b, t, 0))
    heads_spec = pl.BlockSpec((None, seqs, N_HEADS, rows, HEAD_DIM), lambda b, t, j: (0, b, 0, t, 0))
    head_shape = jax.ShapeDtypeStruct((DEPTH, B, N_HEADS, T, HEAD_DIM), F32)
    return pl.pallas_call(
        _proj_kernel,
        grid=(B // seqs, T // rows, D_IN // PROJ_COLS),
        in_specs=[
            rows_spec(D_MODEL),
            pl.BlockSpec((seqs, 6, D_MODEL), lambda b, t, j: (b, 0, 0)),
            pl.BlockSpec((D_MODEL, PROJ_COLS), lambda b, t, j: (0, j)),
        ],
        out_specs=[
            rows_spec(D_SB), heads_spec, heads_spec, rows_spec(D_SB), rows_spec(D_SB),
            rows_spec(D_POOL), rows_spec(D_MODEL), rows_spec(D_MODEL),
        ],
        out_shape=[
            jax.ShapeDtypeStruct((B, T, D_SB), BF16),
            head_shape,
            head_shape,
            jax.ShapeDtypeStruct((B, T, D_SB), BF16),
            jax.ShapeDtypeStruct((B, T, D_SB), BF16),
            jax.ShapeDtypeStruct((B, T, D_POOL), F32),
            jax.ShapeDtypeStruct((B, T, D_MODEL), BF16),
            jax.ShapeDtypeStruct((B, T, D_MODEL), BF16),
        ],
        scratch_shapes=[pltpu.VMEM((seqs * rows, D_MODEL), BF16)],
        compiler_params=_params("parallel", "parallel", "arbitrary"),
        name="proj",
    )(x, mod, w_in)


def _suffix_sum_matrix(n):
    r = lax.broadcasted_iota(jnp.int32, (n, n), 0)
    c = lax.broadcasted_iota(jnp.int32, (n, n), 1)
    return jnp.where(r >= c, 1.0, 0.0).astype(BF16)


def _sb_blocks(qs, k_blks, v_blks, sums, carry_ref, acc_ref, diagonal):
    heads = len(qs)
    tq, n = qs[0].shape[0], k_blks[0].shape[0]
    if diagonal:
        row = lax.broadcasted_iota(jnp.int32, (tq, n), 0)
        col = lax.broadcasted_iota(jnp.int32, (tq, n), 1)
        mask = col < row
    zs = [lax.dot_general(q, k, (((1,), (1,)), ((), ())), preferred_element_type=F32) * SB_SCALE_LOG2
          for q, k in zip(qs, k_blks)]
    terms = []
    for z in zs:
        softplus_tail = jnp.log2(1.0 + jnp.exp2(jnp.minimum(z, -z)))
        log_1mb = jnp.minimum(z, 0.0) - softplus_tail - z
        if diagonal:
            log_1mb = jnp.where(mask, log_1mb, 0.0)
        hi = log_1mb.astype(BF16)
        terms += [hi, (log_1mb - hi.astype(F32)).astype(BF16)]
    sums_out = _dot(jnp.concatenate(terms, axis=0), sums)
    ws = []
    for h in range(heads):
        incl = sums_out[2 * h * tq:(2 * h + 1) * tq] + sums_out[(2 * h + 1) * tq:(2 * h + 2) * tq]
        carry = carry_ref[h]
        w = []
        for c in range(0, n, LANES):
            width = min(LANES, n - c)
            w.append(jnp.exp2(zs[h][:, c:c + width] + incl[:, c:c + width] + carry[:, :width]))
        w = w[0] if len(w) == 1 else jnp.concatenate(w, axis=1)
        if diagonal:
            w = jnp.where(mask, w, 0.0)
        ws.append(w.astype(BF16))
        carry_ref[h] = carry + jnp.broadcast_to(incl[:, :1], carry.shape)
    pvs = [_dot(w, v) for w, v in zip(ws, v_blks)]
    for h in range(heads):
        acc_ref[h] += pvs[h]


def _walk_past(n_blocks, carry_ref, wide_fn, narrow_fn):
    def cond(state):
        j, bound = state
        return (j >= 2) & (bound > UNDERFLOW_LOG2)

    def body(state):
        j, _ = state
        wide_fn(pl.multiple_of((j - 2) * Q_BLOCK, Q_BLOCK))
        return j - 2, jnp.max(carry_ref[...])

    j, bound = lax.while_loop(cond, body, (n_blocks, jnp.float32(0.0)))
    if narrow_fn is not None:
        pl.when((j == 1) & (bound > UNDERFLOW_LOG2))(narrow_fn)


def _head_cols(h):
    return slice(h * HEAD_DIM, (h + 1) * HEAD_DIM)


def _attn_prompt_kernel(q_ref, k_ref, v_ref, o_ref, carry_ref, acc_ref):
    qi = pl.program_id(2)
    heads = q_ref.shape[1] // HEAD_DIM
    carry_ref[...] = jnp.zeros_like(carry_ref)
    acc_ref[...] = jnp.zeros_like(acc_ref)

    def visit(start, n, sums, diagonal):
        rows = pl.ds(start, n)
        hs = [_head_cols(h) for h in range(heads)]
        _sb_blocks([q_ref[:, c] for c in hs], [k_ref[rows, c] for c in hs], [v_ref[rows, c] for c in hs], sums,
                   carry_ref, acc_ref, diagonal)

    visit(pl.multiple_of(qi * Q_BLOCK, Q_BLOCK), Q_BLOCK, _suffix_sum_matrix(Q_BLOCK), True)
    wide_sums = _suffix_sum_matrix(PAST_BLOCK)
    _walk_past(qi, carry_ref,
               lambda start: visit(start, PAST_BLOCK, wide_sums, False),
               lambda: visit(0, Q_BLOCK, _suffix_sum_matrix(Q_BLOCK), False))
    for h in range(heads):
        o_ref[:, _head_cols(h)] = acc_ref[h].astype(o_ref.dtype)


def _attn_prompt(q, k, v, heads):
    B, T, _ = q.shape
    width = heads * HEAD_DIM
    q_spec = pl.BlockSpec((None, Q_BLOCK, width), lambda b, g, i: (b, i, g))
    kv_spec = pl.BlockSpec((None, T, width), lambda b, g, i: (b, 0, g))
    return pl.pallas_call(
        _attn_prompt_kernel,
        grid=(B, N_HEADS // heads, T // Q_BLOCK),
        in_specs=[q_spec, kv_spec, kv_spec],
        out_specs=q_spec,
        out_shape=jax.ShapeDtypeStruct((B, T, D_SB), BF16),
        scratch_shapes=[pltpu.VMEM((heads, Q_BLOCK, LANES), F32), pltpu.VMEM((heads, Q_BLOCK, HEAD_DIM), F32)],
        compiler_params=_params("parallel", "parallel", "arbitrary"),
        name="attn_prompt",
    )(q, k, v)


def _attn_sample_kernel(q_ref, k_ref, v_ref, ck_ref, cv_ref, o_ref, carry_ref, acc_ref):
    tq = q_ref.shape[0]
    carry_ref[...] = jnp.zeros_like(carry_ref)
    acc_ref[...] = jnp.zeros_like(acc_ref)
    hs = [_head_cols(h) for h in range(N_HEADS)]
    qs = [q_ref[:, c] for c in hs]
    _sb_blocks(qs, [k_ref[:, c] for c in hs], [v_ref[:, c] for c in hs], _suffix_sum_matrix(tq),
               carry_ref, acc_ref, True)
    wide_sums = _suffix_sum_matrix(PAST_BLOCK)

    def visit(start):
        rows = pl.ds(start, PAST_BLOCK)
        _sb_blocks(qs, [ck_ref[h, rows, :].astype(BF16) for h in range(N_HEADS)],
                   [cv_ref[h, rows, :].astype(BF16) for h in range(N_HEADS)], wide_sums, carry_ref, acc_ref, False)

    past = ck_ref.shape[1]
    assert past % PAST_BLOCK == 0
    _walk_past(past // Q_BLOCK, carry_ref, visit, None)
    for h in range(N_HEADS):
        o_ref[:, _head_cols(h)] = acc_ref[h].astype(o_ref.dtype)


def _attn_sample(q, k, v, cache_k, cache_v):
    B, T, _ = q.shape
    past = cache_k.shape[3]
    new_spec = pl.BlockSpec((None, T, D_SB), lambda b: (b, 0, 0))
    cache_spec = pl.BlockSpec((None, None, N_HEADS, past, HEAD_DIM), lambda b: (0, b, 0, 0, 0))
    return pl.pallas_call(
        _attn_sample_kernel,
        grid=(B,),
        in_specs=[new_spec, new_spec, new_spec, cache_spec, cache_spec],
        out_specs=new_spec,
        out_shape=jax.ShapeDtypeStruct((B, T, D_SB), BF16),
        scratch_shapes=[pltpu.VMEM((N_HEADS, T, LANES), F32), pltpu.VMEM((N_HEADS, T, HEAD_DIM), F32)],
        compiler_params=_params("parallel"),
        name="attn_sample",
    )(q, k, v, cache_k, cache_v)


def _gate_kernel(o_ref, p_ref, halo_ref, ga_ref, gb_ref, wsb_ref, wpool_ref, pscale_ref, m_ref,
                 *, pos0, zero_first_halo):
    seqs, rows = p_ref.shape[0], p_ref.shape[1]
    t0 = pl.program_id(1) * rows
    y_a = _dot(o_ref[...].reshape(seqs * rows, D_SB), wsb_ref[...])

    halo = halo_ref[...]
    if zero_first_halo:
        halo = jnp.where(t0 == 0, 0.0, halo)
    pos = pos0 + t0 + lax.broadcasted_iota(jnp.int32, (rows, POOL_GROUP), 0)

    y_b = []
    for g, win in enumerate(POOL_WINDOWS):
        cols = slice(g * POOL_GROUP, (g + 1) * POOL_GROUP)
        cnt = jnp.minimum(win, pos + 1).astype(F32)
        diffs = []
        for s in range(seqs):
            p = p_ref[s, :, cols]
            acc = jnp.concatenate([halo[s, :, cols], p], axis=0)
            shift = 1
            while shift < win:
                acc = acc + pltpu.roll(acc, shift, 0)
                shift *= 2
            diffs.append(acc[POOL_HALO:, :] / cnt - p)
        d = jnp.concatenate(diffs, axis=0).astype(BF16)
        y_b.append(_dot(d, wpool_ref[g]))
    y_b = jnp.concatenate(y_b, axis=1) * pscale_ref[...]

    gate_a = jax.nn.sigmoid(ga_ref[...].astype(F32)).reshape(seqs * rows, D_MODEL)
    gate_b = jax.nn.sigmoid(gb_ref[...].astype(F32)).reshape(seqs * rows, D_MODEL)
    m_ref[...] = (gate_a * y_a + gate_b * y_b).astype(BF16).reshape(m_ref.shape)


def _gate(o, p, halo_src, ga, gb, w_sb_out, w_pool, pool_scale, seqs, rows, pos0, halo_from_p):
    B, T, _ = p.shape
    row_block = lambda width: pl.BlockSpec((seqs, rows, width), lambda b, t: (b, t, 0))
    if halo_from_p:
        per_tile = rows // POOL_HALO
        halo_spec = pl.BlockSpec((seqs, POOL_HALO, D_POOL),
                                 lambda b, t: (b, jnp.maximum(t * per_tile - 1, 0), 0))
    else:
        halo_spec = pl.BlockSpec((seqs, POOL_HALO, D_POOL), lambda b, t: (b, 0, 0))
    const = lambda shape: pl.BlockSpec(shape, lambda b, t: (0,) * len(shape))
    return pl.pallas_call(
        functools.partial(_gate_kernel, pos0=pos0, zero_first_halo=halo_from_p),
        grid=(B // seqs, T // rows),
        in_specs=[
            row_block(D_SB), row_block(D_POOL), halo_spec, row_block(D_MODEL), row_block(D_MODEL),
            const((D_SB, D_MODEL)), const((N_POOL_GROUPS, POOL_GROUP, POOL_OUT)), const((1, D_MODEL)),
        ],
        out_specs=row_block(D_MODEL),
        out_shape=jax.ShapeDtypeStruct((B, T, D_MODEL), BF16),
        compiler_params=_params("parallel", "arbitrary"),
        name="gate",
    )(o, p, halo_src, ga, gb, w_sb_out, w_pool, pool_scale)


def _mix_kernel(m_ref, x_ref, mod_ref, w_ref, g_ref, b_ref, o_ref):
    seqs, rows = x_ref.shape[0], x_ref.shape[1]
    for s, r in _row_chunks(seqs, rows):
        mix = _dot(m_ref[s, r, :].reshape(-1, D_MODEL), w_ref[...])
        o_ref[s, r, :] = _residual_ln(x_ref.at[s, r, :], mod_ref.at[s], 2, mix, g_ref, b_ref)


def _mix(m, x, mod, w_out, ln_g, ln_b, seqs, rows):
    B, T, _ = x.shape
    row_block = pl.BlockSpec((seqs, rows, D_MODEL), lambda b, t: (b, t, 0))
    vec = pl.BlockSpec((1, D_MODEL), lambda b, t: (0, 0))
    return pl.pallas_call(
        _mix_kernel,
        grid=(B // seqs, T // rows),
        in_specs=[
            row_block, row_block,
            pl.BlockSpec((seqs, 6, D_MODEL), lambda b, t: (b, 0, 0)),
            pl.BlockSpec((D_MODEL, D_MODEL), lambda b, t: (0, 0)),
            vec, vec,
        ],
        out_specs=row_block,
        out_shape=jax.ShapeDtypeStruct((B, T, D_MODEL), F32),
        compiler_params=_params("parallel", "arbitrary"),
        name="mix",
    )(m, x, mod, w_out, ln_g, ln_b)


def _ffn_kernel(x_ref, mod_ref, wgu_ref, wd_ref, g_ref, b_ref, o_ref, u_ref, acc_ref):
    j = pl.program_id(2)

    @pl.when(j == 0)
    def _():
        u_ref[...] = _modulated_ln(x_ref, mod_ref, 3, 4).astype(BF16)
        acc_ref[...] = jnp.zeros_like(acc_ref)

    gate_up = _dot(u_ref[...], wgu_ref[...])
    h = jax.nn.silu(gate_up[:, :FF_COLS]) * gate_up[:, FF_COLS:]
    acc_ref[...] += _dot(h.astype(BF16), wd_ref[...])

    @pl.when(j == pl.num_programs(2) - 1)
    def _():
        o_ref[...] = _residual_ln(x_ref, mod_ref, 5, acc_ref[...], g_ref, b_ref)


def _stack_gate_up(w_gate, w_up):
    d, f = w_gate.shape
    chunks = lambda w: w.reshape(d, f // FF_COLS, FF_COLS)
    return jnp.concatenate([chunks(w_gate), chunks(w_up)], axis=2).transpose(1, 0, 2).astype(BF16)


def _ffn(x, mod, w_gate_up, w_down, ln_g, ln_b, seqs, rows):
    B, T, _ = x.shape
    row_block = pl.BlockSpec((seqs, rows, D_MODEL), lambda b, t, j: (b, t, 0))
    vec = pl.BlockSpec((1, D_MODEL), lambda b, t, j: (0, 0))
    return pl.pallas_call(
        _ffn_kernel,
        grid=(B // seqs, T // rows, w_gate_up.shape[0]),
        in_specs=[
            row_block,
            pl.BlockSpec((seqs, 6, D_MODEL), lambda b, t, j: (b, 0, 0)),
            pl.BlockSpec((None, D_MODEL, 2 * FF_COLS), lambda b, t, j: (j, 0, 0)),
            pl.BlockSpec((FF_COLS, D_MODEL), lambda b, t, j: (j, 0)),
            vec, vec,
        ],
        out_specs=row_block,
        out_shape=jax.ShapeDtypeStruct((B, T, D_MODEL), F32),
        scratch_shapes=[pltpu.VMEM((seqs * rows, D_MODEL), BF16), pltpu.VMEM((seqs * rows, D_MODEL), F32)],
        compiler_params=_params("parallel", "parallel", "arbitrary"),
        name="ffn",
    )(x, mod, w_gate_up, w_down, ln_g, ln_b)


def _tile(batch, seq, row_tile=ROW_TILE):
    if seq >= row_tile:
        return 1, row_tile
    return min(batch, row_tile // seq), seq


def _layer(x, mod, attn_fn, halo_src, pos0, halo_from_p, w):
    B, T, _ = x.shape
    seqs, rows = _tile(B, T)
    q, k, v, k_bf, v_bf, p, ga, gb = _proj(x, mod, w["w_in"], seqs, rows)
    o = attn_fn(q, k_bf, v_bf)
    m = _gate(o, p, p if halo_from_p else halo_src, ga, gb, w["w_sb_out"], w["w_pool"], w["pool_scale"],
              seqs, rows, pos0, halo_from_p)
    x = _mix(m, x, mod, w["w_out"], w["ln1_g"], w["ln1_b"], seqs, rows)
    x = _ffn(x, mod, w["w_gate_up"], w["w_down"], w["ln2_g"], w["ln2_b"], seqs, rows)
    return x, k, v, p[None, :, T - POOL_PAST:, :]


def kernel(x_prompt, x_sample, c_prompt, c_sample, cache_k, cache_v, state_pool, w_ada, b_ada, w_in, w_sb_out, w_pool, pool_scale, w_out, ln1_g, ln1_b, w_gate, w_up, w_down, ln2_g, ln2_b):
    assert w_in.shape[0] == DEPTH
    n_prompt = c_prompt.shape[0]
    mod = _adaln(jnp.concatenate([c_prompt, c_sample], axis=0), w_ada[0], b_ada[0])
    mod = mod.reshape(mod.shape[0], 6, D_MODEL)
    w = {
        "w_in": w_in[0].astype(BF16), "w_sb_out": w_sb_out[0].astype(BF16), "w_pool": w_pool[0].astype(BF16),
        "w_out": w_out[0].astype(BF16), "w_gate_up": _stack_gate_up(w_gate[0], w_up[0]),
        "w_down": w_down[0].astype(BF16), "pool_scale": pool_scale, "ln1_g": ln1_g, "ln1_b": ln1_b,
        "ln2_g": ln2_g, "ln2_b": ln2_b,
    }
    attn_prompt = functools.partial(_attn_prompt, heads=N_HEADS)
    y_p, k_p, v_p, pool_p = _layer(x_prompt, mod[:n_prompt], attn_prompt, None, 0, True, w)

    pool_halo = jnp.pad(state_pool[0], ((0, 0), (POOL_HALO - POOL_PAST, 0), (0, 0)))
    attn_sample = lambda q, k, v: _attn_sample(q, k, v, cache_k, cache_v)
    y_s, k_s, v_s, pool_s = _layer(x_sample, mod[n_prompt:], attn_sample, pool_halo, cache_k.shape[3], False, w)
    return y_p, y_s, k_p, v_p, pool_p, k_s, v_s, pool_s
```

```python
import functools
import math

import jax
import jax.numpy as jnp
from jax import lax
from jax.experimental import pallas as pl
from jax.experimental.pallas import tpu as pltpu

D_MODEL = 2048
N_HEADS = 8
HEAD_DIM = 128
D_SB = N_HEADS * HEAD_DIM
POOL_WINDOWS = (2, 4, 8, 16)
N_POOL_GROUPS = len(POOL_WINDOWS)
D_POOL = 1024
POOL_GROUP = D_POOL // N_POOL_GROUPS
POOL_OUT = D_MODEL // N_POOL_GROUPS
POOL_PAST = max(POOL_WINDOWS) - 1
POOL_HALO = POOL_PAST + 1
D_IN = 3 * D_SB + D_POOL + 2 * D_MODEL
DEPTH = 1
DN_ALPHA = (2 * DEPTH) ** 0.25
LN_EPS = 1e-5

LOG2_E = math.log2(math.e)
SB_SCALE_LOG2 = HEAD_DIM ** -0.5 * LOG2_E
UNDERFLOW_LOG2 = -110.0 * LOG2_E

V7X_VMEM_BYTES = 64 * 1024 * 1024
VMEM_LIMIT = V7X_VMEM_BYTES - 6 * 1024 * 1024
LANES = 128

ROW_TILE = 512
ROW_CHUNKS = 2
PROJ_COLS = 2048
ADA_COLS = 1024
FF_COLS = 512
Q_BLOCK = 128
CAST_ROWS = 16
PAST_BLOCK = 2 * Q_BLOCK

BF16 = jnp.bfloat16
F32 = jnp.float32


def _params(*sem):
    return pltpu.CompilerParams(dimension_semantics=sem, vmem_limit_bytes=VMEM_LIMIT)


def _dot(a, b):
    return jnp.dot(a, b, preferred_element_type=F32)


def _normalize(x):
    mu = jnp.mean(x, axis=-1, keepdims=True)
    xc = x - mu
    var = jnp.mean(xc * xc, axis=-1, keepdims=True)
    return xc * lax.rsqrt(var + LN_EPS)


def _modulated_ln(x_ref, mod_ref, shift_idx, scale_idx):
    x = x_ref[...]
    u = _normalize(x) * (1.0 + mod_ref[:, scale_idx:scale_idx + 1, :]) + mod_ref[:, shift_idx:shift_idx + 1, :]
    return u.reshape(x.shape[0] * x.shape[1], x.shape[2])


def _row_chunks(seqs, rows):
    if seqs >= ROW_CHUNKS:
        n = seqs // ROW_CHUNKS
        return [(slice(i * n, (i + 1) * n), slice(0, rows)) for i in range(ROW_CHUNKS)]
    n = rows // ROW_CHUNKS
    return [(slice(0, seqs), slice(i * n, (i + 1) * n)) for i in range(ROW_CHUNKS)]


def _residual_ln(x_ref, mod_ref, gate_idx, branch, g_ref, b_ref):
    x = x_ref[...]
    y = DN_ALPHA * x + mod_ref[:, gate_idx:gate_idx + 1, :] * branch.reshape(x.shape)
    return _normalize(y) * g_ref[...] + b_ref[...]


def _adaln_kernel(c_ref, w_ref, b_ref, o_ref):
    a = jax.nn.silu(c_ref[...]).astype(BF16)
    o_ref[...] = _dot(a, w_ref[...].astype(BF16)) + b_ref[...]


def _adaln(c, w_ada, b_ada):
    n = c.shape[0]
    cols = w_ada.shape[1]
    return pl.pallas_call(
        _adaln_kernel,
        grid=(cols // ADA_COLS,),
        in_specs=[
            pl.BlockSpec((n, D_MODEL), lambda j: (0, 0)),
            pl.BlockSpec((D_MODEL, ADA_COLS), lambda j: (0, j)),
            pl.BlockSpec((1, ADA_COLS), lambda j: (0, j)),
        ],
        out_specs=pl.BlockSpec((n, ADA_COLS), lambda j: (0, j)),
        out_shape=jax.ShapeDtypeStruct((n, cols), F32),
        compiler_params=_params("arbitrary"),
        name="adaln",
    )(c, w_ada, b_ada.reshape(1, cols))


_QK_COL, _VP_COL, _GA_COL, _GB_COL = range(4)


def _proj_kernel(x_ref, mod_ref, w_ref, q_ref, k_ref, v_ref, kb_ref, vb_ref, p_ref, ga_ref, gb_ref, u_ref):
    j = pl.program_id(2)
    seqs, rows = x_ref.shape[0], x_ref.shape[1]

    @pl.when(j == 0)
    def _():
        u_ref[...] = _modulated_ln(x_ref, mod_ref, 0, 1).astype(BF16)

    def columns():
        return _dot(u_ref[...], w_ref[...])

    def store_heads(ref, bf_ref, res):
        for s in range(seqs):
            for h in range(N_HEADS):
                ref[s, h] = res[s * rows:(s + 1) * rows, h * HEAD_DIM:(h + 1) * HEAD_DIM]
        bf_ref[...] = res.astype(BF16).reshape(bf_ref.shape)

    @pl.when(j == _QK_COL)
    def _():
        res = columns()
        q_ref[...] = res[:, :D_SB].astype(BF16).reshape(q_ref.shape)
        store_heads(k_ref, kb_ref, res[:, D_SB:])

    @pl.when(j == _VP_COL)
    def _():
        res = columns()
        store_heads(v_ref, vb_ref, res[:, :D_SB])
        p_ref[...] = res[:, D_SB:].reshape(p_ref.shape)

    @pl.when(j == _GA_COL)
    def _():
        ga_ref[...] = columns().astype(BF16).reshape(ga_ref.shape)

    @pl.when(j == _GB_COL)
    def _():
        gb_ref[...] = columns().astype(BF16).reshape(gb_ref.shape)


def _proj(x, mod, w_in, seqs, rows):
    B, T, _ = x.shape
    assert 2 * D_SB == D_SB + D_POOL == D_MODEL == PROJ_COLS
    rows_spec = lambda width: pl.BlockSpec((seqs, rows, width), lambda b, t, j: (b, t, 0))
    heads_spec = pl.BlockSpec((None, seqs, N_HEADS, rows, HEAD_DIM), lambda b, t, j: (0, b, 0, t, 0))
    head_shape = jax.ShapeDtypeStruct((DEPTH, B, N_HEADS, T, HEAD_DIM), F32)
    return pl.pallas_call(
        _proj_kernel,
        grid=(B // seqs, T // rows, D_IN // PROJ_COLS),
        in_specs=[
            rows_spec(D_MODEL),
            pl.BlockSpec((seqs, 6, D_MODEL), lambda b, t, j: (b, 0, 0)),
            pl.BlockSpec((D_MODEL, PROJ_COLS), lambda b, t, j: (0, j)),
        ],
        out_specs=[
            rows_spec(D_SB), heads_spec, heads_spec, rows_spec(D_SB), rows_spec(D_SB),
            rows_spec(D_POOL), rows_spec(D_MODEL), rows_spec(D_MODEL),
        ],
        out_shape=[
            jax.ShapeDtypeStruct((B, T, D_SB), BF16),
            head_shape,
            head_shape,
            jax.ShapeDtypeStruct((B, T, D_SB), BF16),
            jax.ShapeDtypeStruct((B, T, D_SB), BF16),
            jax.ShapeDtypeStruct((B, T, D_POOL), F32),
            jax.ShapeDtypeStruct((B, T, D_MODEL), BF16),
            jax.ShapeDtypeStruct((B, T, D_MODEL), BF16),
        ],
        scratch_shapes=[pltpu.VMEM((seqs * rows, D_MODEL), BF16)],
        compiler_params=_params("parallel", "parallel", "arbitrary"),
        name="proj",
    )(x, mod, w_in)


def _suffix_sum_matrix(n):
    r = lax.broadcasted_iota(jnp.int32, (n, n), 0)
    c = lax.broadcasted_iota(jnp.int32, (n, n), 1)
    return jnp.where(r >= c, -1.0, 0.0).astype(BF16)


def _sb_blocks(qs, k_blks, v_blks, sums, carry_ref, acc_ref, diagonal):
    heads = len(qs)
    tq, n = qs[0].shape[0], k_blks[0].shape[0]
    if diagonal:
        row = lax.broadcasted_iota(jnp.int32, (tq, n), 0)
        col = lax.broadcasted_iota(jnp.int32, (tq, n), 1)
        mask = col < row
    zs = [lax.dot_general(q, k, (((1,), (1,)), ((), ())), preferred_element_type=F32) * SB_SCALE_LOG2
          for q, k in zip(qs, k_blks)]
    terms = []
    for z in zs:
        neg_l = jnp.maximum(z, 0.0) + jnp.log2(1.0 + jnp.exp2(-jnp.abs(z)))
        if diagonal:
            neg_l = jnp.where(mask, neg_l, 0.0)
        hi = neg_l.astype(BF16)
        terms += [hi, (neg_l - hi.astype(F32)).astype(BF16)]
    sums_out = _dot(jnp.concatenate(terms, axis=0), sums)
    ws = []
    for h in range(heads):
        incl = sums_out[2 * h * tq:(2 * h + 1) * tq] + sums_out[(2 * h + 1) * tq:(2 * h + 2) * tq]
        carry = carry_ref[h]
        w = []
        for c in range(0, n, LANES):
            width = min(LANES, n - c)
            w.append(jnp.exp2(zs[h][:, c:c + width] + incl[:, c:c + width] + carry[:, :width]))
        w = w[0] if len(w) == 1 else jnp.concatenate(w, axis=1)
        if diagonal:
            w = jnp.where(mask, w, 0.0)
        ws.append(w.astype(BF16))
        carry_ref[h] = carry + jnp.broadcast_to(incl[:, :1], carry.shape)
    pvs = [_dot(w, v) for w, v in zip(ws, v_blks)]
    for h in range(heads):
        acc_ref[h] += pvs[h]


def _walk_past(n_blocks, carry_ref, wide_fn, narrow_fn):
    def cond(state):
        j, bound = state
        return (j >= 2) & (bound > UNDERFLOW_LOG2)

    def body(state):
        j, _ = state
        wide_fn(pl.multiple_of((j - 2) * Q_BLOCK, Q_BLOCK))
        return j - 2, jnp.max(carry_ref[...])

    j, bound = lax.while_loop(cond, body, (n_blocks, jnp.float32(0.0)))
    if narrow_fn is not None:
        pl.when((j == 1) & (bound > UNDERFLOW_LOG2))(narrow_fn)


def _head_cols(h):
    return slice(h * HEAD_DIM, (h + 1) * HEAD_DIM)


def _attn_prompt_kernel(q_ref, k_ref, v_ref, *refs, n_cast):
    srcs, o_ref, dsts = refs[:n_cast], refs[n_cast], refs[n_cast + 1:2 * n_cast + 1]
    carry_ref, acc_ref = refs[2 * n_cast + 1:2 * n_cast + 3]
    f32_bufs, bf_bufs = refs[2 * n_cast + 3:3 * n_cast + 3], refs[3 * n_cast + 3:4 * n_cast + 3]
    in_sem, out_sem = refs[4 * n_cast + 3:]
    qi = pl.program_id(1)
    step = pl.program_id(0) * pl.num_programs(1) + qi
    last = pl.num_programs(0) * pl.num_programs(1) - 1

    def cast_rows(s):
        return pl.ds(pl.multiple_of(s * CAST_ROWS, CAST_ROWS), CAST_ROWS)

    def load(k, s):
        return pltpu.make_async_copy(srcs[k].at[cast_rows(s), :], f32_bufs[k], in_sem.at[k])

    def store(k, s):
        return pltpu.make_async_copy(bf_bufs[k], dsts[k].at[cast_rows(s), :], out_sem.at[k])

    for k in range(n_cast):
        load(k, step).start()

    carry_ref[...] = jnp.zeros_like(carry_ref)
    acc_ref[...] = jnp.zeros_like(acc_ref)

    def visit(start, n, sums, diagonal):
        rows = pl.ds(start, n)
        hs = [_head_cols(h) for h in range(N_HEADS)]
        _sb_blocks([q_ref[:, c] for c in hs], [k_ref[rows, c] for c in hs], [v_ref[rows, c] for c in hs], sums,
                   carry_ref, acc_ref, diagonal)

    visit(pl.multiple_of(qi * Q_BLOCK, Q_BLOCK), Q_BLOCK, _suffix_sum_matrix(Q_BLOCK), True)
    wide_sums = _suffix_sum_matrix(PAST_BLOCK)
    _walk_past(qi, carry_ref,
               lambda start: visit(start, PAST_BLOCK, wide_sums, False),
               lambda: visit(0, Q_BLOCK, _suffix_sum_matrix(Q_BLOCK), False))
    for h in range(N_HEADS):
        o_ref[:, _head_cols(h)] = acc_ref[h].astype(o_ref.dtype)

    for k in range(n_cast):
        load(k, step).wait()

    @pl.when(step > 0)
    def _():
        for k in range(n_cast):
            store(k, step - 1).wait()

    for k in range(n_cast):
        bf_bufs[k][...] = f32_bufs[k][...].astype(BF16)
        store(k, step).start()

    @pl.when(step == last)
    def _():
        for k in range(n_cast):
            store(k, step).wait()


def _attn_prompt(q, k, v, cast_srcs):
    B, T, _ = q.shape
    steps = B * (T // Q_BLOCK)
    for s in cast_srcs:
        assert s.ndim == 2 and s.shape[0] == steps * CAST_ROWS, s.shape
    n_cast = len(cast_srcs)
    q_spec = pl.BlockSpec((None, Q_BLOCK, D_SB), lambda b, i: (b, i, 0))
    kv_spec = pl.BlockSpec((None, T, D_SB), lambda b, i: (b, 0, 0))
    hbm = pl.BlockSpec(memory_space=pl.ANY)
    widths = [s.shape[1] for s in cast_srcs]
    out = pl.pallas_call(
        functools.partial(_attn_prompt_kernel, n_cast=n_cast),
        grid=(B, T // Q_BLOCK),
        in_specs=[q_spec, kv_spec, kv_spec] + [hbm] * n_cast,
        out_specs=[q_spec] + [hbm] * n_cast,
        out_shape=[jax.ShapeDtypeStruct((B, T, D_SB), BF16)] + [jax.ShapeDtypeStruct(s.shape, BF16) for s in cast_srcs],
        scratch_shapes=(
            [pltpu.VMEM((N_HEADS, Q_BLOCK, LANES), F32), pltpu.VMEM((N_HEADS, Q_BLOCK, HEAD_DIM), F32)]
            + [pltpu.VMEM((CAST_ROWS, c), F32) for c in widths]
            + [pltpu.VMEM((CAST_ROWS, c), BF16) for c in widths]
            + [pltpu.SemaphoreType.DMA((n_cast,)), pltpu.SemaphoreType.DMA((n_cast,))]),
        compiler_params=_params("arbitrary", "arbitrary"),
        name="attn_prompt",
    )(q, k, v, *cast_srcs)
    return out[0], out[1:]


def _attn_sample_kernel(q_ref, k_ref, v_ref, ck_ref, cv_ref, o_ref, carry_ref, acc_ref):
    tq = q_ref.shape[0]
    carry_ref[...] = jnp.zeros_like(carry_ref)
    acc_ref[...] = jnp.zeros_like(acc_ref)
    hs = [_head_cols(h) for h in range(N_HEADS)]
    qs = [q_ref[:, c] for c in hs]
    _sb_blocks(qs, [k_ref[:, c] for c in hs], [v_ref[:, c] for c in hs], _suffix_sum_matrix(tq),
               carry_ref, acc_ref, True)
    wide_sums = _suffix_sum_matrix(PAST_BLOCK)

    def visit(start):
        rows = pl.ds(start, PAST_BLOCK)
        _sb_blocks(qs, [ck_ref[h, rows, :].astype(BF16) for h in range(N_HEADS)],
                   [cv_ref[h, rows, :].astype(BF16) for h in range(N_HEADS)], wide_sums, carry_ref, acc_ref, False)

    past = ck_ref.shape[1]
    assert past % PAST_BLOCK == 0
    _walk_past(past // Q_BLOCK, carry_ref, visit, None)
    for h in range(N_HEADS):
        o_ref[:, _head_cols(h)] = acc_ref[h].astype(o_ref.dtype)


def _attn_sample(q, k, v, cache_k, cache_v):
    B, T, _ = q.shape
    past = cache_k.shape[3]
    new_spec = pl.BlockSpec((None, T, D_SB), lambda b: (b, 0, 0))
    cache_spec = pl.BlockSpec((None, None, N_HEADS, past, HEAD_DIM), lambda b: (0, b, 0, 0, 0))
    return pl.pallas_call(
        _attn_sample_kernel,
        grid=(B,),
        in_specs=[new_spec, new_spec, new_spec, cache_spec, cache_spec],
        out_specs=new_spec,
        out_shape=jax.ShapeDtypeStruct((B, T, D_SB), BF16),
        scratch_shapes=[pltpu.VMEM((N_HEADS, T, LANES), F32), pltpu.VMEM((N_HEADS, T, HEAD_DIM), F32)],
        compiler_params=_params("parallel"),
        name="attn_sample",
    )(q, k, v, cache_k, cache_v)


def _gate_kernel(o_ref, p_ref, halo_ref, ga_ref, gb_ref, wsb_ref, wpool_ref, pscale_ref, m_ref,
                 *, pos0, zero_first_halo):
    seqs, rows = p_ref.shape[0], p_ref.shape[1]
    t0 = pl.program_id(1) * rows
    y_a = _dot(o_ref[...].reshape(seqs * rows, D_SB), wsb_ref[...])

    halo = halo_ref[...]
    if zero_first_halo:
        halo = jnp.where(t0 == 0, 0.0, halo)
    pos = pos0 + t0 + lax.broadcasted_iota(jnp.int32, (rows, LANES), 0)

    y_b = []
    for g, win in enumerate(POOL_WINDOWS):
        cols = slice(g * POOL_GROUP, (g + 1) * POOL_GROUP)
        inv_cnt = 1.0 / jnp.minimum(win, pos + 1).astype(F32)
        inv_cnt = jnp.concatenate([inv_cnt] * (POOL_GROUP // LANES), axis=1)
        diffs = []
        for s in range(seqs):
            p = p_ref[s, :, cols]
            acc = jnp.concatenate([halo[s, :, cols], p], axis=0)
            shift = 1
            while shift < win:
                acc = acc + pltpu.roll(acc, shift, 0)
                shift *= 2
            diffs.append(acc[POOL_HALO:, :] * inv_cnt - p)
        d = jnp.concatenate(diffs, axis=0).astype(BF16)
        y_b.append(_dot(d, wpool_ref[g]))
    y_b = jnp.concatenate(y_b, axis=1) * pscale_ref[...]

    gate_a = jax.nn.sigmoid(ga_ref[...].astype(F32)).reshape(seqs * rows, D_MODEL)
    gate_b = jax.nn.sigmoid(gb_ref[...].astype(F32)).reshape(seqs * rows, D_MODEL)
    m_ref[...] = (gate_a * y_a + gate_b * y_b).astype(BF16).reshape(m_ref.shape)


def _gate(o, p, halo_src, ga, gb, w_sb_out, w_pool, pool_scale, seqs, rows, pos0, halo_from_p):
    B, T, _ = p.shape
    row_block = lambda width: pl.BlockSpec((seqs, rows, width), lambda b, t: (b, t, 0))
    if halo_from_p:
        per_tile = rows // POOL_HALO
        halo_spec = pl.BlockSpec((seqs, POOL_HALO, D_POOL),
                                 lambda b, t: (b, jnp.maximum(t * per_tile - 1, 0), 0))
    else:
        halo_spec = pl.BlockSpec((seqs, POOL_HALO, D_POOL), lambda b, t: (b, 0, 0))
    const = lambda shape: pl.BlockSpec(shape, lambda b, t: (0,) * len(shape))
    return pl.pallas_call(
        functools.partial(_gate_kernel, pos0=pos0, zero_first_halo=halo_from_p),
        grid=(B // seqs, T // rows),
        in_specs=[
            row_block(D_SB), row_block(D_POOL), halo_spec, row_block(D_MODEL), row_block(D_MODEL),
            const((D_SB, D_MODEL)), const((N_POOL_GROUPS, POOL_GROUP, POOL_OUT)), const((1, D_MODEL)),
        ],
        out_specs=row_block(D_MODEL),
        out_shape=jax.ShapeDtypeStruct((B, T, D_MODEL), BF16),
        compiler_params=_params("parallel", "arbitrary"),
        name="gate",
    )(o, p, halo_src, ga, gb, w_sb_out, w_pool, pool_scale)


def _mix_kernel(m_ref, x_ref, mod_ref, w_ref, g_ref, b_ref, o_ref):
    seqs, rows = x_ref.shape[0], x_ref.shape[1]
    for s, r in _row_chunks(seqs, rows):
        mix = _dot(m_ref[s, r, :].reshape(-1, D_MODEL), w_ref[...])
        o_ref[s, r, :] = _residual_ln(x_ref.at[s, r, :], mod_ref.at[s], 2, mix, g_ref, b_ref)


def _mix(m, x, mod, w_out, ln_g, ln_b, seqs, rows):
    B, T, _ = x.shape
    row_block = pl.BlockSpec((seqs, rows, D_MODEL), lambda b, t: (b, t, 0))
    vec = pl.BlockSpec((1, D_MODEL), lambda b, t: (0, 0))
    return pl.pallas_call(
        _mix_kernel,
        grid=(B // seqs, T // rows),
        in_specs=[
            row_block, row_block,
            pl.BlockSpec((seqs, 6, D_MODEL), lambda b, t: (b, 0, 0)),
            pl.BlockSpec((D_MODEL, D_MODEL), lambda b, t: (0, 0)),
            vec, vec,
        ],
        out_specs=row_block,
        out_shape=jax.ShapeDtypeStruct((B, T, D_MODEL), F32),
        compiler_params=_params("parallel", "arbitrary"),
        name="mix",
    )(m, x, mod, w_out, ln_g, ln_b)


def _ffn_kernel(x_ref, mod_ref, wg_ref, wu_ref, wd_ref, g_ref, b_ref, o_ref, u_ref, acc_ref):
    j = pl.program_id(2)

    @pl.when(j == 0)
    def _():
        u_ref[...] = _modulated_ln(x_ref, mod_ref, 3, 4).astype(BF16)
        acc_ref[...] = jnp.zeros_like(acc_ref)

    u = u_ref[...]
    h = jax.nn.silu(_dot(u, wg_ref[...])) * _dot(u, wu_ref[...])
    acc_ref[...] += _dot(h.astype(BF16), wd_ref[...])

    @pl.when(j == pl.num_programs(2) - 1)
    def _():
        o_ref[...] = _residual_ln(x_ref, mod_ref, 5, acc_ref[...], g_ref, b_ref)


def _ffn(x, mod, w_gate, w_up, w_down, ln_g, ln_b, seqs, rows):
    B, T, _ = x.shape
    d_ff = w_gate.shape[1]
    row_block = pl.BlockSpec((seqs, rows, D_MODEL), lambda b, t, j: (b, t, 0))
    vec = pl.BlockSpec((1, D_MODEL), lambda b, t, j: (0, 0))
    return pl.pallas_call(
        _ffn_kernel,
        grid=(B // seqs, T // rows, d_ff // FF_COLS),
        in_specs=[
            row_block,
            pl.BlockSpec((seqs, 6, D_MODEL), lambda b, t, j: (b, 0, 0)),
            pl.BlockSpec((D_MODEL, FF_COLS), lambda b, t, j: (0, j)),
            pl.BlockSpec((D_MODEL, FF_COLS), lambda b, t, j: (0, j)),
            pl.BlockSpec((FF_COLS, D_MODEL), lambda b, t, j: (j, 0)),
            vec, vec,
        ],
        out_specs=row_block,
        out_shape=jax.ShapeDtypeStruct((B, T, D_MODEL), F32),
        scratch_shapes=[pltpu.VMEM((seqs * rows, D_MODEL), BF16), pltpu.VMEM((seqs * rows, D_MODEL), F32)],
        compiler_params=_params("parallel", "parallel", "arbitrary"),
        name="ffn",
    )(x, mod, w_gate, w_up, w_down, ln_g, ln_b)


def _tile(batch, seq, row_tile=ROW_TILE):
    if seq >= row_tile:
        return 1, row_tile
    return min(batch, row_tile // seq), seq


def _front(x, mod, w_in):
    seqs, rows = _tile(x.shape[0], x.shape[1])
    return (seqs, rows), _proj(x, mod, w_in, seqs, rows)


def _back(x, mod, tile, o, p, ga, gb, halo_src, pos0, halo_from_p, w):
    seqs, rows = tile
    m = _gate(o, p, p if halo_from_p else halo_src, ga, gb, w["w_sb_out"], w["w_pool"], w["pool_scale"],
              seqs, rows, pos0, halo_from_p)
    x = _mix(m, x, mod, w["w_out"], w["ln1_g"], w["ln1_b"], seqs, rows)
    x = _ffn(x, mod, w["w_gate"], w["w_up"], w["w_down"], w["ln2_g"], w["ln2_b"], seqs, rows)
    return x, p[None, :, p.shape[1] - POOL_PAST:, :]


def kernel(x_prompt, x_sample, c_prompt, c_sample, cache_k, cache_v, state_pool, w_ada, b_ada, w_in, w_sb_out, w_pool, pool_scale, w_out, ln1_g, ln1_b, w_gate, w_up, w_down, ln2_g, ln2_b):
    assert w_in.shape[0] == DEPTH
    n_prompt = c_prompt.shape[0]
    mod = _adaln(jnp.concatenate([c_prompt, c_sample], axis=0), w_ada[0], b_ada[0])
    mod = mod.reshape(mod.shape[0], 6, D_MODEL)
    mod_p, mod_s = mod[:n_prompt], mod[n_prompt:]
    w_in_bf = w_in[0].astype(BF16)

    tile_p, (q, k_p, v_p, k_bf, v_bf, p_p, ga, gb) = _front(x_prompt, mod_p, w_in_bf)
    weights = {"w_sb_out": w_sb_out[0], "w_pool": w_pool[0], "w_out": w_out[0],
               "w_gate": w_gate[0], "w_up": w_up[0], "w_down": w_down[0]}
    cast_rows = x_prompt.shape[0] * (x_prompt.shape[1] // Q_BLOCK) * CAST_ROWS
    o, casted = _attn_prompt(q, k_bf, v_bf, [v.reshape(cast_rows, v.size // cast_rows) for v in weights.values()])
    w = {name: c.reshape(v.shape) for (name, v), c in zip(weights.items(), casted)}
    w.update(pool_scale=pool_scale, ln1_g=ln1_g, ln1_b=ln1_b, ln2_g=ln2_g, ln2_b=ln2_b)
    y_p, pool_p = _back(x_prompt, mod_p, tile_p, o, p_p, ga, gb, None, 0, True, w)

    tile_s, (q, k_s, v_s, k_bf, v_bf, p_s, ga, gb) = _front(x_sample, mod_s, w_in_bf)
    o = _attn_sample(q, k_bf, v_bf, cache_k, cache_v)
    pool_halo = jnp.pad(state_pool[0], ((0, 0), (POOL_HALO - POOL_PAST, 0), (0, 0)))
    y_s, pool_s = _back(x_sample, mod_s, tile_s, o, p_s, ga, gb, pool_halo, cache_k.shape[3], False, w)
    return y_p, y_s, k_p, v_p, pool_p, k_s, v_s, pool_s
```

```python
import functools
import math

import jax
import jax.numpy as jnp
from jax import lax
from jax.experimental import pallas as pl
from jax.experimental.pallas import tpu as pltpu

D_MODEL = 2048
N_HEADS = 8
HEAD_DIM = 128
D_SB = N_HEADS * HEAD_DIM
POOL_WINDOWS = (2, 4, 8, 16)
N_POOL_GROUPS = len(POOL_WINDOWS)
D_POOL = 1024
POOL_GROUP = D_POOL // N_POOL_GROUPS
POOL_OUT = D_MODEL // N_POOL_GROUPS
POOL_PAST = max(POOL_WINDOWS) - 1
POOL_HALO = POOL_PAST + 1
D_IN = 3 * D_SB + D_POOL + 2 * D_MODEL
DEPTH = 1
DN_ALPHA = (2 * DEPTH) ** 0.25
LN_EPS = 1e-5

LOG2_E = math.log2(math.e)
SB_SCALE_LOG2 = HEAD_DIM ** -0.5 * LOG2_E
UNDERFLOW_LOG2 = -110.0 * LOG2_E

V7X_VMEM_BYTES = 64 * 1024 * 1024
VMEM_LIMIT = V7X_VMEM_BYTES - 6 * 1024 * 1024
LANES = 128

ROW_TILE = 512
ROW_CHUNKS = 2
PROJ_COLS = 2048
ADA_COLS = 1024
FF_COLS = 512
Q_BLOCK = 128
PAST_BLOCK = 2 * Q_BLOCK
EAGER_PAST_BLOCKS = 2

BF16 = jnp.bfloat16
F32 = jnp.float32


def _params(*sem):
    return pltpu.CompilerParams(dimension_semantics=sem, vmem_limit_bytes=VMEM_LIMIT)


def _dot(a, b):
    return jnp.dot(a, b, preferred_element_type=F32)


def _normalize(x):
    mu = jnp.mean(x, axis=-1, keepdims=True)
    xc = x - mu
    var = jnp.mean(xc * xc, axis=-1, keepdims=True)
    return xc * lax.rsqrt(var + LN_EPS)


def _modulated_ln(x_ref, mod_ref, shift_idx, scale_idx):
    x = x_ref[...]
    u = _normalize(x) * (1.0 + mod_ref[:, scale_idx:scale_idx + 1, :]) + mod_ref[:, shift_idx:shift_idx + 1, :]
    return u.reshape(x.shape[0] * x.shape[1], x.shape[2])


def _row_chunks(seqs, rows):
    if seqs >= ROW_CHUNKS:
        n = seqs // ROW_CHUNKS
        return [(slice(i * n, (i + 1) * n), slice(0, rows)) for i in range(ROW_CHUNKS)]
    n = rows // ROW_CHUNKS
    return [(slice(0, seqs), slice(i * n, (i + 1) * n)) for i in range(ROW_CHUNKS)]


def _residual_ln(x_ref, mod_ref, gate_idx, branch, g_ref, b_ref):
    x = x_ref[...]
    y = DN_ALPHA * x + mod_ref[:, gate_idx:gate_idx + 1, :] * branch.reshape(x.shape)
    return _normalize(y) * g_ref[...] + b_ref[...]


def _adaln_kernel(c_ref, w_ref, b_ref, o_ref):
    a = jax.nn.silu(c_ref[...]).astype(BF16)
    o_ref[...] = _dot(a, w_ref[...].astype(BF16)) + b_ref[...]


def _adaln(c, w_ada, b_ada):
    n = c.shape[0]
    cols = w_ada.shape[1]
    return pl.pallas_call(
        _adaln_kernel,
        grid=(cols // ADA_COLS,),
        in_specs=[
            pl.BlockSpec((n, D_MODEL), lambda j: (0, 0)),
            pl.BlockSpec((D_MODEL, ADA_COLS), lambda j: (0, j)),
            pl.BlockSpec((1, ADA_COLS), lambda j: (0, j)),
        ],
        out_specs=pl.BlockSpec((n, ADA_COLS), lambda j: (0, j)),
        out_shape=jax.ShapeDtypeStruct((n, cols), F32),
        compiler_params=_params("arbitrary"),
        name="adaln",
    )(c, w_ada, b_ada.reshape(1, cols))


_QK_COL, _VP_COL, _GA_COL, _GB_COL = range(4)


def _proj_kernel(x_ref, mod_ref, w_ref, q_ref, k_ref, v_ref, kb_ref, vb_ref, p_ref, ga_ref, gb_ref, u_ref):
    j = pl.program_id(2)
    seqs, rows = x_ref.shape[0], x_ref.shape[1]

    @pl.when(j == 0)
    def _():
        u_ref[...] = _modulated_ln(x_ref, mod_ref, 0, 1).astype(BF16)

    def columns():
        return _dot(u_ref[...], w_ref[...])

    def store_heads(ref, bf_ref, res):
        for s in range(seqs):
            for h in range(N_HEADS):
                ref[s, h] = res[s * rows:(s + 1) * rows, h * HEAD_DIM:(h + 1) * HEAD_DIM]
        bf_ref[...] = res.astype(BF16).reshape(bf_ref.shape)

    @pl.when(j == _QK_COL)
    def _():
        res = columns()
        q_ref[...] = res[:, :D_SB].astype(BF16).reshape(q_ref.shape)
        store_heads(k_ref, kb_ref, res[:, D_SB:])

    @pl.when(j == _VP_COL)
    def _():
        res = columns()
        store_heads(v_ref, vb_ref, res[:, :D_SB])
        p_ref[...] = res[:, D_SB:].reshape(p_ref.shape)

    @pl.when(j == _GA_COL)
    def _():
        ga_ref[...] = columns().astype(BF16).reshape(ga_ref.shape)

    @pl.when(j == _GB_COL)
    def _():
        gb_ref[...] = columns().astype(BF16).reshape(gb_ref.shape)


def _proj(x, mod, w_in, seqs, rows):
    B, T, _ = x.shape
    assert 2 * D_SB == D_SB + D_POOL == D_MODEL == PROJ_COLS
    rows_spec = lambda width: pl.BlockSpec((seqs, rows, width), lambda b, t, j: (b, t, 0))
    heads_spec = pl.BlockSpec((None, seqs, N_HEADS, rows, HEAD_DIM), lambda b, t, j: (0, b, 0, t, 0))
    head_shape = jax.ShapeDtypeStruct((DEPTH, B, N_HEADS, T, HEAD_DIM), F32)
    return pl.pallas_call(
        _proj_kernel,
        grid=(B // seqs, T // rows, D_IN // PROJ_COLS),
        in_specs=[
            rows_spec(D_MODEL),
            pl.BlockSpec((seqs, 6, D_MODEL), lambda b, t, j: (b, 0, 0)),
            pl.BlockSpec((D_MODEL, PROJ_COLS), lambda b, t, j: (0, j)),
        ],
        out_specs=[
            rows_spec(D_SB), heads_spec, heads_spec, rows_spec(D_SB), rows_spec(D_SB),
            rows_spec(D_POOL), rows_spec(D_MODEL), rows_spec(D_MODEL),
        ],
        out_shape=[
            jax.ShapeDtypeStruct((B, T, D_SB), BF16),
            head_shape,
            head_shape,
            jax.ShapeDtypeStruct((B, T, D_SB), BF16),
            jax.ShapeDtypeStruct((B, T, D_SB), BF16),
            jax.ShapeDtypeStruct((B, T, D_POOL), F32),
            jax.ShapeDtypeStruct((B, T, D_MODEL), BF16),
            jax.ShapeDtypeStruct((B, T, D_MODEL), BF16),
        ],
        scratch_shapes=[pltpu.VMEM((seqs * rows, D_MODEL), BF16)],
        compiler_params=_params("parallel", "parallel", "arbitrary"),
        name="proj",
    )(x, mod, w_in)


def _suffix_sum_matrix(n):
    r = lax.broadcasted_iota(jnp.int32, (n, n), 0)
    c = lax.broadcasted_iota(jnp.int32, (n, n), 1)
    return jnp.where(r >= c, -1.0, 0.0).astype(BF16)


def _sb_blocks(qs, k_blks, v_blks, sums, carry_ref, acc_ref, diagonal):
    heads = len(qs)
    tq, n = qs[0].shape[0], k_blks[0].shape[0]
    if diagonal:
        row = lax.broadcasted_iota(jnp.int32, (tq, n), 0)
        col = lax.broadcasted_iota(jnp.int32, (tq, n), 1)
        mask = col < row
    zs = [lax.dot_general(q, k, (((1,), (1,)), ((), ())), preferred_element_type=F32) * SB_SCALE_LOG2
          for q, k in zip(qs, k_blks)]
    terms = []
    for z in zs:
        neg_l = jnp.maximum(z, 0.0) + jnp.log2(1.0 + jnp.exp2(-jnp.abs(z)))
        if diagonal:
            neg_l = jnp.where(mask, neg_l, 0.0)
        hi = neg_l.astype(BF16)
        terms += [hi, (neg_l - hi.astype(F32)).astype(BF16)]
    sums_out = _dot(jnp.concatenate(terms, axis=0), sums)
    ws = []
    for h in range(heads):
        incl = sums_out[2 * h * tq:(2 * h + 1) * tq] + sums_out[(2 * h + 1) * tq:(2 * h + 2) * tq]
        carry = carry_ref[h]
        w = []
        for c in range(0, n, LANES):
            width = min(LANES, n - c)
            w.append(jnp.exp2(zs[h][:, c:c + width] + incl[:, c:c + width] + carry[:, :width]))
        w = w[0] if len(w) == 1 else jnp.concatenate(w, axis=1)
        if diagonal:
            w = jnp.where(mask, w, 0.0)
        ws.append(w.astype(BF16))
        carry_ref[h] = carry + jnp.broadcast_to(incl[:, :1], carry.shape)
    pvs = [_dot(w, v) for w, v in zip(ws, v_blks)]
    for h in range(heads):
        acc_ref[h] += pvs[h]


def _walk_past(n_blocks, bound, carry_ref, wide_fn, narrow_fn):
    def cond(state):
        j, bound = state
        return (j >= 2) & (bound > UNDERFLOW_LOG2)

    def body(state):
        j, _ = state
        wide_fn(pl.multiple_of((j - 2) * Q_BLOCK, Q_BLOCK))
        return j - 2, jnp.max(carry_ref[...])

    j, bound = lax.while_loop(cond, body, (n_blocks, bound))
    if narrow_fn is not None:
        pl.when((j == 1) & (bound > UNDERFLOW_LOG2))(narrow_fn)


def _head_cols(h):
    return slice(h * HEAD_DIM, (h + 1) * HEAD_DIM)


def _attn_prompt_kernel(q_ref, k_ref, v_ref, o_ref, carry_ref, acc_ref):
    qi = pl.program_id(1)
    carry_ref[...] = jnp.zeros_like(carry_ref)
    acc_ref[...] = jnp.zeros_like(acc_ref)

    def visit(start, n, sums, diagonal):
        rows = pl.ds(pl.multiple_of(start, Q_BLOCK), n)
        hs = [_head_cols(h) for h in range(N_HEADS)]
        _sb_blocks([q_ref[:, c] for c in hs], [k_ref[rows, c] for c in hs], [v_ref[rows, c] for c in hs], sums,
                   carry_ref, acc_ref, diagonal)

    def walk(n_blocks, bound):
        wide_sums = _suffix_sum_matrix(PAST_BLOCK)
        _walk_past(n_blocks, bound, carry_ref,
                   lambda start: visit(start, PAST_BLOCK, wide_sums, False),
                   lambda: visit(0, Q_BLOCK, _suffix_sum_matrix(Q_BLOCK), False))

    eager = EAGER_PAST_BLOCKS * (PAST_BLOCK // Q_BLOCK)

    @pl.when(qi >= eager)
    def _():
        visit(qi * Q_BLOCK, Q_BLOCK, _suffix_sum_matrix(Q_BLOCK), True)
        wide_sums = _suffix_sum_matrix(PAST_BLOCK)
        for b in range(1, EAGER_PAST_BLOCKS + 1):
            visit(qi * Q_BLOCK - b * PAST_BLOCK, PAST_BLOCK, wide_sums, False)
        walk(qi - eager, jnp.max(carry_ref[...]))

    @pl.when(qi < eager)
    def _():
        visit(qi * Q_BLOCK, Q_BLOCK, _suffix_sum_matrix(Q_BLOCK), True)
        walk(qi, jnp.float32(0.0))

    for h in range(N_HEADS):
        o_ref[:, _head_cols(h)] = acc_ref[h].astype(o_ref.dtype)


def _attn_prompt(q, k, v):
    B, T, _ = q.shape
    q_spec = pl.BlockSpec((None, Q_BLOCK, D_SB), lambda b, i: (b, i, 0))
    kv_spec = pl.BlockSpec((None, T, D_SB), lambda b, i: (b, 0, 0))
    return pl.pallas_call(
        _attn_prompt_kernel,
        grid=(B, T // Q_BLOCK),
        in_specs=[q_spec, kv_spec, kv_spec],
        out_specs=q_spec,
        out_shape=jax.ShapeDtypeStruct((B, T, D_SB), BF16),
        scratch_shapes=[pltpu.VMEM((N_HEADS, Q_BLOCK, LANES), F32), pltpu.VMEM((N_HEADS, Q_BLOCK, HEAD_DIM), F32)],
        compiler_params=_params("parallel", "arbitrary"),
        name="attn_prompt",
    )(q, k, v)


def _attn_sample_kernel(q_ref, k_ref, v_ref, ck_ref, cv_ref, o_ref, carry_ref, acc_ref):
    tq = q_ref.shape[0]
    carry_ref[...] = jnp.zeros_like(carry_ref)
    acc_ref[...] = jnp.zeros_like(acc_ref)
    hs = [_head_cols(h) for h in range(N_HEADS)]
    qs = [q_ref[:, c] for c in hs]
    _sb_blocks(qs, [k_ref[:, c] for c in hs], [v_ref[:, c] for c in hs], _suffix_sum_matrix(tq),
               carry_ref, acc_ref, True)
    wide_sums = _suffix_sum_matrix(PAST_BLOCK)

    def visit(start):
        rows = pl.ds(start, PAST_BLOCK)
        _sb_blocks(qs, [ck_ref[h, rows, :].astype(BF16) for h in range(N_HEADS)],
                   [cv_ref[h, rows, :].astype(BF16) for h in range(N_HEADS)], wide_sums, carry_ref, acc_ref, False)

    past = ck_ref.shape[1]
    assert past % PAST_BLOCK == 0
    _walk_past(past // Q_BLOCK, jnp.float32(0.0), carry_ref, visit, None)
    for h in range(N_HEADS):
        o_ref[:, _head_cols(h)] = acc_ref[h].astype(o_ref.dtype)


def _attn_sample(q, k, v, cache_k, cache_v):
    B, T, _ = q.shape
    past = cache_k.shape[3]
    new_spec = pl.BlockSpec((None, T, D_SB), lambda b: (b, 0, 0))
    cache_spec = pl.BlockSpec((None, None, N_HEADS, past, HEAD_DIM), lambda b: (0, b, 0, 0, 0))
    return pl.pallas_call(
        _attn_sample_kernel,
        grid=(B,),
        in_specs=[new_spec, new_spec, new_spec, cache_spec, cache_spec],
        out_specs=new_spec,
        out_shape=jax.ShapeDtypeStruct((B, T, D_SB), BF16),
        scratch_shapes=[pltpu.VMEM((N_HEADS, T, LANES), F32), pltpu.VMEM((N_HEADS, T, HEAD_DIM), F32)],
        compiler_params=_params("parallel"),
        name="attn_sample",
    )(q, k, v, cache_k, cache_v)


def _gate_kernel(o_ref, p_ref, halo_ref, ga_ref, gb_ref, wsb_ref, wpool_ref, pscale_ref, m_ref,
                 *, pos0, zero_first_halo):
    seqs, rows = p_ref.shape[0], p_ref.shape[1]
    t0 = pl.program_id(1) * rows
    y_a = _dot(o_ref[...].reshape(seqs * rows, D_SB), wsb_ref[...])

    halo = halo_ref[...]
    if zero_first_halo:
        halo = jnp.where(t0 == 0, 0.0, halo)
    pos = pos0 + t0 + lax.broadcasted_iota(jnp.int32, (rows, LANES), 0)

    y_b = []
    for g, win in enumerate(POOL_WINDOWS):
        cols = slice(g * POOL_GROUP, (g + 1) * POOL_GROUP)
        inv_cnt = 1.0 / jnp.minimum(win, pos + 1).astype(F32)
        inv_cnt = jnp.concatenate([inv_cnt] * (POOL_GROUP // LANES), axis=1)
        diffs = []
        for s in range(seqs):
            p = p_ref[s, :, cols]
            acc = jnp.concatenate([halo[s, :, cols], p], axis=0)
            shift = 1
            while shift < win:
                acc = acc + pltpu.roll(acc, shift, 0)
                shift *= 2
            diffs.append(acc[POOL_HALO:, :] * inv_cnt - p)
        d = jnp.concatenate(diffs, axis=0).astype(BF16)
        y_b.append(_dot(d, wpool_ref[g]))
    y_b = jnp.concatenate(y_b, axis=1) * pscale_ref[...]

    gate_a = jax.nn.sigmoid(ga_ref[...].astype(F32)).reshape(seqs * rows, D_MODEL)
    gate_b = jax.nn.sigmoid(gb_ref[...].astype(F32)).reshape(seqs * rows, D_MODEL)
    m_ref[...] = (gate_a * y_a + gate_b * y_b).astype(BF16).reshape(m_ref.shape)


def _gate(o, p, halo_src, ga, gb, w_sb_out, w_pool, pool_scale, seqs, rows, pos0, halo_from_p):
    B, T, _ = p.shape
    row_block = lambda width: pl.BlockSpec((seqs, rows, width), lambda b, t: (b, t, 0))
    if halo_from_p:
        per_tile = rows // POOL_HALO
        halo_spec = pl.BlockSpec((seqs, POOL_HALO, D_POOL),
                                 lambda b, t: (b, jnp.maximum(t * per_tile - 1, 0), 0))
    else:
        halo_spec = pl.BlockSpec((seqs, POOL_HALO, D_POOL), lambda b, t: (b, 0, 0))
    const = lambda shape: pl.BlockSpec(shape, lambda b, t: (0,) * len(shape))
    return pl.pallas_call(
        functools.partial(_gate_kernel, pos0=pos0, zero_first_halo=halo_from_p),
        grid=(B // seqs, T // rows),
        in_specs=[
            row_block(D_SB), row_block(D_POOL), halo_spec, row_block(D_MODEL), row_block(D_MODEL),
            const((D_SB, D_MODEL)), const((N_POOL_GROUPS, POOL_GROUP, POOL_OUT)), const((1, D_MODEL)),
        ],
        out_specs=row_block(D_MODEL),
        out_shape=jax.ShapeDtypeStruct((B, T, D_MODEL), BF16),
        compiler_params=_params("parallel", "arbitrary"),
        name="gate",
    )(o, p, halo_src, ga, gb, w_sb_out, w_pool, pool_scale)


def _mix_kernel(m_ref, x_ref, mod_ref, w_ref, g_ref, b_ref, o_ref):
    seqs, rows = x_ref.shape[0], x_ref.shape[1]
    for s, r in _row_chunks(seqs, rows):
        mix = _dot(m_ref[s, r, :].reshape(-1, D_MODEL), w_ref[...])
        o_ref[s, r, :] = _residual_ln(x_ref.at[s, r, :], mod_ref.at[s], 2, mix, g_ref, b_ref)


def _mix(m, x, mod, w_out, ln_g, ln_b, seqs, rows):
    B, T, _ = x.shape
    row_block = pl.BlockSpec((seqs, rows, D_MODEL), lambda b, t: (b, t, 0))
    vec = pl.BlockSpec((1, D_MODEL), lambda b, t: (0, 0))
    return pl.pallas_call(
        _mix_kernel,
        grid=(B // seqs, T // rows),
        in_specs=[
            row_block, row_block,
            pl.BlockSpec((seqs, 6, D_MODEL), lambda b, t: (b, 0, 0)),
            pl.BlockSpec((D_MODEL, D_MODEL), lambda b, t: (0, 0)),
            vec, vec,
        ],
        out_specs=row_block,
        out_shape=jax.ShapeDtypeStruct((B, T, D_MODEL), F32),
        compiler_params=_params("parallel", "arbitrary"),
        name="mix",
    )(m, x, mod, w_out, ln_g, ln_b)


def _ffn_kernel(x_ref, mod_ref, wg_ref, wu_ref, wd_ref, g_ref, b_ref, o_ref, u_ref, acc_ref):
    j = pl.program_id(2)

    @pl.when(j == 0)
    def _():
        u_ref[...] = _modulated_ln(x_ref, mod_ref, 3, 4).astype(BF16)
        acc_ref[...] = jnp.zeros_like(acc_ref)

    u = u_ref[...]
    h = jax.nn.silu(_dot(u, wg_ref[...])) * _dot(u, wu_ref[...])
    acc_ref[...] += _dot(h.astype(BF16), wd_ref[...])

    @pl.when(j == pl.num_programs(2) - 1)
    def _():
        o_ref[...] = _residual_ln(x_ref, mod_ref, 5, acc_ref[...], g_ref, b_ref)


def _ffn(x, mod, w_gate, w_up, w_down, ln_g, ln_b, seqs, rows):
    B, T, _ = x.shape
    d_ff = w_gate.shape[1]
    row_block = pl.BlockSpec((seqs, rows, D_MODEL), lambda b, t, j: (b, t, 0))
    vec = pl.BlockSpec((1, D_MODEL), lambda b, t, j: (0, 0))
    return pl.pallas_call(
        _ffn_kernel,
        grid=(B // seqs, T // rows, d_ff // FF_COLS),
        in_specs=[
            row_block,
            pl.BlockSpec((seqs, 6, D_MODEL), lambda b, t, j: (b, 0, 0)),
            pl.BlockSpec((D_MODEL, FF_COLS), lambda b, t, j: (0, j)),
            pl.BlockSpec((D_MODEL, FF_COLS), lambda b, t, j: (0, j)),
            pl.BlockSpec((FF_COLS, D_MODEL), lambda b, t, j: (j, 0)),
            vec, vec,
        ],
        out_specs=row_block,
        out_shape=jax.ShapeDtypeStruct((B, T, D_MODEL), F32),
        scratch_shapes=[pltpu.VMEM((seqs * rows, D_MODEL), BF16), pltpu.VMEM((seqs * rows, D_MODEL), F32)],
        compiler_params=_params("parallel", "parallel", "arbitrary"),
        name="ffn",
    )(x, mod, w_gate, w_up, w_down, ln_g, ln_b)


def _tile(batch, seq, row_tile=ROW_TILE):
    if seq >= row_tile:
        return 1, row_tile
    return min(batch, row_tile // seq), seq


def _layer(x, mod, attn_fn, halo_src, pos0, halo_from_p, w):
    B, T, _ = x.shape
    seqs, rows = _tile(B, T)
    q, k, v, k_bf, v_bf, p, ga, gb = _proj(x, mod, w["w_in"], seqs, rows)
    o = attn_fn(q, k_bf, v_bf)
    m = _gate(o, p, p if halo_from_p else halo_src, ga, gb, w["w_sb_out"], w["w_pool"], w["pool_scale"],
              seqs, rows, pos0, halo_from_p)
    x = _mix(m, x, mod, w["w_out"], w["ln1_g"], w["ln1_b"], seqs, rows)
    x = _ffn(x, mod, w["w_gate"], w["w_up"], w["w_down"], w["ln2_g"], w["ln2_b"], seqs, rows)
    return x, k, v, p[None, :, T - POOL_PAST:, :]


def kernel(x_prompt, x_sample, c_prompt, c_sample, cache_k, cache_v, state_pool, w_ada, b_ada, w_in, w_sb_out, w_pool, pool_scale, w_out, ln1_g, ln1_b, w_gate, w_up, w_down, ln2_g, ln2_b):
    assert w_in.shape[0] == DEPTH
    n_prompt = c_prompt.shape[0]
    mod = _adaln(jnp.concatenate([c_prompt, c_sample], axis=0), w_ada[0], b_ada[0])
    mod = mod.reshape(mod.shape[0], 6, D_MODEL)
    w = {
        "w_in": w_in[0].astype(BF16), "w_sb_out": w_sb_out[0].astype(BF16), "w_pool": w_pool[0].astype(BF16),
        "w_out": w_out[0].astype(BF16), "w_gate": w_gate[0].astype(BF16), "w_up": w_up[0].astype(BF16),
        "w_down": w_down[0].astype(BF16), "pool_scale": pool_scale, "ln1_g": ln1_g, "ln1_b": ln1_b,
        "ln2_g": ln2_g, "ln2_b": ln2_b,
    }
    y_p, k_p, v_p, pool_p = _layer(x_prompt, mod[:n_prompt], _attn_prompt, None, 0, True, w)

    pool_halo = jnp.pad(state_pool[0], ((0, 0), (POOL_HALO - POOL_PAST, 0), (0, 0)))
    attn_sample = lambda q, k, v: _attn_sample(q, k, v, cache_k, cache_v)
    y_s, k_s, v_s, pool_s = _layer(x_sample, mod[n_prompt:], attn_sample, pool_halo, cache_k.shape[3], False, w)
    return y_p, y_s, k_p, v_p, pool_p, k_s, v_s, pool_s
```

```python
import functools
import math

import jax
import jax.numpy as jnp
from jax import lax
from jax.experimental import pallas as pl
from jax.experimental.pallas import tpu as pltpu

D_MODEL = 2048
N_HEADS = 8
HEAD_DIM = 128
D_SB = N_HEADS * HEAD_DIM
POOL_WINDOWS = (2, 4, 8, 16)
N_POOL_GROUPS = len(POOL_WINDOWS)
D_POOL = 1024
POOL_GROUP = D_POOL // N_POOL_GROUPS
POOL_OUT = D_MODEL // N_POOL_GROUPS
POOL_PAST = max(POOL_WINDOWS) - 1
POOL_HALO = POOL_PAST + 1
D_IN = 3 * D_SB + D_POOL + 2 * D_MODEL
DEPTH = 1
DN_ALPHA = (2 * DEPTH) ** 0.25
LN_EPS = 1e-5

LOG2_E = math.log2(math.e)
SB_SCALE_LOG2 = HEAD_DIM ** -0.5 * LOG2_E
UNDERFLOW_LOG2 = -110.0 * LOG2_E

V7X_VMEM_BYTES = 64 * 1024 * 1024
VMEM_LIMIT = V7X_VMEM_BYTES - 6 * 1024 * 1024
LANES = 128

ROW_TILE = 512
ROW_CHUNKS = 2
PROJ_COLS = 2048
ADA_COLS = 1024
FF_COLS = 512
Q_BLOCK = 128
PAST_BLOCK = 2 * Q_BLOCK
EAGER_PAST_BLOCKS = 2

BF16 = jnp.bfloat16
F32 = jnp.float32


def _params(*sem):
    return pltpu.CompilerParams(dimension_semantics=sem, vmem_limit_bytes=VMEM_LIMIT)


def _dot(a, b):
    return jnp.dot(a, b, preferred_element_type=F32)


def _normalize(x):
    mu = jnp.mean(x, axis=-1, keepdims=True)
    xc = x - mu
    var = jnp.mean(xc * xc, axis=-1, keepdims=True)
    return xc * lax.rsqrt(var + LN_EPS)


def _modulated_ln(x_ref, mod_ref, shift_idx, scale_idx):
    x = x_ref[...]
    u = _normalize(x) * (1.0 + mod_ref[:, scale_idx:scale_idx + 1, :]) + mod_ref[:, shift_idx:shift_idx + 1, :]
    return u.reshape(x.shape[0] * x.shape[1], x.shape[2])


def _row_chunks(seqs, rows):
    if seqs >= ROW_CHUNKS:
        n = seqs // ROW_CHUNKS
        return [(slice(i * n, (i + 1) * n), slice(0, rows)) for i in range(ROW_CHUNKS)]
    n = rows // ROW_CHUNKS
    return [(slice(0, seqs), slice(i * n, (i + 1) * n)) for i in range(ROW_CHUNKS)]


def _residual_ln(x_ref, mod_ref, gate_idx, branch, g_ref, b_ref):
    x = x_ref[...]
    y = DN_ALPHA * x + mod_ref[:, gate_idx:gate_idx + 1, :] * branch.reshape(x.shape)
    return _normalize(y) * g_ref[...] + b_ref[...]


def _adaln_kernel(c_ref, w_ref, b_ref, o_ref):
    a = jax.nn.silu(c_ref[...]).astype(BF16)
    o_ref[...] = _dot(a, w_ref[...].astype(BF16)) + b_ref[...]


def _adaln(c, w_ada, b_ada):
    n = c.shape[0]
    cols = w_ada.shape[1]
    return pl.pallas_call(
        _adaln_kernel,
        grid=(cols // ADA_COLS,),
        in_specs=[
            pl.BlockSpec((n, D_MODEL), lambda j: (0, 0)),
            pl.BlockSpec((D_MODEL, ADA_COLS), lambda j: (0, j)),
            pl.BlockSpec((1, ADA_COLS), lambda j: (0, j)),
        ],
        out_specs=pl.BlockSpec((n, ADA_COLS), lambda j: (0, j)),
        out_shape=jax.ShapeDtypeStruct((n, cols), F32),
        compiler_params=_params("arbitrary"),
        name="adaln",
    )(c, w_ada, b_ada.reshape(1, cols))


_QK_COL, _VP_COL, _GA_COL, _GB_COL = range(4)


def _proj_kernel(x_ref, mod_ref, w_ref, qkv_ref, k_ref, v_ref, p_ref, gab_ref, u_ref):
    j = pl.program_id(2)
    seqs, rows = x_ref.shape[0], x_ref.shape[1]

    @pl.when(j == 0)
    def _():
        u_ref[...] = _modulated_ln(x_ref, mod_ref, 0, 1).astype(BF16)

    def columns():
        return _dot(u_ref[...], w_ref[...])

    def store_heads(ref, res):
        for s in range(seqs):
            for h in range(N_HEADS):
                ref[s, h] = res[s * rows:(s + 1) * rows, h * HEAD_DIM:(h + 1) * HEAD_DIM]

    def store_bf16(ref, first_col, res):
        ref[:, :, first_col:first_col + res.shape[1]] = res.astype(BF16).reshape(seqs, rows, res.shape[1])

    @pl.when(j == _QK_COL)
    def _():
        res = columns()
        store_bf16(qkv_ref, 0, res)
        store_heads(k_ref, res[:, D_SB:])

    @pl.when(j == _VP_COL)
    def _():
        res = columns()
        store_bf16(qkv_ref, 2 * D_SB, res[:, :D_SB])
        store_heads(v_ref, res[:, :D_SB])
        p_ref[...] = res[:, D_SB:].reshape(p_ref.shape)

    @pl.when(j == _GA_COL)
    def _():
        store_bf16(gab_ref, 0, columns())

    @pl.when(j == _GB_COL)
    def _():
        store_bf16(gab_ref, D_MODEL, columns())


def _proj(x, mod, w_in, seqs, rows):
    B, T, _ = x.shape
    assert 2 * D_SB == D_SB + D_POOL == D_MODEL == PROJ_COLS
    rows_spec = lambda width: pl.BlockSpec((seqs, rows, width), lambda b, t, j: (b, t, 0))
    heads_spec = pl.BlockSpec((None, seqs, N_HEADS, rows, HEAD_DIM), lambda b, t, j: (0, b, 0, t, 0))
    head_shape = jax.ShapeDtypeStruct((DEPTH, B, N_HEADS, T, HEAD_DIM), F32)
    return pl.pallas_call(
        _proj_kernel,
        grid=(B // seqs, T // rows, D_IN // PROJ_COLS),
        in_specs=[
            rows_spec(D_MODEL),
            pl.BlockSpec((seqs, 6, D_MODEL), lambda b, t, j: (b, 0, 0)),
            pl.BlockSpec((D_MODEL, PROJ_COLS), lambda b, t, j: (0, j)),
        ],
        out_specs=[rows_spec(3 * D_SB), heads_spec, heads_spec, rows_spec(D_POOL), rows_spec(2 * D_MODEL)],
        out_shape=[
            jax.ShapeDtypeStruct((B, T, 3 * D_SB), BF16),
            head_shape,
            head_shape,
            jax.ShapeDtypeStruct((B, T, D_POOL), F32),
            jax.ShapeDtypeStruct((B, T, 2 * D_MODEL), BF16),
        ],
        scratch_shapes=[pltpu.VMEM((seqs * rows, D_MODEL), BF16)],
        compiler_params=_params("parallel", "parallel", "arbitrary"),
        name="proj",
    )(x, mod, w_in)


def _suffix_sum_matrix(n):
    r = lax.broadcasted_iota(jnp.int32, (n, n), 0)
    c = lax.broadcasted_iota(jnp.int32, (n, n), 1)
    return jnp.where(r >= c, -1.0, 0.0).astype(BF16)


def _sb_blocks(qs, blocks, sums, carry_ref, acc_ref):
    heads, tq = len(qs), qs[0].shape[0]
    row = lax.broadcasted_iota(jnp.int32, (tq, tq), 0)
    col = lax.broadcasted_iota(jnp.int32, (tq, tq), 1)
    mask = col < row
    zs = [[lax.dot_general(q, k, (((1,), (1,)), ((), ())), preferred_element_type=F32) * SB_SCALE_LOG2
           for q, k in zip(qs, k_blks)] for k_blks, _, _ in blocks]
    terms, first_term = {}, []
    for z_blk, (k_blks, _, diagonal) in zip(zs, blocks):
        of_width = terms.setdefault(k_blks[0].shape[0], [])
        first_term.append(len(of_width))
        for z in z_blk:
            neg_l = jnp.maximum(z, 0.0) + jnp.log2(1.0 + jnp.exp2(-jnp.abs(z)))
            if diagonal:
                neg_l = jnp.where(mask, neg_l, 0.0)
            hi = neg_l.astype(BF16)
            of_width += [hi, (neg_l - hi.astype(F32)).astype(BF16)]
    sums_out = {n: _dot(jnp.concatenate(t, axis=0), sums[n]) for n, t in terms.items()}
    ws = [[] for _ in blocks]
    for h in range(heads):
        carry = carry_ref[h]
        for b, (k_blks, _, diagonal) in enumerate(blocks):
            n = k_blks[0].shape[0]
            at = (first_term[b] + 2 * h) * tq
            incl = sums_out[n][at:at + tq] + sums_out[n][at + tq:at + 2 * tq]
            w = []
            for c in range(0, n, LANES):
                width = min(LANES, n - c)
                w.append(jnp.exp2(zs[b][h][:, c:c + width] + incl[:, c:c + width] + carry[:, :width]))
            w = w[0] if len(w) == 1 else jnp.concatenate(w, axis=1)
            if diagonal:
                w = jnp.where(mask, w, 0.0)
            ws[b].append(w.astype(BF16))
            carry = carry + jnp.broadcast_to(incl[:, :1], carry.shape)
        carry_ref[h] = carry
    pvs = [[_dot(w, v) for w, v in zip(ws[b], v_blks)] for b, (_, v_blks, _) in enumerate(blocks)]
    for h in range(heads):
        acc_ref[h] += functools.reduce(lambda a, b: a + b, [pv[h] for pv in pvs])


def _walk_past(n_blocks, bound, carry_ref, wide_fn, narrow_fn):
    def cond(state):
        j, bound = state
        return (j >= 2) & (bound > UNDERFLOW_LOG2)

    def body(state):
        j, _ = state
        wide_fn(pl.multiple_of((j - 2) * Q_BLOCK, Q_BLOCK))
        return j - 2, jnp.max(carry_ref[...])

    j, bound = lax.while_loop(cond, body, (n_blocks, bound))
    if narrow_fn is not None:
        pl.when((j == 1) & (bound > UNDERFLOW_LOG2))(narrow_fn)


def _head_cols(h):
    return slice(h * HEAD_DIM, (h + 1) * HEAD_DIM)


def _attn_prompt_kernel(q_ref, k_ref, v_ref, o_ref, carry_ref, acc_ref):
    qi = pl.program_id(1)
    carry_ref[...] = jnp.zeros_like(carry_ref)
    acc_ref[...] = jnp.zeros_like(acc_ref)

    hs = [_head_cols(h) for h in range(N_HEADS)]

    def keys(start, n, diagonal):
        rows = pl.ds(pl.multiple_of(start, Q_BLOCK), n)
        return [k_ref[rows, c] for c in hs], [v_ref[rows, c] for c in hs], diagonal

    def visit(blocks, sums):
        _sb_blocks([q_ref[:, c] for c in hs], blocks, sums, carry_ref, acc_ref)

    def walk(n_blocks, bound, sums):
        _walk_past(n_blocks, bound, carry_ref,
                   lambda start: visit([keys(start, PAST_BLOCK, False)], sums),
                   lambda: visit([keys(0, Q_BLOCK, False)], sums))

    eager = EAGER_PAST_BLOCKS * (PAST_BLOCK // Q_BLOCK)

    @pl.when(qi >= eager)
    def _():
        sums = {n: _suffix_sum_matrix(n) for n in (Q_BLOCK, PAST_BLOCK)}
        visit([keys(qi * Q_BLOCK, Q_BLOCK, True)]
              + [keys(qi * Q_BLOCK - b * PAST_BLOCK, PAST_BLOCK, False) for b in range(1, EAGER_PAST_BLOCKS + 1)], sums)
        walk(qi - eager, jnp.max(carry_ref[...]), sums)

    @pl.when(qi < eager)
    def _():
        sums = {n: _suffix_sum_matrix(n) for n in (Q_BLOCK, PAST_BLOCK)}
        visit([keys(qi * Q_BLOCK, Q_BLOCK, True)], sums)
        walk(qi, jnp.float32(0.0), sums)

    for h in range(N_HEADS):
        o_ref[:, _head_cols(h)] = acc_ref[h].astype(o_ref.dtype)


def _attn_prompt(qkv):
    B, T, _ = qkv.shape
    q_spec = pl.BlockSpec((None, Q_BLOCK, D_SB), lambda b, i: (b, i, 0))
    kv_spec = lambda part: pl.BlockSpec((None, T, D_SB), lambda b, i: (b, 0, part))
    return pl.pallas_call(
        _attn_prompt_kernel,
        grid=(B, T // Q_BLOCK),
        in_specs=[q_spec, kv_spec(1), kv_spec(2)],
        out_specs=q_spec,
        out_shape=jax.ShapeDtypeStruct((B, T, D_SB), BF16),
        scratch_shapes=[pltpu.VMEM((N_HEADS, Q_BLOCK, LANES), F32), pltpu.VMEM((N_HEADS, Q_BLOCK, HEAD_DIM), F32)],
        compiler_params=_params("parallel", "arbitrary"),
        name="attn_prompt",
    )(qkv, qkv, qkv)


def _attn_sample_kernel(q_ref, k_ref, v_ref, ck_ref, cv_ref, o_ref, carry_ref, acc_ref):
    tq = q_ref.shape[0]
    carry_ref[...] = jnp.zeros_like(carry_ref)
    acc_ref[...] = jnp.zeros_like(acc_ref)
    hs = [_head_cols(h) for h in range(N_HEADS)]
    qs = [q_ref[:, c] for c in hs]
    sums = {n: _suffix_sum_matrix(n) for n in (tq, PAST_BLOCK)}
    _sb_blocks(qs, [([k_ref[:, c] for c in hs], [v_ref[:, c] for c in hs], True)], sums, carry_ref, acc_ref)

    def visit(start):
        rows = pl.ds(start, PAST_BLOCK)
        cached = ([ck_ref[h, rows, :].astype(BF16) for h in range(N_HEADS)],
                  [cv_ref[h, rows, :].astype(BF16) for h in range(N_HEADS)], False)
        _sb_blocks(qs, [cached], sums, carry_ref, acc_ref)

    past = ck_ref.shape[1]
    assert past % PAST_BLOCK == 0
    _walk_past(past // Q_BLOCK, jnp.float32(0.0), carry_ref, visit, None)
    for h in range(N_HEADS):
        o_ref[:, _head_cols(h)] = acc_ref[h].astype(o_ref.dtype)


def _attn_sample(qkv, cache_k, cache_v):
    B, T, _ = qkv.shape
    past = cache_k.shape[3]
    new_spec = lambda part: pl.BlockSpec((None, T, D_SB), lambda b: (b, 0, part))
    cache_spec = pl.BlockSpec((None, None, N_HEADS, past, HEAD_DIM), lambda b: (0, b, 0, 0, 0))
    return pl.pallas_call(
        _attn_sample_kernel,
        grid=(B,),
        in_specs=[new_spec(0), new_spec(1), new_spec(2), cache_spec, cache_spec],
        out_specs=new_spec(0),
        out_shape=jax.ShapeDtypeStruct((B, T, D_SB), BF16),
        scratch_shapes=[pltpu.VMEM((N_HEADS, T, LANES), F32), pltpu.VMEM((N_HEADS, T, HEAD_DIM), F32)],
        compiler_params=_params("parallel"),
        name="attn_sample",
    )(qkv, qkv, qkv, cache_k, cache_v)


def _gate_kernel(o_ref, p_ref, halo_ref, ga_ref, gb_ref, wsb_ref, wpool_ref, pscale_ref, m_ref,
                 *, pos0, zero_first_halo):
    seqs, rows = p_ref.shape[0], p_ref.shape[1]
    t0 = pl.program_id(1) * rows
    y_a = _dot(o_ref[...].reshape(seqs * rows, D_SB), wsb_ref[...])

    halo = halo_ref[...]
    if zero_first_halo:
        halo = jnp.where(t0 == 0, 0.0, halo)
    pos = pos0 + t0 + lax.broadcasted_iota(jnp.int32, (rows, LANES), 0)

    y_b = []
    for g, win in enumerate(POOL_WINDOWS):
        cols = slice(g * POOL_GROUP, (g + 1) * POOL_GROUP)
        inv_cnt = 1.0 / jnp.minimum(win, pos + 1).astype(F32)
        inv_cnt = jnp.concatenate([inv_cnt] * (POOL_GROUP // LANES), axis=1)
        diffs = []
        for s in range(seqs):
            p = p_ref[s, :, cols]
            acc = jnp.concatenate([halo[s, :, cols], p], axis=0)
            shift = 1
            while shift < win:
                acc = acc + pltpu.roll(acc, shift, 0)
                shift *= 2
            diffs.append(acc[POOL_HALO:, :] * inv_cnt - p)
        d = jnp.concatenate(diffs, axis=0).astype(BF16)
        y_b.append(_dot(d, wpool_ref[g]))
    y_b = jnp.concatenate(y_b, axis=1) * pscale_ref[...]

    gate_a = jax.nn.sigmoid(ga_ref[...].astype(F32)).reshape(seqs * rows, D_MODEL)
    gate_b = jax.nn.sigmoid(gb_ref[...].astype(F32)).reshape(seqs * rows, D_MODEL)
    m_ref[...] = (gate_a * y_a + gate_b * y_b).astype(BF16).reshape(m_ref.shape)


def _gate(o, p, halo_src, gab, w_sb_out, w_pool, pool_scale, seqs, rows, pos0, halo_from_p):
    B, T, _ = p.shape
    row_block = lambda width, part=0: pl.BlockSpec((seqs, rows, width), lambda b, t: (b, t, part))
    if halo_from_p:
        per_tile = rows // POOL_HALO
        halo_spec = pl.BlockSpec((seqs, POOL_HALO, D_POOL),
                                 lambda b, t: (b, jnp.maximum(t * per_tile - 1, 0), 0))
    else:
        halo_spec = pl.BlockSpec((seqs, POOL_HALO, D_POOL), lambda b, t: (b, 0, 0))
    const = lambda shape: pl.BlockSpec(shape, lambda b, t: (0,) * len(shape))
    return pl.pallas_call(
        functools.partial(_gate_kernel, pos0=pos0, zero_first_halo=halo_from_p),
        grid=(B // seqs, T // rows),
        in_specs=[
            row_block(D_SB), row_block(D_POOL), halo_spec, row_block(D_MODEL, 0), row_block(D_MODEL, 1),
            const((D_SB, D_MODEL)), const((N_POOL_GROUPS, POOL_GROUP, POOL_OUT)), const((1, D_MODEL)),
        ],
        out_specs=row_block(D_MODEL),
        out_shape=jax.ShapeDtypeStruct((B, T, D_MODEL), BF16),
        compiler_params=_params("parallel", "arbitrary"),
        name="gate",
    )(o, p, halo_src, gab, gab, w_sb_out, w_pool, pool_scale)


def _mix_kernel(m_ref, x_ref, mod_ref, w_ref, g_ref, b_ref, o_ref):
    seqs, rows = x_ref.shape[0], x_ref.shape[1]
    for s, r in _row_chunks(seqs, rows):
        mix = _dot(m_ref[s, r, :].reshape(-1, D_MODEL), w_ref[...])
        o_ref[s, r, :] = _residual_ln(x_ref.at[s, r, :], mod_ref.at[s], 2, mix, g_ref, b_ref)


def _mix(m, x, mod, w_out, ln_g, ln_b, seqs, rows):
    B, T, _ = x.shape
    row_block = pl.BlockSpec((seqs, rows, D_MODEL), lambda b, t: (b, t, 0))
    vec = pl.BlockSpec((1, D_MODEL), lambda b, t: (0, 0))
    return pl.pallas_call(
        _mix_kernel,
        grid=(B // seqs, T // rows),
        in_specs=[
            row_block, row_block,
            pl.BlockSpec((seqs, 6, D_MODEL), lambda b, t: (b, 0, 0)),
            pl.BlockSpec((D_MODEL, D_MODEL), lambda b, t: (0, 0)),
            vec, vec,
        ],
        out_specs=row_block,
        out_shape=jax.ShapeDtypeStruct((B, T, D_MODEL), F32),
        compiler_params=_params("parallel", "arbitrary"),
        name="mix",
    )(m, x, mod, w_out, ln_g, ln_b)


def _ffn_kernel(x_ref, mod_ref, wg_ref, wu_ref, wd_ref, g_ref, b_ref, o_ref, u_ref, acc_ref):
    j = pl.program_id(2)

    @pl.when(j == 0)
    def _():
        u_ref[...] = _modulated_ln(x_ref, mod_ref, 3, 4).astype(BF16)
        acc_ref[...] = jnp.zeros_like(acc_ref)

    u = u_ref[...]
    h = jax.nn.silu(_dot(u, wg_ref[...])) * _dot(u, wu_ref[...])
    acc_ref[...] += _dot(h.astype(BF16), wd_ref[...])

    @pl.when(j == pl.num_programs(2) - 1)
    def _():
        o_ref[...] = _residual_ln(x_ref, mod_ref, 5, acc_ref[...], g_ref, b_ref)


def _ffn(x, mod, w_gate, w_up, w_down, ln_g, ln_b, seqs, rows):
    B, T, _ = x.shape
    d_ff = w_gate.shape[1]
    row_block = pl.BlockSpec((seqs, rows, D_MODEL), lambda b, t, j: (b, t, 0))
    vec = pl.BlockSpec((1, D_MODEL), lambda b, t, j: (0, 0))
    return pl.pallas_call(
        _ffn_kernel,
        grid=(B // seqs, T // rows, d_ff // FF_COLS),
        in_specs=[
            row_block,
            pl.BlockSpec((seqs, 6, D_MODEL), lambda b, t, j: (b, 0, 0)),
            pl.BlockSpec((D_MODEL, FF_COLS), lambda b, t, j: (0, j)),
            pl.BlockSpec((D_MODEL, FF_COLS), lambda b, t, j: (0, j)),
            pl.BlockSpec((FF_COLS, D_MODEL), lambda b, t, j: (j, 0)),
            vec, vec,
        ],
        out_specs=row_block,
        out_shape=jax.ShapeDtypeStruct((B, T, D_MODEL), F32),
        scratch_shapes=[pltpu.VMEM((seqs * rows, D_MODEL), BF16), pltpu.VMEM((seqs * rows, D_MODEL), F32)],
        compiler_params=_params("parallel", "parallel", "arbitrary"),
        name="ffn",
    )(x, mod, w_gate, w_up, w_down, ln_g, ln_b)


def _tile(batch, seq, row_tile=ROW_TILE):
    if seq >= row_tile:
        return 1, row_tile
    return min(batch, row_tile // seq), seq


def _layer(x, mod, attn_fn, halo_src, pos0, halo_from_p, w):
    B, T, _ = x.shape
    seqs, rows = _tile(B, T)
    qkv, k, v, p, gab = _proj(x, mod, w["w_in"], seqs, rows)
    o = attn_fn(qkv)
    m = _gate(o, p, p if halo_from_p else halo_src, gab, w["w_sb_out"], w["w_pool"], w["pool_scale"],
              seqs, rows, pos0, halo_from_p)
    x = _mix(m, x, mod, w["w_out"], w["ln1_g"], w["ln1_b"], seqs, rows)
    x = _ffn(x, mod, w["w_gate"], w["w_up"], w["w_down"], w["ln2_g"], w["ln2_b"], seqs, rows)
    return x, k, v, p[None, :, T - POOL_PAST:, :]


def kernel(x_prompt, x_sample, c_prompt, c_sample, cache_k, cache_v, state_pool, w_ada, b_ada, w_in, w_sb_out, w_pool, pool_scale, w_out, ln1_g, ln1_b, w_gate, w_up, w_down, ln2_g, ln2_b):
    assert w_in.shape[0] == DEPTH
    n_prompt = c_prompt.shape[0]
    mod = _adaln(jnp.concatenate([c_prompt, c_sample], axis=0), w_ada[0], b_ada[0])
    mod = mod.reshape(mod.shape[0], 6, D_MODEL)
    w = {
        "w_in": w_in[0].astype(BF16), "w_sb_out": w_sb_out[0].astype(BF16), "w_pool": w_pool[0].astype(BF16),
        "w_out": w_out[0].astype(BF16), "w_gate": w_gate[0].astype(BF16), "w_up": w_up[0].astype(BF16),
        "w_down": w_down[0].astype(BF16), "pool_scale": pool_scale, "ln1_g": ln1_g, "ln1_b": ln1_b,
        "ln2_g": ln2_g, "ln2_b": ln2_b,
    }
    y_p, k_p, v_p, pool_p = _layer(x_prompt, mod[:n_prompt], _attn_prompt, None, 0, True, w)

    pool_halo = jnp.pad(state_pool[0], ((0, 0), (POOL_HALO - POOL_PAST, 0), (0, 0)))
    attn_sample = lambda qkv: _attn_sample(qkv, cache_k, cache_v)
    y_s, k_s, v_s, pool_s = _layer(x_sample, mod[n_prompt:], attn_sample, pool_halo, cache_k.shape[3], False, w)
    return y_p, y_s, k_p, v_p, pool_p, k_s, v_s, pool_s
```

```python
import functools
import math

import jax
import jax.numpy as jnp
from jax import lax
from jax.experimental import pallas as pl
from jax.experimental.pallas import tpu as pltpu

D_MODEL = 2048
N_HEADS = 8
HEAD_DIM = 128
D_SB = N_HEADS * HEAD_DIM
POOL_WINDOWS = (2, 4, 8, 16)
N_POOL_GROUPS = len(POOL_WINDOWS)
D_POOL = 1024
POOL_GROUP = D_POOL // N_POOL_GROUPS
POOL_OUT = D_MODEL // N_POOL_GROUPS
POOL_PAST = max(POOL_WINDOWS) - 1
POOL_HALO = POOL_PAST + 1
D_IN = 3 * D_SB + D_POOL + 2 * D_MODEL
DEPTH = 1
DN_ALPHA = (2 * DEPTH) ** 0.25
LN_EPS = 1e-5

LOG2_E = math.log2(math.e)
SB_SCALE_LOG2 = HEAD_DIM ** -0.5 * LOG2_E
UNDERFLOW_LOG2 = -110.0 * LOG2_E

V7X_VMEM_BYTES = 64 * 1024 * 1024
VMEM_LIMIT = V7X_VMEM_BYTES - 6 * 1024 * 1024
LANES = 128

ROW_TILE = 512
ROW_CHUNKS = 2
PROJ_COLS = 2048
ADA_COLS = 1024
FF_COLS = 512
Q_BLOCK = 128
PAST_BLOCK = 2 * Q_BLOCK
EAGER_PAST_BLOCKS = 2

BF16 = jnp.bfloat16
F32 = jnp.float32


def _params(*sem):
    return pltpu.CompilerParams(dimension_semantics=sem, vmem_limit_bytes=VMEM_LIMIT)


def _dot(a, b):
    return jnp.dot(a, b, preferred_element_type=F32)


def _normalize(x):
    mu = jnp.mean(x, axis=-1, keepdims=True)
    xc = x - mu
    var = jnp.mean(xc * xc, axis=-1, keepdims=True)
    return xc * lax.rsqrt(var + LN_EPS)


def _modulated_ln(x_ref, mod_ref, shift_idx, scale_idx):
    x = x_ref[...]
    u = _normalize(x) * (1.0 + mod_ref[:, scale_idx:scale_idx + 1, :]) + mod_ref[:, shift_idx:shift_idx + 1, :]
    return u.reshape(x.shape[0] * x.shape[1], x.shape[2])


def _row_chunks(seqs, rows):
    if seqs >= ROW_CHUNKS:
        n = seqs // ROW_CHUNKS
        return [(slice(i * n, (i + 1) * n), slice(0, rows)) for i in range(ROW_CHUNKS)]
    n = rows // ROW_CHUNKS
    return [(slice(0, seqs), slice(i * n, (i + 1) * n)) for i in range(ROW_CHUNKS)]


def _residual_ln(x_ref, mod_ref, gate_idx, branch, g_ref, b_ref):
    x = x_ref[...]
    y = DN_ALPHA * x + mod_ref[:, gate_idx:gate_idx + 1, :] * branch.reshape(x.shape)
    return _normalize(y) * g_ref[...] + b_ref[...]


def _adaln_kernel(c_ref, w_ref, b_ref, o_ref):
    a = jax.nn.silu(c_ref[...]).astype(BF16)
    o_ref[...] = _dot(a, w_ref[...].astype(BF16)) + b_ref[...]


def _adaln(c, w_ada, b_ada):
    n = c.shape[0]
    cols = w_ada.shape[1]
    return pl.pallas_call(
        _adaln_kernel,
        grid=(cols // ADA_COLS,),
        in_specs=[
            pl.BlockSpec((n, D_MODEL), lambda j: (0, 0)),
            pl.BlockSpec((D_MODEL, ADA_COLS), lambda j: (0, j)),
            pl.BlockSpec((1, ADA_COLS), lambda j: (0, j)),
        ],
        out_specs=pl.BlockSpec((n, ADA_COLS), lambda j: (0, j)),
        out_shape=jax.ShapeDtypeStruct((n, cols), F32),
        compiler_params=_params("arbitrary"),
        name="adaln",
    )(c, w_ada, b_ada.reshape(1, cols))


_QK_COL, _VP_COL, _GA_COL, _GB_COL = range(4)


def _proj_kernel(x_ref, mod_ref, w_ref, qkv_ref, k_ref, v_ref, p_ref, gab_ref, u_ref):
    j = pl.program_id(2)
    seqs, rows = x_ref.shape[0], x_ref.shape[1]

    @pl.when(j == 0)
    def _():
        u_ref[...] = _modulated_ln(x_ref, mod_ref, 0, 1).astype(BF16)

    def columns():
        return _dot(u_ref[...], w_ref[...])

    def store_heads(ref, res):
        for s in range(seqs):
            for h in range(N_HEADS):
                ref[s, h] = res[s * rows:(s + 1) * rows, h * HEAD_DIM:(h + 1) * HEAD_DIM]

    def store_bf16(ref, first_col, res):
        ref[:, :, first_col:first_col + res.shape[1]] = res.astype(BF16).reshape(seqs, rows, res.shape[1])

    @pl.when(j == _QK_COL)
    def _():
        res = columns()
        store_bf16(qkv_ref, 0, res)
        store_heads(k_ref, res[:, D_SB:])

    @pl.when(j == _VP_COL)
    def _():
        res = columns()
        store_bf16(qkv_ref, 2 * D_SB, res[:, :D_SB])
        store_heads(v_ref, res[:, :D_SB])
        p_ref[...] = res[:, D_SB:].reshape(p_ref.shape)

    @pl.when(j == _GA_COL)
    def _():
        store_bf16(gab_ref, 0, columns())

    @pl.when(j == _GB_COL)
    def _():
        store_bf16(gab_ref, D_MODEL, columns())


def _proj(x, mod, w_in, seqs, rows):
    B, T, _ = x.shape
    assert 2 * D_SB == D_SB + D_POOL == D_MODEL == PROJ_COLS
    per_seq = T // rows
    last_tile = (B // seqs) * per_seq - 1

    def tile_after(final_step):
        def index(b, t, j):
            tile = jnp.minimum(b * per_seq + t + (j > final_step).astype(jnp.int32), last_tile)
            return tile // per_seq, tile % per_seq
        return index

    def rows_spec(width, final_step=None):
        if final_step is None:
            return pl.BlockSpec((seqs, rows, width), lambda b, t, j: (b, t, 0))
        index = tile_after(final_step)
        return pl.BlockSpec((seqs, rows, width), lambda b, t, j: (*index(b, t, j), 0))

    def heads_spec(final_step):
        index = tile_after(final_step)

        def index_map(b, t, j):
            bb, tt = index(b, t, j)
            return 0, bb, 0, tt, 0
        return pl.BlockSpec((None, seqs, N_HEADS, rows, HEAD_DIM), index_map)

    head_shape = jax.ShapeDtypeStruct((DEPTH, B, N_HEADS, T, HEAD_DIM), F32)
    return pl.pallas_call(
        _proj_kernel,
        grid=(B // seqs, T // rows, D_IN // PROJ_COLS),
        in_specs=[
            rows_spec(D_MODEL),
            pl.BlockSpec((seqs, 6, D_MODEL), lambda b, t, j: (b, 0, 0)),
            pl.BlockSpec((D_MODEL, PROJ_COLS), lambda b, t, j: (0, j)),
        ],
        out_specs=[rows_spec(3 * D_SB, _VP_COL), heads_spec(_QK_COL), heads_spec(_VP_COL), rows_spec(D_POOL, _VP_COL),
                   rows_spec(2 * D_MODEL)],
        out_shape=[
            jax.ShapeDtypeStruct((B, T, 3 * D_SB), BF16),
            head_shape,
            head_shape,
            jax.ShapeDtypeStruct((B, T, D_POOL), F32),
            jax.ShapeDtypeStruct((B, T, 2 * D_MODEL), BF16),
        ],
        scratch_shapes=[pltpu.VMEM((seqs * rows, D_MODEL), BF16)],
        compiler_params=_params("arbitrary", "arbitrary", "arbitrary"),
        name="proj",
    )(x, mod, w_in)


def _suffix_sum_matrix(n):
    r = lax.broadcasted_iota(jnp.int32, (n, n), 0)
    c = lax.broadcasted_iota(jnp.int32, (n, n), 1)
    return jnp.where(r >= c, -1.0, 0.0).astype(BF16)


def _sb_blocks(qs, blocks, sums, carry_ref, acc_ref):
    heads, tq = len(qs), qs[0].shape[0]
    row = lax.broadcasted_iota(jnp.int32, (tq, tq), 0)
    col = lax.broadcasted_iota(jnp.int32, (tq, tq), 1)
    mask = col < row
    zs = [[lax.dot_general(q, k, (((1,), (1,)), ((), ())), preferred_element_type=F32) * SB_SCALE_LOG2
           for q, k in zip(qs, k_blks)] for k_blks, _, _ in blocks]
    terms, first_term = {}, []
    for z_blk, (k_blks, _, diagonal) in zip(zs, blocks):
        of_width = terms.setdefault(k_blks[0].shape[0], [])
        first_term.append(len(of_width))
        for z in z_blk:
            neg_l = jnp.maximum(z, 0.0) + jnp.log2(1.0 + jnp.exp2(-jnp.abs(z)))
            if diagonal:
                neg_l = jnp.where(mask, neg_l, 0.0)
            hi = neg_l.astype(BF16)
            of_width += [hi, (neg_l - hi.astype(F32)).astype(BF16)]
    sums_out = {n: _dot(jnp.concatenate(t, axis=0), sums[n]) for n, t in terms.items()}
    ws = [[] for _ in blocks]
    for h in range(heads):
        carry = carry_ref[h]
        for b, (k_blks, _, diagonal) in enumerate(blocks):
            n = k_blks[0].shape[0]
            at = (first_term[b] + 2 * h) * tq
            incl = sums_out[n][at:at + tq] + sums_out[n][at + tq:at + 2 * tq]
            w = []
            for c in range(0, n, LANES):
                width = min(LANES, n - c)
                w.append(jnp.exp2(zs[b][h][:, c:c + width] + incl[:, c:c + width] + carry[:, :width]))
            w = w[0] if len(w) == 1 else jnp.concatenate(w, axis=1)
            if diagonal:
                w = jnp.where(mask, w, 0.0)
            ws[b].append(w.astype(BF16))
            carry = carry + jnp.broadcast_to(incl[:, :1], carry.shape)
        carry_ref[h] = carry
    pvs = [[_dot(w, v) for w, v in zip(ws[b], v_blks)] for b, (_, v_blks, _) in enumerate(blocks)]
    for h in range(heads):
        acc_ref[h] += functools.reduce(lambda a, b: a + b, [pv[h] for pv in pvs])


def _walk_past(n_blocks, bound, carry_ref, wide_fn, narrow_fn):
    def cond(state):
        j, bound = state
        return (j >= 2) & (bound > UNDERFLOW_LOG2)

    def body(state):
        j, _ = state
        wide_fn(pl.multiple_of((j - 2) * Q_BLOCK, Q_BLOCK))
        return j - 2, jnp.max(carry_ref[...])

    j, bound = lax.while_loop(cond, body, (n_blocks, bound))
    if narrow_fn is not None:
        pl.when((j == 1) & (bound > UNDERFLOW_LOG2))(narrow_fn)


def _head_cols(h):
    return slice(h * HEAD_DIM, (h + 1) * HEAD_DIM)


def _attn_prompt_kernel(q_ref, k_ref, v_ref, o_ref, carry_ref, acc_ref):
    qi = pl.program_id(1)
    carry_ref[...] = jnp.zeros_like(carry_ref)
    acc_ref[...] = jnp.zeros_like(acc_ref)

    hs = [_head_cols(h) for h in range(N_HEADS)]

    def keys(start, n, diagonal):
        rows = pl.ds(pl.multiple_of(start, Q_BLOCK), n)
        return [k_ref[rows, c] for c in hs], [v_ref[rows, c] for c in hs], diagonal

    def visit(blocks, sums):
        _sb_blocks([q_ref[:, c] for c in hs], blocks, sums, carry_ref, acc_ref)

    def walk(n_blocks, bound, sums):
        _walk_past(n_blocks, bound, carry_ref,
                   lambda start: visit([keys(start, PAST_BLOCK, False)], sums),
                   lambda: visit([keys(0, Q_BLOCK, False)], sums))

    eager = EAGER_PAST_BLOCKS * (PAST_BLOCK // Q_BLOCK)

    @pl.when(qi >= eager)
    def _():
        sums = {n: _suffix_sum_matrix(n) for n in (Q_BLOCK, PAST_BLOCK)}
        visit([keys(qi * Q_BLOCK, Q_BLOCK, True)]
              + [keys(qi * Q_BLOCK - b * PAST_BLOCK, PAST_BLOCK, False) for b in range(1, EAGER_PAST_BLOCKS + 1)], sums)
        walk(qi - eager, jnp.max(carry_ref[...]), sums)

    @pl.when(qi < eager)
    def _():
        sums = {n: _suffix_sum_matrix(n) for n in (Q_BLOCK, PAST_BLOCK)}
        visit([keys(qi * Q_BLOCK, Q_BLOCK, True)], sums)
        walk(qi, jnp.float32(0.0), sums)

    for h in range(N_HEADS):
        o_ref[:, _head_cols(h)] = acc_ref[h].astype(o_ref.dtype)


def _attn_prompt(qkv):
    B, T, _ = qkv.shape
    q_spec = pl.BlockSpec((None, Q_BLOCK, D_SB), lambda b, i: (b, i, 0))
    kv_spec = lambda part: pl.BlockSpec((None, T, D_SB), lambda b, i: (b, 0, part))
    return pl.pallas_call(
        _attn_prompt_kernel,
        grid=(B, T // Q_BLOCK),
        in_specs=[q_spec, kv_spec(1), kv_spec(2)],
        out_specs=q_spec,
        out_shape=jax.ShapeDtypeStruct((B, T, D_SB), BF16),
        scratch_shapes=[pltpu.VMEM((N_HEADS, Q_BLOCK, LANES), F32), pltpu.VMEM((N_HEADS, Q_BLOCK, HEAD_DIM), F32)],
        compiler_params=_params("parallel", "arbitrary"),
        name="attn_prompt",
    )(qkv, qkv, qkv)


def _attn_sample_kernel(q_ref, k_ref, v_ref, ck_ref, cv_ref, o_ref, carry_ref, acc_ref):
    tq = q_ref.shape[0]
    carry_ref[...] = jnp.zeros_like(carry_ref)
    acc_ref[...] = jnp.zeros_like(acc_ref)
    hs = [_head_cols(h) for h in range(N_HEADS)]
    qs = [q_ref[:, c] for c in hs]
    sums = {n: _suffix_sum_matrix(n) for n in (tq, PAST_BLOCK)}
    _sb_blocks(qs, [([k_ref[:, c] for c in hs], [v_ref[:, c] for c in hs], True)], sums, carry_ref, acc_ref)

    def visit(start):
        rows = pl.ds(start, PAST_BLOCK)
        cached = ([ck_ref[h, rows, :].astype(BF16) for h in range(N_HEADS)],
                  [cv_ref[h, rows, :].astype(BF16) for h in range(N_HEADS)], False)
        _sb_blocks(qs, [cached], sums, carry_ref, acc_ref)

    past = ck_ref.shape[1]
    assert past % PAST_BLOCK == 0
    _walk_past(past // Q_BLOCK, jnp.float32(0.0), carry_ref, visit, None)
    for h in range(N_HEADS):
        o_ref[:, _head_cols(h)] = acc_ref[h].astype(o_ref.dtype)


def _attn_sample(qkv, cache_k, cache_v):
    B, T, _ = qkv.shape
    past = cache_k.shape[3]
    new_spec = lambda part: pl.BlockSpec((None, T, D_SB), lambda b: (b, 0, part))
    cache_spec = pl.BlockSpec((None, None, N_HEADS, past, HEAD_DIM), lambda b: (0, b, 0, 0, 0))
    return pl.pallas_call(
        _attn_sample_kernel,
        grid=(B,),
        in_specs=[new_spec(0), new_spec(1), new_spec(2), cache_spec, cache_spec],
        out_specs=new_spec(0),
        out_shape=jax.ShapeDtypeStruct((B, T, D_SB), BF16),
        scratch_shapes=[pltpu.VMEM((N_HEADS, T, LANES), F32), pltpu.VMEM((N_HEADS, T, HEAD_DIM), F32)],
        compiler_params=_params("parallel"),
        name="attn_sample",
    )(qkv, qkv, qkv, cache_k, cache_v)


def _gate_kernel(o_ref, p_ref, halo_ref, ga_ref, gb_ref, wsb_ref, wpool_ref, pscale_ref, m_ref,
                 *, pos0, zero_first_halo):
    seqs, rows = p_ref.shape[0], p_ref.shape[1]
    t0 = pl.program_id(1) * rows
    y_a = _dot(o_ref[...].reshape(seqs * rows, D_SB), wsb_ref[...])

    halo = halo_ref[...]
    if zero_first_halo:
        halo = jnp.where(t0 == 0, 0.0, halo)
    pos = pos0 + t0 + lax.broadcasted_iota(jnp.int32, (rows, LANES), 0)

    y_b = []
    for g, win in enumerate(POOL_WINDOWS):
        cols = slice(g * POOL_GROUP, (g + 1) * POOL_GROUP)
        inv_cnt = 1.0 / jnp.minimum(win, pos + 1).astype(F32)
        inv_cnt = jnp.concatenate([inv_cnt] * (POOL_GROUP // LANES), axis=1)
        diffs = []
        for s in range(seqs):
            p = p_ref[s, :, cols]
            acc = jnp.concatenate([halo[s, :, cols], p], axis=0)
            shift = 1
            while shift < win:
                acc = acc + pltpu.roll(acc, shift, 0)
                shift *= 2
            diffs.append(acc[POOL_HALO:, :] * inv_cnt - p)
        d = jnp.concatenate(diffs, axis=0).astype(BF16)
        y_b.append(_dot(d, wpool_ref[g]))
    y_b = jnp.concatenate(y_b, axis=1) * pscale_ref[...]

    gate_a = jax.nn.sigmoid(ga_ref[...].astype(F32)).reshape(seqs * rows, D_MODEL)
    gate_b = jax.nn.sigmoid(gb_ref[...].astype(F32)).reshape(seqs * rows, D_MODEL)
    m_ref[...] = (gate_a * y_a + gate_b * y_b).astype(BF16).reshape(m_ref.shape)


def _gate(o, p, halo_src, gab, w_sb_out, w_pool, pool_scale, seqs, rows, pos0, halo_from_p):
    B, T, _ = p.shape
    row_block = lambda width, part=0: pl.BlockSpec((seqs, rows, width), lambda b, t: (b, t, part))
    if halo_from_p:
        per_tile = rows // POOL_HALO
        halo_spec = pl.BlockSpec((seqs, POOL_HALO, D_POOL),
                                 lambda b, t: (b, jnp.maximum(t * per_tile - 1, 0), 0))
    else:
        halo_spec = pl.BlockSpec((seqs, POOL_HALO, D_POOL), lambda b, t: (b, 0, 0))
    const = lambda shape: pl.BlockSpec(shape, lambda b, t: (0,) * len(shape))
    return pl.pallas_call(
        functools.partial(_gate_kernel, pos0=pos0, zero_first_halo=halo_from_p),
        grid=(B // seqs, T // rows),
        in_specs=[
            row_block(D_SB), row_block(D_POOL), halo_spec, row_block(D_MODEL, 0), row_block(D_MODEL, 1),
            const((D_SB, D_MODEL)), const((N_POOL_GROUPS, POOL_GROUP, POOL_OUT)), const((1, D_MODEL)),
        ],
        out_specs=row_block(D_MODEL),
        out_shape=jax.ShapeDtypeStruct((B, T, D_MODEL), BF16),
        compiler_params=_params("parallel", "arbitrary"),
        name="gate",
    )(o, p, halo_src, gab, gab, w_sb_out, w_pool, pool_scale)


def _mix_kernel(m_ref, x_ref, mod_ref, w_ref, g_ref, b_ref, o_ref):
    seqs, rows = x_ref.shape[0], x_ref.shape[1]
    for s, r in _row_chunks(seqs, rows):
        mix = _dot(m_ref[s, r, :].reshape(-1, D_MODEL), w_ref[...])
        o_ref[s, r, :] = _residual_ln(x_ref.at[s, r, :], mod_ref.at[s], 2, mix, g_ref, b_ref)


def _mix(m, x, mod, w_out, ln_g, ln_b, seqs, rows):
    B, T, _ = x.shape
    row_block = pl.BlockSpec((seqs, rows, D_MODEL), lambda b, t: (b, t, 0))
    vec = pl.BlockSpec((1, D_MODEL), lambda b, t: (0, 0))
    return pl.pallas_call(
        _mix_kernel,
        grid=(B // seqs, T // rows),
        in_specs=[
            row_block, row_block,
            pl.BlockSpec((seqs, 6, D_MODEL), lambda b, t: (b, 0, 0)),
            pl.BlockSpec((D_MODEL, D_MODEL), lambda b, t: (0, 0)),
            vec, vec,
        ],
        out_specs=row_block,
        out_shape=jax.ShapeDtypeStruct((B, T, D_MODEL), F32),
        compiler_params=_params("parallel", "arbitrary"),
        name="mix",
    )(m, x, mod, w_out, ln_g, ln_b)


def _ffn_kernel(x_ref, mod_ref, wg_ref, wu_ref, wd_ref, g_ref, b_ref, o_ref, u_ref, acc_ref):
    j = pl.program_id(2)

    @pl.when(j == 0)
    def _():
        u_ref[...] = _modulated_ln(x_ref, mod_ref, 3, 4).astype(BF16)
        acc_ref[...] = jnp.zeros_like(acc_ref)

    u = u_ref[...]
    h = jax.nn.silu(_dot(u, wg_ref[...])) * _dot(u, wu_ref[...])
    acc_ref[...] += _dot(h.astype(BF16), wd_ref[...])

    @pl.when(j == pl.num_programs(2) - 1)
    def _():
        o_ref[...] = _residual_ln(x_ref, mod_ref, 5, acc_ref[...], g_ref, b_ref)


def _ffn(x, mod, w_gate, w_up, w_down, ln_g, ln_b, seqs, rows):
    B, T, _ = x.shape
    d_ff = w_gate.shape[1]
    row_block = pl.BlockSpec((seqs, rows, D_MODEL), lambda b, t, j: (b, t, 0))
    vec = pl.BlockSpec((1, D_MODEL), lambda b, t, j: (0, 0))
    return pl.pallas_call(
        _ffn_kernel,
        grid=(B // seqs, T // rows, d_ff // FF_COLS),
        in_specs=[
            row_block,
            pl.BlockSpec((seqs, 6, D_MODEL), lambda b, t, j: (b, 0, 0)),
            pl.BlockSpec((D_MODEL, FF_COLS), lambda b, t, j: (0, j)),
            pl.BlockSpec((D_MODEL, FF_COLS), lambda b, t, j: (0, j)),
            pl.BlockSpec((FF_COLS, D_MODEL), lambda b, t, j: (j, 0)),
            vec, vec,
        ],
        out_specs=row_block,
        out_shape=jax.ShapeDtypeStruct((B, T, D_MODEL), F32),
        scratch_shapes=[pltpu.VMEM((seqs * rows, D_MODEL), BF16), pltpu.VMEM((seqs * rows, D_MODEL), F32)],
        compiler_params=_params("parallel", "parallel", "arbitrary"),
        name="ffn",
    )(x, mod, w_gate, w_up, w_down, ln_g, ln_b)


def _tile(batch, seq, row_tile=ROW_TILE):
    if seq >= row_tile:
        return 1, row_tile
    return min(batch, row_tile // seq), seq


def _layer(x, mod, attn_fn, halo_src, pos0, halo_from_p, w):
    B, T, _ = x.shape
    seqs, rows = _tile(B, T)
    qkv, k, v, p, gab = _proj(x, mod, w["w_in"], seqs, rows)
    o = attn_fn(qkv)
    m = _gate(o, p, p if halo_from_p else halo_src, gab, w["w_sb_out"], w["w_pool"], w["pool_scale"],
              seqs, rows, pos0, halo_from_p)
    x = _mix(m, x, mod, w["w_out"], w["ln1_g"], w["ln1_b"], seqs, rows)
    x = _ffn(x, mod, w["w_gate"], w["w_up"], w["w_down"], w["ln2_g"], w["ln2_b"], seqs, rows)
    return x, k, v, p[None, :, T - POOL_PAST:, :]


def kernel(x_prompt, x_sample, c_prompt, c_sample, cache_k, cache_v, state_pool, w_ada, b_ada, w_in, w_sb_out, w_pool, pool_scale, w_out, ln1_g, ln1_b, w_gate, w_up, w_down, ln2_g, ln2_b):
    assert w_in.shape[0] == DEPTH
    n_prompt = c_prompt.shape[0]
    mod = _adaln(jnp.concatenate([c_prompt, c_sample], axis=0), w_ada[0], b_ada[0])
    mod = mod.reshape(mod.shape[0], 6, D_MODEL)
    w = {
        "w_in": w_in[0].astype(BF16), "w_sb_out": w_sb_out[0].astype(BF16), "w_pool": w_pool[0].astype(BF16),
        "w_out": w_out[0].astype(BF16), "w_gate": w_gate[0].astype(BF16), "w_up": w_up[0].astype(BF16),
        "w_down": w_down[0].astype(BF16), "pool_scale": pool_scale, "ln1_g": ln1_g, "ln1_b": ln1_b,
        "ln2_g": ln2_g, "ln2_b": ln2_b,
    }
    y_p, k_p, v_p, pool_p = _layer(x_prompt, mod[:n_prompt], _attn_prompt, None, 0, True, w)

    pool_halo = jnp.pad(state_pool[0], ((0, 0), (POOL_HALO - POOL_PAST, 0), (0, 0)))
    attn_sample = lambda qkv: _attn_sample(qkv, cache_k, cache_v)
    y_s, k_s, v_s, pool_s = _layer(x_sample, mod[n_prompt:], attn_sample, pool_halo, cache_k.shape[3], False, w)
    return y_p, y_s, k_p, v_p, pool_p, k_s, v_s, pool_s
```

```python
import functools
import math

import jax
import jax.numpy as jnp
from jax import lax
from jax.experimental import pallas as pl
from jax.experimental.pallas import tpu as pltpu

D_MODEL = 2048
N_HEADS = 8
HEAD_DIM = 128
D_SB = N_HEADS * HEAD_DIM
POOL_WINDOWS = (2, 4, 8, 16)
N_POOL_GROUPS = len(POOL_WINDOWS)
D_POOL = 1024
POOL_GROUP = D_POOL // N_POOL_GROUPS
POOL_OUT = D_MODEL // N_POOL_GROUPS
POOL_PAST = max(POOL_WINDOWS) - 1
POOL_HALO = POOL_PAST + 1
D_IN = 3 * D_SB + D_POOL + 2 * D_MODEL
DEPTH = 1
DN_ALPHA = (2 * DEPTH) ** 0.25
LN_EPS = 1e-5

LOG2_E = math.log2(math.e)
SB_SCALE_LOG2 = HEAD_DIM ** -0.5 * LOG2_E
UNDERFLOW_LOG2 = -110.0 * LOG2_E

V7X_VMEM_BYTES = 64 * 1024 * 1024
VMEM_LIMIT = V7X_VMEM_BYTES - 6 * 1024 * 1024
LANES = 128

ROW_TILE = 512
ROW_CHUNKS = 2
PROJ_COLS = 2048
ADA_COLS = 1024
FF_COLS = 512
Q_BLOCK = 128
PAST_BLOCK = 2 * Q_BLOCK
EAGER_PAST_BLOCKS = 2

BF16 = jnp.bfloat16
F32 = jnp.float32


def _params(*sem):
    return pltpu.CompilerParams(dimension_semantics=sem, vmem_limit_bytes=VMEM_LIMIT)


def _dot(a, b):
    return jnp.dot(a, b, preferred_element_type=F32)


def _normalize(x):
    mu = jnp.mean(x, axis=-1, keepdims=True)
    xc = x - mu
    var = jnp.mean(xc * xc, axis=-1, keepdims=True)
    return xc * lax.rsqrt(var + LN_EPS)


def _modulated_ln(x_ref, mod_ref, shift_idx, scale_idx):
    x = x_ref[...]
    u = _normalize(x) * (1.0 + mod_ref[:, scale_idx:scale_idx + 1, :]) + mod_ref[:, shift_idx:shift_idx + 1, :]
    return u.reshape(x.shape[0] * x.shape[1], x.shape[2])


def _row_chunks(seqs, rows):
    if seqs >= ROW_CHUNKS:
        n = seqs // ROW_CHUNKS
        return [(slice(i * n, (i + 1) * n), slice(0, rows)) for i in range(ROW_CHUNKS)]
    n = rows // ROW_CHUNKS
    return [(slice(0, seqs), slice(i * n, (i + 1) * n)) for i in range(ROW_CHUNKS)]


def _residual_ln(x_ref, mod_ref, gate_idx, branch, g_ref, b_ref):
    x = x_ref[...]
    y = DN_ALPHA * x + mod_ref[:, gate_idx:gate_idx + 1, :] * branch.reshape(x.shape)
    return _normalize(y) * g_ref[...] + b_ref[...]


def _adaln_kernel(c_ref, w_ref, b_ref, o_ref):
    a = jax.nn.silu(c_ref[...]).astype(BF16)
    o_ref[...] = _dot(a, w_ref[...].astype(BF16)) + b_ref[...]


def _adaln(c, w_ada, b_ada):
    n = c.shape[0]
    cols = w_ada.shape[1]
    return pl.pallas_call(
        _adaln_kernel,
        grid=(cols // ADA_COLS,),
        in_specs=[
            pl.BlockSpec((n, D_MODEL), lambda j: (0, 0)),
            pl.BlockSpec((D_MODEL, ADA_COLS), lambda j: (0, j)),
            pl.BlockSpec((1, ADA_COLS), lambda j: (0, j)),
        ],
        out_specs=pl.BlockSpec((n, ADA_COLS), lambda j: (0, j)),
        out_shape=jax.ShapeDtypeStruct((n, cols), F32),
        compiler_params=_params("arbitrary"),
        name="adaln",
    )(c, w_ada, b_ada.reshape(1, cols))


_QK_COL, _VP_COL, _GA_COL, _GB_COL = range(4)


def _proj_kernel(x_ref, mod_ref, w_ref, qkv_ref, k_ref, v_ref, p_ref, gab_ref, u_ref):
    j = pl.program_id(2)
    seqs, rows = x_ref.shape[0], x_ref.shape[1]

    @pl.when(j == 0)
    def _():
        u_ref[...] = _modulated_ln(x_ref, mod_ref, 0, 1).astype(BF16)

    def columns():
        return _dot(u_ref[...], w_ref[...])

    def store_heads(ref, res):
        for s in range(seqs):
            for h in range(N_HEADS):
                ref[s, h] = res[s * rows:(s + 1) * rows, h * HEAD_DIM:(h + 1) * HEAD_DIM]

    def store_bf16(ref, first_col, res):
        ref[:, :, first_col:first_col + res.shape[1]] = res.astype(BF16).reshape(seqs, rows, res.shape[1])

    @pl.when(j == _QK_COL)
    def _():
        res = columns()
        store_bf16(qkv_ref, 0, res)
        store_heads(k_ref, res[:, D_SB:])

    @pl.when(j == _VP_COL)
    def _():
        res = columns()
        store_bf16(qkv_ref, 2 * D_SB, res[:, :D_SB])
        store_heads(v_ref, res[:, :D_SB])
        p_ref[...] = res[:, D_SB:].reshape(p_ref.shape)

    @pl.when(j == _GA_COL)
    def _():
        store_bf16(gab_ref, 0, columns())

    @pl.when(j == _GB_COL)
    def _():
        store_bf16(gab_ref, D_MODEL, columns())


def _proj(x, mod, w_in, seqs, rows):
    B, T, _ = x.shape
    assert 2 * D_SB == D_SB + D_POOL == D_MODEL == PROJ_COLS
    per_seq = T // rows
    last_tile = (B // seqs) * per_seq - 1

    def tile_after(final_step):
        def index(b, t, j):
            tile = jnp.minimum(b * per_seq + t + (j > final_step).astype(jnp.int32), last_tile)
            return tile // per_seq, tile % per_seq
        return index

    def rows_spec(width, final_step=None):
        if final_step is None:
            return pl.BlockSpec((seqs, rows, width), lambda b, t, j: (b, t, 0))
        index = tile_after(final_step)
        return pl.BlockSpec((seqs, rows, width), lambda b, t, j: (*index(b, t, j), 0))

    def heads_spec(final_step):
        index = tile_after(final_step)

        def index_map(b, t, j):
            bb, tt = index(b, t, j)
            return 0, bb, 0, tt, 0
        return pl.BlockSpec((None, seqs, N_HEADS, rows, HEAD_DIM), index_map)

    head_shape = jax.ShapeDtypeStruct((DEPTH, B, N_HEADS, T, HEAD_DIM), F32)
    return pl.pallas_call(
        _proj_kernel,
        grid=(B // seqs, T // rows, D_IN // PROJ_COLS),
        in_specs=[
            rows_spec(D_MODEL, 0),
            pl.BlockSpec((seqs, 6, D_MODEL), lambda b, t, j: (b, 0, 0)),
            pl.BlockSpec((D_MODEL, PROJ_COLS), lambda b, t, j: (0, j)),
        ],
        out_specs=[rows_spec(3 * D_SB, _VP_COL), heads_spec(_QK_COL), heads_spec(_VP_COL), rows_spec(D_POOL, _VP_COL),
                   rows_spec(2 * D_MODEL)],
        out_shape=[
            jax.ShapeDtypeStruct((B, T, 3 * D_SB), BF16),
            head_shape,
            head_shape,
            jax.ShapeDtypeStruct((B, T, D_POOL), F32),
            jax.ShapeDtypeStruct((B, T, 2 * D_MODEL), BF16),
        ],
        scratch_shapes=[pltpu.VMEM((seqs * rows, D_MODEL), BF16)],
        compiler_params=_params("arbitrary", "arbitrary", "arbitrary"),
        name="proj",
    )(x, mod, w_in)


def _suffix_sum_matrix(n):
    r = lax.broadcasted_iota(jnp.int32, (n, n), 0)
    c = lax.broadcasted_iota(jnp.int32, (n, n), 1)
    return jnp.where(r >= c, -1.0, 0.0).astype(BF16)


def _sb_blocks(qs, blocks, sums, carry_ref, acc_ref):
    heads, tq = len(qs), qs[0].shape[0]
    row = lax.broadcasted_iota(jnp.int32, (tq, tq), 0)
    col = lax.broadcasted_iota(jnp.int32, (tq, tq), 1)
    mask = col < row
    zs = [[lax.dot_general(q, k, (((1,), (1,)), ((), ())), preferred_element_type=F32) * SB_SCALE_LOG2
           for q, k in zip(qs, k_blks)] for k_blks, _, _ in blocks]
    terms, first_term = {}, []
    for z_blk, (k_blks, _, diagonal) in zip(zs, blocks):
        of_width = terms.setdefault(k_blks[0].shape[0], [])
        first_term.append(len(of_width))
        for z in z_blk:
            neg_l = jnp.maximum(z, 0.0) + jnp.log2(1.0 + jnp.exp2(-jnp.abs(z)))
            if diagonal:
                neg_l = jnp.where(mask, neg_l, 0.0)
            hi = neg_l.astype(BF16)
            of_width += [hi, (neg_l - hi.astype(F32)).astype(BF16)]
    sums_out = {n: _dot(jnp.concatenate(t, axis=0), sums[n]) for n, t in terms.items()}
    ws = [[] for _ in blocks]
    for h in range(heads):
        carry = carry_ref[h]
        for b, (k_blks, _, diagonal) in enumerate(blocks):
            n = k_blks[0].shape[0]
            at = (first_term[b] + 2 * h) * tq
            incl = sums_out[n][at:at + tq] + sums_out[n][at + tq:at + 2 * tq]
            w = []
            for c in range(0, n, LANES):
                width = min(LANES, n - c)
                w.append(jnp.exp2(zs[b][h][:, c:c + width] + incl[:, c:c + width] + carry[:, :width]))
            w = w[0] if len(w) == 1 else jnp.concatenate(w, axis=1)
            if diagonal:
                w = jnp.where(mask, w, 0.0)
            ws[b].append(w.astype(BF16))
            carry = carry + jnp.broadcast_to(incl[:, :1], carry.shape)
        carry_ref[h] = carry
    pvs = [[_dot(w, v) for w, v in zip(ws[b], v_blks)] for b, (_, v_blks, _) in enumerate(blocks)]
    for h in range(heads):
        acc_ref[h] += functools.reduce(lambda a, b: a + b, [pv[h] for pv in pvs])


def _walk_past(n_blocks, bound, carry_ref, wide_fn, narrow_fn):
    def cond(state):
        j, bound = state
        return (j >= 2) & (bound > UNDERFLOW_LOG2)

    def body(state):
        j, _ = state
        wide_fn(pl.multiple_of((j - 2) * Q_BLOCK, Q_BLOCK))
        return j - 2, jnp.max(carry_ref[...])

    j, bound = lax.while_loop(cond, body, (n_blocks, bound))
    if narrow_fn is not None:
        pl.when((j == 1) & (bound > UNDERFLOW_LOG2))(narrow_fn)


def _head_cols(h):
    return slice(h * HEAD_DIM, (h + 1) * HEAD_DIM)


def _attn_prompt_kernel(q_ref, k_ref, v_ref, o_ref, carry_ref, acc_ref):
    qi = pl.program_id(1)
    carry_ref[...] = jnp.zeros_like(carry_ref)
    acc_ref[...] = jnp.zeros_like(acc_ref)

    hs = [_head_cols(h) for h in range(N_HEADS)]

    def keys(start, n, diagonal):
        rows = pl.ds(pl.multiple_of(start, Q_BLOCK), n)
        return [k_ref[rows, c] for c in hs], [v_ref[rows, c] for c in hs], diagonal

    def visit(blocks, sums):
        _sb_blocks([q_ref[:, c] for c in hs], blocks, sums, carry_ref, acc_ref)

    def walk(n_blocks, bound, sums):
        _walk_past(n_blocks, bound, carry_ref,
                   lambda start: visit([keys(start, PAST_BLOCK, False)], sums),
                   lambda: visit([keys(0, Q_BLOCK, False)], sums))

    eager = EAGER_PAST_BLOCKS * (PAST_BLOCK // Q_BLOCK)

    @pl.when(qi >= eager)
    def _():
        sums = {n: _suffix_sum_matrix(n) for n in (Q_BLOCK, PAST_BLOCK)}
        visit([keys(qi * Q_BLOCK, Q_BLOCK, True)]
              + [keys(qi * Q_BLOCK - b * PAST_BLOCK, PAST_BLOCK, False) for b in range(1, EAGER_PAST_BLOCKS + 1)], sums)
        walk(qi - eager, jnp.max(carry_ref[...]), sums)

    @pl.when(qi < eager)
    def _():
        sums = {n: _suffix_sum_matrix(n) for n in (Q_BLOCK, PAST_BLOCK)}
        visit([keys(qi * Q_BLOCK, Q_BLOCK, True)], sums)
        walk(qi, jnp.float32(0.0), sums)

    for h in range(N_HEADS):
        o_ref[:, _head_cols(h)] = acc_ref[h].astype(o_ref.dtype)


def _attn_prompt(qkv):
    B, T, _ = qkv.shape
    q_spec = pl.BlockSpec((None, Q_BLOCK, D_SB), lambda b, i: (b, i, 0))
    kv_spec = lambda part: pl.BlockSpec((None, T, D_SB), lambda b, i: (b, 0, part))
    return pl.pallas_call(
        _attn_prompt_kernel,
        grid=(B, T // Q_BLOCK),
        in_specs=[q_spec, kv_spec(1), kv_spec(2)],
        out_specs=q_spec,
        out_shape=jax.ShapeDtypeStruct((B, T, D_SB), BF16),
        scratch_shapes=[pltpu.VMEM((N_HEADS, Q_BLOCK, LANES), F32), pltpu.VMEM((N_HEADS, Q_BLOCK, HEAD_DIM), F32)],
        compiler_params=_params("parallel", "arbitrary"),
        name="attn_prompt",
    )(qkv, qkv, qkv)


def _attn_sample_kernel(q_ref, k_ref, v_ref, ck_ref, cv_ref, o_ref, carry_ref, acc_ref):
    tq = q_ref.shape[0]
    carry_ref[...] = jnp.zeros_like(carry_ref)
    acc_ref[...] = jnp.zeros_like(acc_ref)
    hs = [_head_cols(h) for h in range(N_HEADS)]
    qs = [q_ref[:, c] for c in hs]
    sums = {n: _suffix_sum_matrix(n) for n in (tq, PAST_BLOCK)}
    _sb_blocks(qs, [([k_ref[:, c] for c in hs], [v_ref[:, c] for c in hs], True)], sums, carry_ref, acc_ref)

    def visit(start):
        rows = pl.ds(start, PAST_BLOCK)
        cached = ([ck_ref[h, rows, :].astype(BF16) for h in range(N_HEADS)],
                  [cv_ref[h, rows, :].astype(BF16) for h in range(N_HEADS)], False)
        _sb_blocks(qs, [cached], sums, carry_ref, acc_ref)

    past = ck_ref.shape[1]
    assert past % PAST_BLOCK == 0
    _walk_past(past // Q_BLOCK, jnp.float32(0.0), carry_ref, visit, None)
    for h in range(N_HEADS):
        o_ref[:, _head_cols(h)] = acc_ref[h].astype(o_ref.dtype)


def _attn_sample(qkv, cache_k, cache_v):
    B, T, _ = qkv.shape
    past = cache_k.shape[3]
    new_spec = lambda part: pl.BlockSpec((None, T, D_SB), lambda b: (b, 0, part))
    cache_spec = pl.BlockSpec((None, None, N_HEADS, past, HEAD_DIM), lambda b: (0, b, 0, 0, 0))
    return pl.pallas_call(
        _attn_sample_kernel,
        grid=(B,),
        in_specs=[new_spec(0), new_spec(1), new_spec(2), cache_spec, cache_spec],
        out_specs=new_spec(0),
        out_shape=jax.ShapeDtypeStruct((B, T, D_SB), BF16),
        scratch_shapes=[pltpu.VMEM((N_HEADS, T, LANES), F32), pltpu.VMEM((N_HEADS, T, HEAD_DIM), F32)],
        compiler_params=_params("parallel"),
        name="attn_sample",
    )(qkv, qkv, qkv, cache_k, cache_v)


def _gate_kernel(o_ref, p_ref, halo_ref, ga_ref, gb_ref, wsb_ref, wpool_ref, pscale_ref, m_ref,
                 *, pos0, zero_first_halo):
    seqs, rows = p_ref.shape[0], p_ref.shape[1]
    t0 = pl.program_id(1) * rows
    y_a = _dot(o_ref[...].reshape(seqs * rows, D_SB), wsb_ref[...])

    halo = halo_ref[...]
    if zero_first_halo:
        halo = jnp.where(t0 == 0, 0.0, halo)
    pos = pos0 + t0 + lax.broadcasted_iota(jnp.int32, (rows, LANES), 0)

    y_b = []
    for g, win in enumerate(POOL_WINDOWS):
        cols = slice(g * POOL_GROUP, (g + 1) * POOL_GROUP)
        inv_cnt = 1.0 / jnp.minimum(win, pos + 1).astype(F32)
        inv_cnt = jnp.concatenate([inv_cnt] * (POOL_GROUP // LANES), axis=1)
        diffs = []
        for s in range(seqs):
            p = p_ref[s, :, cols]
            acc = jnp.concatenate([halo[s, :, cols], p], axis=0)
            shift = 1
            while shift < win:
                acc = acc + pltpu.roll(acc, shift, 0)
                shift *= 2
            diffs.append(acc[POOL_HALO:, :] * inv_cnt - p)
        d = jnp.concatenate(diffs, axis=0).astype(BF16)
        y_b.append(_dot(d, wpool_ref[g]))
    y_b = jnp.concatenate(y_b, axis=1) * pscale_ref[...]

    gate_a = jax.nn.sigmoid(ga_ref[...].astype(F32)).reshape(seqs * rows, D_MODEL)
    gate_b = jax.nn.sigmoid(gb_ref[...].astype(F32)).reshape(seqs * rows, D_MODEL)
    m_ref[...] = (gate_a * y_a + gate_b * y_b).astype(BF16).reshape(m_ref.shape)


def _gate(o, p, halo_src, gab, w_sb_out, w_pool, pool_scale, seqs, rows, pos0, halo_from_p):
    B, T, _ = p.shape
    row_block = lambda width, part=0: pl.BlockSpec((seqs, rows, width), lambda b, t: (b, t, part))
    if halo_from_p:
        per_tile = rows // POOL_HALO
        halo_spec = pl.BlockSpec((seqs, POOL_HALO, D_POOL),
                                 lambda b, t: (b, jnp.maximum(t * per_tile - 1, 0), 0))
    else:
        halo_spec = pl.BlockSpec((seqs, POOL_HALO, D_POOL), lambda b, t: (b, 0, 0))
    const = lambda shape: pl.BlockSpec(shape, lambda b, t: (0,) * len(shape))
    return pl.pallas_call(
        functools.partial(_gate_kernel, pos0=pos0, zero_first_halo=halo_from_p),
        grid=(B // seqs, T // rows),
        in_specs=[
            row_block(D_SB), row_block(D_POOL), halo_spec, row_block(D_MODEL, 0), row_block(D_MODEL, 1),
            const((D_SB, D_MODEL)), const((N_POOL_GROUPS, POOL_GROUP, POOL_OUT)), const((1, D_MODEL)),
        ],
        out_specs=row_block(D_MODEL),
        out_shape=jax.ShapeDtypeStruct((B, T, D_MODEL), BF16),
        compiler_params=_params("parallel", "arbitrary"),
        name="gate",
    )(o, p, halo_src, gab, gab, w_sb_out, w_pool, pool_scale)


def _mix_kernel(m_ref, x_ref, mod_ref, w_ref, g_ref, b_ref, o_ref):
    seqs, rows = x_ref.shape[0], x_ref.shape[1]
    for s, r in _row_chunks(seqs, rows):
        mix = _dot(m_ref[s, r, :].reshape(-1, D_MODEL), w_ref[...])
        o_ref[s, r, :] = _residual_ln(x_ref.at[s, r, :], mod_ref.at[s], 2, mix, g_ref, b_ref)


def _mix(m, x, mod, w_out, ln_g, ln_b, seqs, rows):
    B, T, _ = x.shape
    row_block = pl.BlockSpec((seqs, rows, D_MODEL), lambda b, t: (b, t, 0))
    vec = pl.BlockSpec((1, D_MODEL), lambda b, t: (0, 0))
    return pl.pallas_call(
        _mix_kernel,
        grid=(B // seqs, T // rows),
        in_specs=[
            row_block, row_block,
            pl.BlockSpec((seqs, 6, D_MODEL), lambda b, t: (b, 0, 0)),
            pl.BlockSpec((D_MODEL, D_MODEL), lambda b, t: (0, 0)),
            vec, vec,
        ],
        out_specs=row_block,
        out_shape=jax.ShapeDtypeStruct((B, T, D_MODEL), F32),
        compiler_params=_params("parallel", "arbitrary"),
        name="mix",
    )(m, x, mod, w_out, ln_g, ln_b)


def _ffn_kernel(x_ref, mod_ref, wg_ref, wu_ref, wd_ref, g_ref, b_ref, o_ref, u_ref, acc_ref):
    j = pl.program_id(2)

    @pl.when(j == 0)
    def _():
        u_ref[...] = _modulated_ln(x_ref, mod_ref, 3, 4).astype(BF16)
        acc_ref[...] = jnp.zeros_like(acc_ref)

    u = u_ref[...]
    h = jax.nn.silu(_dot(u, wg_ref[...])) * _dot(u, wu_ref[...])
    acc_ref[...] += _dot(h.astype(BF16), wd_ref[...])

    @pl.when(j == pl.num_programs(2) - 1)
    def _():
        o_ref[...] = _residual_ln(x_ref, mod_ref, 5, acc_ref[...], g_ref, b_ref)


def _ffn(x, mod, w_gate, w_up, w_down, ln_g, ln_b, seqs, rows):
    B, T, _ = x.shape
    d_ff = w_gate.shape[1]
    row_block = pl.BlockSpec((seqs, rows, D_MODEL), lambda b, t, j: (b, t, 0))
    vec = pl.BlockSpec((1, D_MODEL), lambda b, t, j: (0, 0))
    return pl.pallas_call(
        _ffn_kernel,
        grid=(B // seqs, T // rows, d_ff // FF_COLS),
        in_specs=[
            row_block,
            pl.BlockSpec((seqs, 6, D_MODEL), lambda b, t, j: (b, 0, 0)),
            pl.BlockSpec((D_MODEL, FF_COLS), lambda b, t, j: (0, j)),
            pl.BlockSpec((D_MODEL, FF_COLS), lambda b, t, j: (0, j)),
            pl.BlockSpec((FF_COLS, D_MODEL), lambda b, t, j: (j, 0)),
            vec, vec,
        ],
        out_specs=row_block,
        out_shape=jax.ShapeDtypeStruct((B, T, D_MODEL), F32),
        scratch_shapes=[pltpu.VMEM((seqs * rows, D_MODEL), BF16), pltpu.VMEM((seqs * rows, D_MODEL), F32)],
        compiler_params=_params("parallel", "parallel", "arbitrary"),
        name="ffn",
    )(x, mod, w_gate, w_up, w_down, ln_g, ln_b)


def _tile(batch, seq, row_tile=ROW_TILE):
    if seq >= row_tile:
        return 1, row_tile
    return min(batch, row_tile // seq), seq


def _layer(x, mod, attn_fn, halo_src, pos0, halo_from_p, w):
    B, T, _ = x.shape
    seqs, rows = _tile(B, T)
    qkv, k, v, p, gab = _proj(x, mod, w["w_in"], seqs, rows)
    o = attn_fn(qkv)
    m = _gate(o, p, p if halo_from_p else halo_src, gab, w["w_sb_out"], w["w_pool"], w["pool_scale"],
              seqs, rows, pos0, halo_from_p)
    x = _mix(m, x, mod, w["w_out"], w["ln1_g"], w["ln1_b"], seqs, rows)
    x = _ffn(x, mod, w["w_gate"], w["w_up"], w["w_down"], w["ln2_g"], w["ln2_b"], seqs, rows)
    return x, k, v, p[None, :, T - POOL_PAST:, :]


def kernel(x_prompt, x_sample, c_prompt, c_sample, cache_k, cache_v, state_pool, w_ada, b_ada, w_in, w_sb_out, w_pool, pool_scale, w_out, ln1_g, ln1_b, w_gate, w_up, w_down, ln2_g, ln2_b):
    assert w_in.shape[0] == DEPTH
    n_prompt = c_prompt.shape[0]
    mod = _adaln(jnp.concatenate([c_prompt, c_sample], axis=0), w_ada[0], b_ada[0])
    mod = mod.reshape(mod.shape[0], 6, D_MODEL)
    w = {
        "w_in": w_in[0].astype(BF16), "w_sb_out": w_sb_out[0].astype(BF16), "w_pool": w_pool[0].astype(BF16),
        "w_out": w_out[0].astype(BF16), "w_gate": w_gate[0].astype(BF16), "w_up": w_up[0].astype(BF16),
        "w_down": w_down[0].astype(BF16), "pool_scale": pool_scale, "ln1_g": ln1_g, "ln1_b": ln1_b,
        "ln2_g": ln2_g, "ln2_b": ln2_b,
    }
    y_p, k_p, v_p, pool_p = _layer(x_prompt, mod[:n_prompt], _attn_prompt, None, 0, True, w)

    pool_halo = jnp.pad(state_pool[0], ((0, 0), (POOL_HALO - POOL_PAST, 0), (0, 0)))
    attn_sample = lambda qkv: _attn_sample(qkv, cache_k, cache_v)
    y_s, k_s, v_s, pool_s = _layer(x_sample, mod[n_prompt:], attn_sample, pool_halo, cache_k.shape[3], False, w)
    return y_p, y_s, k_p, v_p, pool_p, k_s, v_s, pool_s
```

```python
import functools
import math

import jax
import jax.numpy as jnp
from jax import lax
from jax.experimental import pallas as pl
from jax.experimental.pallas import tpu as pltpu

D_MODEL = 2048
N_HEADS = 8
HEAD_DIM = 128
D_SB = N_HEADS * HEAD_DIM
POOL_WINDOWS = (2, 4, 8, 16)
N_POOL_GROUPS = len(POOL_WINDOWS)
D_POOL = 1024
POOL_GROUP = D_POOL // N_POOL_GROUPS
POOL_OUT = D_MODEL // N_POOL_GROUPS
POOL_PAST = max(POOL_WINDOWS) - 1
POOL_HALO = POOL_PAST + 1
D_IN = 3 * D_SB + D_POOL + 2 * D_MODEL
DEPTH = 1
DN_ALPHA = (2 * DEPTH) ** 0.25
LN_EPS = 1e-5

LOG2_E = math.log2(math.e)
SB_SCALE_LOG2 = HEAD_DIM ** -0.5 * LOG2_E
UNDERFLOW_LOG2 = -110.0 * LOG2_E

V7X_VMEM_BYTES = 64 * 1024 * 1024
VMEM_LIMIT = V7X_VMEM_BYTES - 6 * 1024 * 1024
LANES = 128

ROW_TILE = 512
ROW_CHUNKS = 2
PROJ_COLS = 2048
ADA_COLS = 1024
FF_COLS = 512
Q_BLOCK = 128
PAST_BLOCK = 2 * Q_BLOCK
EAGER_PAST_BLOCKS = 2

BF16 = jnp.bfloat16
F32 = jnp.float32


def _params(*sem):
    return pltpu.CompilerParams(dimension_semantics=sem, vmem_limit_bytes=VMEM_LIMIT)


def _dot(a, b):
    return jnp.dot(a, b, preferred_element_type=F32)


def _normalize(x):
    mu = jnp.mean(x, axis=-1, keepdims=True)
    xc = x - mu
    var = jnp.mean(xc * xc, axis=-1, keepdims=True)
    return xc * lax.rsqrt(var + LN_EPS)


def _modulated_ln(x_ref, mod_ref, shift_idx, scale_idx):
    x = x_ref[...]
    u = _normalize(x) * (1.0 + mod_ref[:, scale_idx:scale_idx + 1, :]) + mod_ref[:, shift_idx:shift_idx + 1, :]
    return u.reshape(x.shape[0] * x.shape[1], x.shape[2])


def _row_chunks(seqs, rows):
    if seqs >= ROW_CHUNKS:
        n = seqs // ROW_CHUNKS
        return [(slice(i * n, (i + 1) * n), slice(0, rows)) for i in range(ROW_CHUNKS)]
    n = rows // ROW_CHUNKS
    return [(slice(0, seqs), slice(i * n, (i + 1) * n)) for i in range(ROW_CHUNKS)]


def _residual_ln(x_ref, mod_ref, gate_idx, branch, g_ref, b_ref):
    x = x_ref[...]
    y = DN_ALPHA * x + mod_ref[:, gate_idx:gate_idx + 1, :] * branch.reshape(x.shape)
    return _normalize(y) * g_ref[...] + b_ref[...]


def _adaln_kernel(c_ref, w_ref, b_ref, o_ref):
    a = jax.nn.silu(c_ref[...]).astype(BF16)
    o_ref[...] = _dot(a, w_ref[...].astype(BF16)) + b_ref[...]


def _adaln(c, w_ada, b_ada):
    n = c.shape[0]
    cols = w_ada.shape[1]
    return pl.pallas_call(
        _adaln_kernel,
        grid=(cols // ADA_COLS,),
        in_specs=[
            pl.BlockSpec((n, D_MODEL), lambda j: (0, 0)),
            pl.BlockSpec((D_MODEL, ADA_COLS), lambda j: (0, j)),
            pl.BlockSpec((1, ADA_COLS), lambda j: (0, j)),
        ],
        out_specs=pl.BlockSpec((n, ADA_COLS), lambda j: (0, j)),
        out_shape=jax.ShapeDtypeStruct((n, cols), F32),
        compiler_params=_params("arbitrary"),
        name="adaln",
    )(c, w_ada, b_ada.reshape(1, cols))


_QK_COL, _VP_COL, _GA_COL, _GB_COL = range(4)


def _proj_kernel(x_ref, mod_ref, w_ref, qkv_ref, k_ref, v_ref, p_ref, gab_ref, u_ref):
    j = pl.program_id(2)
    seqs, rows = x_ref.shape[0], x_ref.shape[1]

    @pl.when(j == 0)
    def _():
        u_ref[...] = _modulated_ln(x_ref, mod_ref, 0, 1).astype(BF16)

    def columns():
        return _dot(u_ref[...], w_ref[...])

    def store_heads(ref, res):
        for s in range(seqs):
            for h in range(N_HEADS):
                ref[s, h] = res[s * rows:(s + 1) * rows, h * HEAD_DIM:(h + 1) * HEAD_DIM]

    def store_bf16(ref, first_col, res):
        ref[:, :, first_col:first_col + res.shape[1]] = res.astype(BF16).reshape(seqs, rows, res.shape[1])

    @pl.when(j == _QK_COL)
    def _():
        res = columns()
        store_bf16(qkv_ref, 0, res)
        store_heads(k_ref, res[:, D_SB:])

    @pl.when(j == _VP_COL)
    def _():
        res = columns()
        store_bf16(qkv_ref, 2 * D_SB, res[:, :D_SB])
        store_heads(v_ref, res[:, :D_SB])
        p_ref[...] = res[:, D_SB:].reshape(p_ref.shape)

    @pl.when(j == _GA_COL)
    def _():
        store_bf16(gab_ref, 0, columns())

    @pl.when(j == _GB_COL)
    def _():
        store_bf16(gab_ref, D_MODEL, columns())


def _proj(x, mod, w_in, seqs, rows):
    B, T, _ = x.shape
    assert 2 * D_SB == D_SB + D_POOL == D_MODEL == PROJ_COLS
    per_seq = T // rows
    last_tile = (B // seqs) * per_seq - 1

    def tile_after(final_step):
        def index(b, t, j):
            tile = jnp.minimum(b * per_seq + t + (j > final_step).astype(jnp.int32), last_tile)
            return tile // per_seq, tile % per_seq
        return index

    def rows_spec(width, final_step=None):
        if final_step is None:
            return pl.BlockSpec((seqs, rows, width), lambda b, t, j: (b, t, 0))
        index = tile_after(final_step)
        return pl.BlockSpec((seqs, rows, width), lambda b, t, j: (*index(b, t, j), 0))

    def heads_spec(final_step):
        index = tile_after(final_step)

        def index_map(b, t, j):
            bb, tt = index(b, t, j)
            return 0, bb, 0, tt, 0
        return pl.BlockSpec((None, seqs, N_HEADS, rows, HEAD_DIM), index_map)

    head_shape = jax.ShapeDtypeStruct((DEPTH, B, N_HEADS, T, HEAD_DIM), F32)
    return pl.pallas_call(
        _proj_kernel,
        grid=(B // seqs, T // rows, D_IN // PROJ_COLS),
        in_specs=[
            rows_spec(D_MODEL),
            pl.BlockSpec((seqs, 6, D_MODEL), lambda b, t, j: (b, 0, 0)),
            pl.BlockSpec((D_MODEL, PROJ_COLS), lambda b, t, j: (0, j)),
        ],
        out_specs=[rows_spec(3 * D_SB, _VP_COL), heads_spec(_QK_COL), heads_spec(_VP_COL), rows_spec(D_POOL, _VP_COL),
                   rows_spec(2 * D_MODEL)],
        out_shape=[
            jax.ShapeDtypeStruct((B, T, 3 * D_SB), BF16),
            head_shape,
            head_shape,
            jax.ShapeDtypeStruct((B, T, D_POOL), F32),
            jax.ShapeDtypeStruct((B, T, 2 * D_MODEL), BF16),
        ],
        scratch_shapes=[pltpu.VMEM((seqs * rows, D_MODEL), BF16)],
        compiler_params=_params("arbitrary", "arbitrary", "arbitrary"),
        name="proj",
    )(x, mod, w_in)


def _suffix_sum_matrix(n):
    r = lax.broadcasted_iota(jnp.int32, (n, n), 0)
    c = lax.broadcasted_iota(jnp.int32, (n, n), 1)
    return jnp.where(r >= c, -1.0, 0.0).astype(BF16)


def _sb_blocks(qs, blocks, sums, carry_ref, acc_ref):
    heads, tq = len(qs), qs[0].shape[0]
    row = lax.broadcasted_iota(jnp.int32, (tq, tq), 0)
    col = lax.broadcasted_iota(jnp.int32, (tq, tq), 1)
    mask = col < row
    zs = [[lax.dot_general(q, k, (((1,), (1,)), ((), ())), preferred_element_type=F32) * SB_SCALE_LOG2
           for q, k in zip(qs, k_blks)] for k_blks, _, _ in blocks]
    terms, first_term = {}, []
    for z_blk, (k_blks, _, diagonal) in zip(zs, blocks):
        of_width = terms.setdefault(k_blks[0].shape[0], [])
        first_term.append(len(of_width))
        for z in z_blk:
            neg_l = jnp.maximum(z, 0.0) + jnp.log2(1.0 + jnp.exp2(-jnp.abs(z)))
            if diagonal:
                neg_l = jnp.where(mask, neg_l, 0.0)
            hi = neg_l.astype(BF16)
            of_width += [hi, (neg_l - hi.astype(F32)).astype(BF16)]
    sums_out = {n: _dot(jnp.concatenate(t, axis=0), sums[n]) for n, t in terms.items()}
    ws = [[] for _ in blocks]
    for h in range(heads):
        carry = carry_ref[h]
        for b, (k_blks, _, diagonal) in enumerate(blocks):
            n = k_blks[0].shape[0]
            at = (first_term[b] + 2 * h) * tq
            incl = sums_out[n][at:at + tq] + sums_out[n][at + tq:at + 2 * tq]
            w = []
            for c in range(0, n, LANES):
                width = min(LANES, n - c)
                w.append(jnp.exp2(zs[b][h][:, c:c + width] + incl[:, c:c + width] + carry[:, :width]))
            w = w[0] if len(w) == 1 else jnp.concatenate(w, axis=1)
            if diagonal:
                w = jnp.where(mask, w, 0.0)
            ws[b].append(w.astype(BF16))
            carry = carry + jnp.broadcast_to(incl[:, :1], carry.shape)
        carry_ref[h] = carry
    pvs = [[_dot(w, v) for w, v in zip(ws[b], v_blks)] for b, (_, v_blks, _) in enumerate(blocks)]
    for h in range(heads):
        acc_ref[h] += functools.reduce(lambda a, b: a + b, [pv[h] for pv in pvs])


def _walk_past(n_blocks, bound, carry_ref, wide_fn, narrow_fn):
    def cond(state):
        j, bound = state
        return (j >= 2) & (bound > UNDERFLOW_LOG2)

    def body(state):
        j, _ = state
        wide_fn(pl.multiple_of((j - 2) * Q_BLOCK, Q_BLOCK))
        return j - 2, jnp.max(carry_ref[...])

    j, bound = lax.while_loop(cond, body, (n_blocks, bound))
    if narrow_fn is not None:
        pl.when((j == 1) & (bound > UNDERFLOW_LOG2))(narrow_fn)


def _head_cols(h):
    return slice(h * HEAD_DIM, (h + 1) * HEAD_DIM)


def _attn_prompt_kernel(q_ref, k_ref, v_ref, o_ref, carry_ref, acc_ref):
    qi = pl.program_id(1)
    carry_ref[...] = jnp.zeros_like(carry_ref)
    acc_ref[...] = jnp.zeros_like(acc_ref)

    hs = [_head_cols(h) for h in range(N_HEADS)]

    def keys(start, n, diagonal):
        rows = pl.ds(pl.multiple_of(start, Q_BLOCK), n)
        return [k_ref[rows, c] for c in hs], [v_ref[rows, c] for c in hs], diagonal

    def visit(blocks, sums):
        _sb_blocks([q_ref[:, c] for c in hs], blocks, sums, carry_ref, acc_ref)

    def walk(n_blocks, bound, sums):
        _walk_past(n_blocks, bound, carry_ref,
                   lambda start: visit([keys(start, PAST_BLOCK, False)], sums),
                   lambda: visit([keys(0, Q_BLOCK, False)], sums))

    eager = EAGER_PAST_BLOCKS * (PAST_BLOCK // Q_BLOCK)

    @pl.when(qi >= eager)
    def _():
        sums = {n: _suffix_sum_matrix(n) for n in (Q_BLOCK, PAST_BLOCK)}
        visit([keys(qi * Q_BLOCK, Q_BLOCK, True)]
              + [keys(qi * Q_BLOCK - b * PAST_BLOCK, PAST_BLOCK, False) for b in range(1, EAGER_PAST_BLOCKS + 1)], sums)
        walk(qi - eager, jnp.max(carry_ref[...]), sums)

    @pl.when(qi < eager)
    def _():
        sums = {n: _suffix_sum_matrix(n) for n in (Q_BLOCK, PAST_BLOCK)}
        visit([keys(qi * Q_BLOCK, Q_BLOCK, True)], sums)
        walk(qi, jnp.float32(0.0), sums)

    for h in range(N_HEADS):
        o_ref[:, _head_cols(h)] = acc_ref[h].astype(o_ref.dtype)


def _attn_prompt(qkv):
    B, T, _ = qkv.shape
    q_spec = pl.BlockSpec((None, Q_BLOCK, D_SB), lambda b, i: (b, i, 0))
    kv_spec = lambda part: pl.BlockSpec((None, T, D_SB), lambda b, i: (b, 0, part))
    return pl.pallas_call(
        _attn_prompt_kernel,
        grid=(B, T // Q_BLOCK),
        in_specs=[q_spec, kv_spec(1), kv_spec(2)],
        out_specs=q_spec,
        out_shape=jax.ShapeDtypeStruct((B, T, D_SB), BF16),
        scratch_shapes=[pltpu.VMEM((N_HEADS, Q_BLOCK, LANES), F32), pltpu.VMEM((N_HEADS, Q_BLOCK, HEAD_DIM), F32)],
        compiler_params=_params("parallel", "arbitrary"),
        name="attn_prompt",
    )(qkv, qkv, qkv)


def _attn_sample_kernel(q_ref, k_ref, v_ref, ck_ref, cv_ref, o_ref, carry_ref, acc_ref):
    tq = q_ref.shape[0]
    carry_ref[...] = jnp.zeros_like(carry_ref)
    acc_ref[...] = jnp.zeros_like(acc_ref)
    hs = [_head_cols(h) for h in range(N_HEADS)]
    qs = [q_ref[:, c] for c in hs]
    sums = {n: _suffix_sum_matrix(n) for n in (tq, PAST_BLOCK)}
    _sb_blocks(qs, [([k_ref[:, c] for c in hs], [v_ref[:, c] for c in hs], True)], sums, carry_ref, acc_ref)

    def visit(start):
        rows = pl.ds(start, PAST_BLOCK)
        cached = ([ck_ref[h, rows, :].astype(BF16) for h in range(N_HEADS)],
                  [cv_ref[h, rows, :].astype(BF16) for h in range(N_HEADS)], False)
        _sb_blocks(qs, [cached], sums, carry_ref, acc_ref)

    past = ck_ref.shape[1]
    assert past % PAST_BLOCK == 0
    _walk_past(past // Q_BLOCK, jnp.float32(0.0), carry_ref, visit, None)
    for h in range(N_HEADS):
        o_ref[:, _head_cols(h)] = acc_ref[h].astype(o_ref.dtype)


def _attn_sample(qkv, cache_k, cache_v):
    B, T, _ = qkv.shape
    past = cache_k.shape[3]
    new_spec = lambda part: pl.BlockSpec((None, T, D_SB), lambda b: (b, 0, part))
    cache_spec = pl.BlockSpec((None, None, N_HEADS, past, HEAD_DIM), lambda b: (0, b, 0, 0, 0))
    return pl.pallas_call(
        _attn_sample_kernel,
        grid=(B,),
        in_specs=[new_spec(0), new_spec(1), new_spec(2), cache_spec, cache_spec],
        out_specs=new_spec(0),
        out_shape=jax.ShapeDtypeStruct((B, T, D_SB), BF16),
        scratch_shapes=[pltpu.VMEM((N_HEADS, T, LANES), F32), pltpu.VMEM((N_HEADS, T, HEAD_DIM), F32)],
        compiler_params=_params("parallel"),
        name="attn_sample",
    )(qkv, qkv, qkv, cache_k, cache_v)


def _gate_kernel(o_ref, p_ref, halo_ref, ga_ref, gb_ref, wsb_ref, wpool_ref, pscale_ref, m_ref,
                 *, pos0, zero_first_halo):
    seqs, rows = p_ref.shape[0], p_ref.shape[1]
    t0 = pl.program_id(1) * rows
    y_a = _dot(o_ref[...].reshape(seqs * rows, D_SB), wsb_ref[...])

    halo = halo_ref[...]
    if zero_first_halo:
        halo = jnp.where(t0 == 0, 0.0, halo)
    pos = pos0 + t0 + lax.broadcasted_iota(jnp.int32, (rows, LANES), 0)

    y_b = []
    for g, win in enumerate(POOL_WINDOWS):
        cols = slice(g * POOL_GROUP, (g + 1) * POOL_GROUP)
        inv_cnt = 1.0 / jnp.minimum(win, pos + 1).astype(F32)
        inv_cnt = jnp.concatenate([inv_cnt] * (POOL_GROUP // LANES), axis=1)
        diffs = []
        for s in range(seqs):
            p = p_ref[s, :, cols]
            acc = jnp.concatenate([halo[s, :, cols], p], axis=0)
            shift = 1
            while shift < win:
                acc = acc + pltpu.roll(acc, shift, 0)
                shift *= 2
            diffs.append(acc[POOL_HALO:, :] * inv_cnt - p)
        d = jnp.concatenate(diffs, axis=0).astype(BF16)
        y_b.append(_dot(d, wpool_ref[g]))
    y_b = jnp.concatenate(y_b, axis=1) * pscale_ref[...]

    gate_a = jax.nn.sigmoid(ga_ref[...].astype(F32)).reshape(seqs * rows, D_MODEL)
    gate_b = jax.nn.sigmoid(gb_ref[...].astype(F32)).reshape(seqs * rows, D_MODEL)
    m_ref[...] = (gate_a * y_a + gate_b * y_b).astype(BF16).reshape(m_ref.shape)


def _gate(o, p, halo_src, gab, w_sb_out, w_pool, pool_scale, seqs, rows, pos0, halo_from_p):
    B, T, _ = p.shape
    row_block = lambda width, part=0: pl.BlockSpec((seqs, rows, width), lambda b, t: (b, t, part))
    if halo_from_p:
        per_tile = rows // POOL_HALO
        halo_spec = pl.BlockSpec((seqs, POOL_HALO, D_POOL),
                                 lambda b, t: (b, jnp.maximum(t * per_tile - 1, 0), 0))
    else:
        halo_spec = pl.BlockSpec((seqs, POOL_HALO, D_POOL), lambda b, t: (b, 0, 0))
    const = lambda shape: pl.BlockSpec(shape, lambda b, t: (0,) * len(shape))
    return pl.pallas_call(
        functools.partial(_gate_kernel, pos0=pos0, zero_first_halo=halo_from_p),
        grid=(B // seqs, T // rows),
        in_specs=[
            row_block(D_SB), row_block(D_POOL), halo_spec, row_block(D_MODEL, 0), row_block(D_MODEL, 1),
            const((D_SB, D_MODEL)), const((N_POOL_GROUPS, POOL_GROUP, POOL_OUT)), const((1, D_MODEL)),
        ],
        out_specs=row_block(D_MODEL),
        out_shape=jax.ShapeDtypeStruct((B, T, D_MODEL), BF16),
        compiler_params=_params("parallel", "arbitrary"),
        name="gate",
    )(o, p, halo_src, gab, gab, w_sb_out, w_pool, pool_scale)


def _mix_kernel(m_ref, x_ref, mod_ref, w_ref, g_ref, b_ref, o_ref):
    seqs, rows = x_ref.shape[0], x_ref.shape[1]
    for s, r in _row_chunks(seqs, rows):
        mix = _dot(m_ref[s, r, :].reshape(-1, D_MODEL), w_ref[...])
        o_ref[s, r, :] = _residual_ln(x_ref.at[s, r, :], mod_ref.at[s], 2, mix, g_ref, b_ref)


def _mix(m, x, mod, w_out, ln_g, ln_b, seqs, rows):
    B, T, _ = x.shape
    row_block = pl.BlockSpec((seqs, rows, D_MODEL), lambda b, t: (b, t, 0))
    vec = pl.BlockSpec((1, D_MODEL), lambda b, t: (0, 0))
    return pl.pallas_call(
        _mix_kernel,
        grid=(B // seqs, T // rows),
        in_specs=[
            row_block, row_block,
            pl.BlockSpec((seqs, 6, D_MODEL), lambda b, t: (b, 0, 0)),
            pl.BlockSpec((D_MODEL, D_MODEL), lambda b, t: (0, 0)),
            vec, vec,
        ],
        out_specs=row_block,
        out_shape=jax.ShapeDtypeStruct((B, T, D_MODEL), F32),
        compiler_params=_params("parallel", "arbitrary"),
        name="mix",
    )(m, x, mod, w_out, ln_g, ln_b)


def _ffn_kernel(x_ref, mod_ref, wg_ref, wu_ref, wd_ref, g_ref, b_ref, o_ref, u_ref, acc_ref):
    j = pl.program_id(2)
    last = pl.num_programs(2) - 1

    def chunk(u):
        h = jax.nn.silu(_dot(u, wg_ref[...])) * _dot(u, wu_ref[...])
        return _dot(h.astype(BF16), wd_ref[...])

    @pl.when(j == 0)
    def _():
        u = _modulated_ln(x_ref, mod_ref, 3, 4).astype(BF16)
        u_ref[...] = u
        acc_ref[...] = chunk(u)

    @pl.when((j > 0) & (j < last))
    def _():
        acc_ref[...] += chunk(u_ref[...])

    @pl.when(j == last)
    def _():
        o_ref[...] = _residual_ln(x_ref, mod_ref, 5, acc_ref[...] + chunk(u_ref[...]), g_ref, b_ref)


def _ffn(x, mod, w_gate, w_up, w_down, ln_g, ln_b, seqs, rows):
    B, T, _ = x.shape
    d_ff = w_gate.shape[1]
    row_block = pl.BlockSpec((seqs, rows, D_MODEL), lambda b, t, j: (b, t, 0))
    vec = pl.BlockSpec((1, D_MODEL), lambda b, t, j: (0, 0))
    return pl.pallas_call(
        _ffn_kernel,
        grid=(B // seqs, T // rows, d_ff // FF_COLS),
        in_specs=[
            row_block,
            pl.BlockSpec((seqs, 6, D_MODEL), lambda b, t, j: (b, 0, 0)),
            pl.BlockSpec((D_MODEL, FF_COLS), lambda b, t, j: (0, j)),
            pl.BlockSpec((D_MODEL, FF_COLS), lambda b, t, j: (0, j)),
            pl.BlockSpec((FF_COLS, D_MODEL), lambda b, t, j: (j, 0)),
            vec, vec,
        ],
        out_specs=row_block,
        out_shape=jax.ShapeDtypeStruct((B, T, D_MODEL), F32),
        scratch_shapes=[pltpu.VMEM((seqs * rows, D_MODEL), BF16), pltpu.VMEM((seqs * rows, D_MODEL), F32)],
        compiler_params=_params("parallel", "parallel", "arbitrary"),
        name="ffn",
    )(x, mod, w_gate, w_up, w_down, ln_g, ln_b)


def _tile(batch, seq, row_tile=ROW_TILE):
    if seq >= row_tile:
        return 1, row_tile
    return min(batch, row_tile // seq), seq


def _layer(x, mod, attn_fn, halo_src, pos0, halo_from_p, w):
    B, T, _ = x.shape
    seqs, rows = _tile(B, T)
    qkv, k, v, p, gab = _proj(x, mod, w["w_in"], seqs, rows)
    o = attn_fn(qkv)
    m = _gate(o, p, p if halo_from_p else halo_src, gab, w["w_sb_out"], w["w_pool"], w["pool_scale"],
              seqs, rows, pos0, halo_from_p)
    x = _mix(m, x, mod, w["w_out"], w["ln1_g"], w["ln1_b"], seqs, rows)
    x = _ffn(x, mod, w["w_gate"], w["w_up"], w["w_down"], w["ln2_g"], w["ln2_b"], seqs, rows)
    return x, k, v, p[None, :, T - POOL_PAST:, :]


def kernel(x_prompt, x_sample, c_prompt, c_sample, cache_k, cache_v, state_pool, w_ada, b_ada, w_in, w_sb_out, w_pool, pool_scale, w_out, ln1_g, ln1_b, w_gate, w_up, w_down, ln2_g, ln2_b):
    assert w_in.shape[0] == DEPTH
    n_prompt = c_prompt.shape[0]
    mod = _adaln(jnp.concatenate([c_prompt, c_sample], axis=0), w_ada[0], b_ada[0])
    mod = mod.reshape(mod.shape[0], 6, D_MODEL)
    w = {
        "w_in": w_in[0].astype(BF16), "w_sb_out": w_sb_out[0].astype(BF16), "w_pool": w_pool[0].astype(BF16),
        "w_out": w_out[0].astype(BF16), "w_gate": w_gate[0].astype(BF16), "w_up": w_up[0].astype(BF16),
        "w_down": w_down[0].astype(BF16), "pool_scale": pool_scale, "ln1_g": ln1_g, "ln1_b": ln1_b,
        "ln2_g": ln2_g, "ln2_b": ln2_b,
    }
    y_p, k_p, v_p, pool_p = _layer(x_prompt, mod[:n_prompt], _attn_prompt, None, 0, True, w)

    pool_halo = jnp.pad(state_pool[0], ((0, 0), (POOL_HALO - POOL_PAST, 0), (0, 0)))
    attn_sample = lambda qkv: _attn_sample(qkv, cache_k, cache_v)
    y_s, k_s, v_s, pool_s = _layer(x_sample, mod[n_prompt:], attn_sample, pool_halo, cache_k.shape[3], False, w)
    return y_p, y_s, k_p, v_p, pool_p, k_s, v_s, pool_s
```

```python
import functools
import math

import jax
import jax.numpy as jnp
from jax import lax
from jax.experimental import pallas as pl
from jax.experimental.pallas import tpu as pltpu

D_MODEL = 2048
N_HEADS = 8
HEAD_DIM = 128
D_SB = N_HEADS * HEAD_DIM
POOL_WINDOWS = (2, 4, 8, 16)
N_POOL_GROUPS = len(POOL_WINDOWS)
D_POOL = 1024
POOL_GROUP = D_POOL // N_POOL_GROUPS
POOL_OUT = D_MODEL // N_POOL_GROUPS
POOL_PAST = max(POOL_WINDOWS) - 1
POOL_HALO = POOL_PAST + 1
D_IN = 3 * D_SB + D_POOL + 2 * D_MODEL
DEPTH = 1
DN_ALPHA = (2 * DEPTH) ** 0.25
LN_EPS = 1e-5

LOG2_E = math.log2(math.e)
SB_SCALE_LOG2 = HEAD_DIM ** -0.5 * LOG2_E
UNDERFLOW_LOG2 = -110.0 * LOG2_E

V7X_VMEM_BYTES = 64 * 1024 * 1024
VMEM_LIMIT = V7X_VMEM_BYTES - 6 * 1024 * 1024
LANES = 128

ROW_TILE = 512
ROW_CHUNKS = 2
PROJ_COLS = 2048
ADA_COLS = 1024
FF_COLS = 512
Q_BLOCK = 128
PAST_BLOCK = 2 * Q_BLOCK
EAGER_PAST_BLOCKS = 2

BF16 = jnp.bfloat16
F32 = jnp.float32


def _params(*sem):
    return pltpu.CompilerParams(dimension_semantics=sem, vmem_limit_bytes=VMEM_LIMIT)


def _dot(a, b):
    return jnp.dot(a, b, preferred_element_type=F32)


def _normalize(x):
    mu = jnp.mean(x, axis=-1, keepdims=True)
    xc = x - mu
    var = jnp.mean(xc * xc, axis=-1, keepdims=True)
    return xc * lax.rsqrt(var + LN_EPS)


def _modulated_ln(x_ref, mod_ref, shift_idx, scale_idx):
    x = x_ref[...]
    u = _normalize(x) * (1.0 + mod_ref[:, scale_idx:scale_idx + 1, :]) + mod_ref[:, shift_idx:shift_idx + 1, :]
    return u.reshape(x.shape[0] * x.shape[1], x.shape[2])


def _row_chunks(seqs, rows):
    if seqs >= ROW_CHUNKS:
        n = seqs // ROW_CHUNKS
        return [(slice(i * n, (i + 1) * n), slice(0, rows)) for i in range(ROW_CHUNKS)]
    n = rows // ROW_CHUNKS
    return [(slice(0, seqs), slice(i * n, (i + 1) * n)) for i in range(ROW_CHUNKS)]


def _residual_ln(x_ref, mod_ref, gate_idx, branch, g_ref, b_ref):
    x = x_ref[...]
    y = DN_ALPHA * x + mod_ref[:, gate_idx:gate_idx + 1, :] * branch.reshape(x.shape)
    return _normalize(y) * g_ref[...] + b_ref[...]


def _adaln_kernel(c_ref, w_ref, b_ref, o_ref):
    a = jax.nn.silu(c_ref[...]).astype(BF16)
    o_ref[...] = _dot(a, w_ref[...].astype(BF16)) + b_ref[...]


def _adaln(c, w_ada, b_ada):
    n = c.shape[0]
    cols = w_ada.shape[1]
    return pl.pallas_call(
        _adaln_kernel,
        grid=(cols // ADA_COLS,),
        in_specs=[
            pl.BlockSpec((n, D_MODEL), lambda j: (0, 0)),
            pl.BlockSpec((D_MODEL, ADA_COLS), lambda j: (0, j)),
            pl.BlockSpec((1, ADA_COLS), lambda j: (0, j)),
        ],
        out_specs=pl.BlockSpec((n, ADA_COLS), lambda j: (0, j)),
        out_shape=jax.ShapeDtypeStruct((n, cols), F32),
        compiler_params=_params("arbitrary"),
        name="adaln",
    )(c, w_ada, b_ada.reshape(1, cols))


_QK_COL, _VP_COL, _GA_COL, _GB_COL = range(4)


def _proj_kernel(x_ref, mod_ref, w_ref, qkv_ref, k_ref, v_ref, p_ref, gab_ref, u_ref):
    j = pl.program_id(2)
    seqs, rows = x_ref.shape[0], x_ref.shape[1]

    @pl.when(j == 0)
    def _():
        u_ref[...] = _modulated_ln(x_ref, mod_ref, 0, 1).astype(BF16)

    def columns():
        return _dot(u_ref[...], w_ref[...])

    def store_heads(ref, res):
        for s in range(seqs):
            for h in range(N_HEADS):
                ref[s, h] = res[s * rows:(s + 1) * rows, h * HEAD_DIM:(h + 1) * HEAD_DIM]

    def store_bf16(ref, first_col, res):
        ref[:, :, first_col:first_col + res.shape[1]] = res.astype(BF16).reshape(seqs, rows, res.shape[1])

    @pl.when(j == _QK_COL)
    def _():
        res = columns()
        store_bf16(qkv_ref, 0, res)
        store_heads(k_ref, res[:, D_SB:])

    @pl.when(j == _VP_COL)
    def _():
        res = columns()
        store_bf16(qkv_ref, 2 * D_SB, res[:, :D_SB])
        store_heads(v_ref, res[:, :D_SB])
        p_ref[...] = res[:, D_SB:].reshape(p_ref.shape)

    @pl.when(j == _GA_COL)
    def _():
        store_bf16(gab_ref, 0, columns())

    @pl.when(j == _GB_COL)
    def _():
        store_bf16(gab_ref, D_MODEL, columns())


def _proj(x, mod, w_in, seqs, rows):
    B, T, _ = x.shape
    assert 2 * D_SB == D_SB + D_POOL == D_MODEL == PROJ_COLS
    per_seq = T // rows
    last_tile = (B // seqs) * per_seq - 1

    def tile_after(final_step):
        def index(b, t, j):
            tile = jnp.minimum(b * per_seq + t + (j > final_step).astype(jnp.int32), last_tile)
            return tile // per_seq, tile % per_seq
        return index

    def rows_spec(width, final_step=None):
        if final_step is None:
            return pl.BlockSpec((seqs, rows, width), lambda b, t, j: (b, t, 0))
        index = tile_after(final_step)
        return pl.BlockSpec((seqs, rows, width), lambda b, t, j: (*index(b, t, j), 0))

    def heads_spec(final_step):
        index = tile_after(final_step)

        def index_map(b, t, j):
            bb, tt = index(b, t, j)
            return 0, bb, 0, tt, 0
        return pl.BlockSpec((None, seqs, N_HEADS, rows, HEAD_DIM), index_map)

    head_shape = jax.ShapeDtypeStruct((DEPTH, B, N_HEADS, T, HEAD_DIM), F32)
    return pl.pallas_call(
        _proj_kernel,
        grid=(B // seqs, T // rows, D_IN // PROJ_COLS),
        in_specs=[
            rows_spec(D_MODEL),
            pl.BlockSpec((seqs, 6, D_MODEL), lambda b, t, j: (b, 0, 0)),
            pl.BlockSpec((D_MODEL, PROJ_COLS), lambda b, t, j: (0, j)),
        ],
        out_specs=[rows_spec(3 * D_SB, _VP_COL), heads_spec(_QK_COL), heads_spec(_VP_COL), rows_spec(D_POOL, _VP_COL),
                   rows_spec(2 * D_MODEL)],
        out_shape=[
            jax.ShapeDtypeStruct((B, T, 3 * D_SB), BF16),
            head_shape,
            head_shape,
            jax.ShapeDtypeStruct((B, T, D_POOL), F32),
            jax.ShapeDtypeStruct((B, T, 2 * D_MODEL), BF16),
        ],
        scratch_shapes=[pltpu.VMEM((seqs * rows, D_MODEL), BF16)],
        compiler_params=_params("arbitrary", "arbitrary", "arbitrary"),
        name="proj",
    )(x, mod, w_in)


def _suffix_sum_matrix(n):
    r = lax.broadcasted_iota(jnp.int32, (n, n), 0)
    c = lax.broadcasted_iota(jnp.int32, (n, n), 1)
    return jnp.where(r >= c, -1.0, 0.0).astype(BF16)


def _sb_blocks(qs, blocks, sums, carry_ref, acc_ref):
    heads, tq = len(qs), qs[0].shape[0]
    row = lax.broadcasted_iota(jnp.int32, (tq, tq), 0)
    col = lax.broadcasted_iota(jnp.int32, (tq, tq), 1)
    mask = col < row
    zs = [[lax.dot_general(q, k, (((1,), (1,)), ((), ())), preferred_element_type=F32) * SB_SCALE_LOG2
           for q, k in zip(qs, k_blks)] for k_blks, _, _ in blocks]
    terms, first_term = {}, []
    for z_blk, (k_blks, _, diagonal) in zip(zs, blocks):
        of_width = terms.setdefault(k_blks[0].shape[0], [])
        first_term.append(len(of_width))
        for z in z_blk:
            neg_l = jnp.maximum(z, 0.0) + jnp.log2(1.0 + jnp.exp2(-jnp.abs(z)))
            if diagonal:
                neg_l = jnp.where(mask, neg_l, 0.0)
            hi = neg_l.astype(BF16)
            of_width += [hi, (neg_l - hi.astype(F32)).astype(BF16)]
    sums_out = {n: _dot(jnp.concatenate(t, axis=0), sums[n]) for n, t in terms.items()}
    ws = [[] for _ in blocks]
    for h in range(heads):
        carry = carry_ref[h]
        for b, (k_blks, _, diagonal) in enumerate(blocks):
            n = k_blks[0].shape[0]
            at = (first_term[b] + 2 * h) * tq
            incl = sums_out[n][at:at + tq] + sums_out[n][at + tq:at + 2 * tq]
            w = []
            for c in range(0, n, LANES):
                width = min(LANES, n - c)
                w.append(jnp.exp2(zs[b][h][:, c:c + width] + incl[:, c:c + width] + carry[:, :width]))
            w = w[0] if len(w) == 1 else jnp.concatenate(w, axis=1)
            if diagonal:
                w = jnp.where(mask, w, 0.0)
            ws[b].append(w.astype(BF16))
            carry = carry + jnp.broadcast_to(incl[:, :1], carry.shape)
        carry_ref[h] = carry
    pvs = [[_dot(w, v) for w, v in zip(ws[b], v_blks)] for b, (_, v_blks, _) in enumerate(blocks)]
    for h in range(heads):
        acc_ref[h] += functools.reduce(lambda a, b: a + b, [pv[h] for pv in pvs])


def _walk_past(n_blocks, bound, carry_ref, wide_fn, narrow_fn):
    def cond(state):
        j, bound = state
        return (j >= 2) & (bound > UNDERFLOW_LOG2)

    def body(state):
        j, _ = state
        wide_fn(pl.multiple_of((j - 2) * Q_BLOCK, Q_BLOCK))
        return j - 2, jnp.max(carry_ref[...])

    j, bound = lax.while_loop(cond, body, (n_blocks, bound))
    if narrow_fn is not None:
        pl.when((j == 1) & (bound > UNDERFLOW_LOG2))(narrow_fn)


def _head_cols(h):
    return slice(h * HEAD_DIM, (h + 1) * HEAD_DIM)


def _attn_prompt_kernel(q_ref, k_ref, v_ref, o_ref, carry_ref, acc_ref):
    qi = pl.program_id(1)
    carry_ref[...] = jnp.zeros_like(carry_ref)
    acc_ref[...] = jnp.zeros_like(acc_ref)

    hs = [_head_cols(h) for h in range(N_HEADS)]

    def keys(start, n, diagonal):
        rows = pl.ds(pl.multiple_of(start, Q_BLOCK), n)
        return [k_ref[rows, c] for c in hs], [v_ref[rows, c] for c in hs], diagonal

    def visit(blocks, sums):
        _sb_blocks([q_ref[:, c] for c in hs], blocks, sums, carry_ref, acc_ref)

    def walk(n_blocks, bound, sums):
        _walk_past(n_blocks, bound, carry_ref,
                   lambda start: visit([keys(start, PAST_BLOCK, False)], sums),
                   lambda: visit([keys(0, Q_BLOCK, False)], sums))

    eager = EAGER_PAST_BLOCKS * (PAST_BLOCK // Q_BLOCK)

    @pl.when(qi >= eager)
    def _():
        sums = {n: _suffix_sum_matrix(n) for n in (Q_BLOCK, PAST_BLOCK)}
        visit([keys(qi * Q_BLOCK, Q_BLOCK, True)]
              + [keys(qi * Q_BLOCK - b * PAST_BLOCK, PAST_BLOCK, False) for b in range(1, EAGER_PAST_BLOCKS + 1)], sums)
        walk(qi - eager, jnp.max(carry_ref[...]), sums)

    @pl.when(qi < eager)
    def _():
        sums = {n: _suffix_sum_matrix(n) for n in (Q_BLOCK, PAST_BLOCK)}
        visit([keys(qi * Q_BLOCK, Q_BLOCK, True)], sums)
        walk(qi, jnp.float32(0.0), sums)

    for h in range(N_HEADS):
        o_ref[:, _head_cols(h)] = acc_ref[h].astype(o_ref.dtype)


def _attn_prompt(qkv):
    B, T, _ = qkv.shape
    q_spec = pl.BlockSpec((None, Q_BLOCK, D_SB), lambda b, i: (b, i, 0))
    kv_spec = lambda part: pl.BlockSpec((None, T, D_SB), lambda b, i: (b, 0, part))
    return pl.pallas_call(
        _attn_prompt_kernel,
        grid=(B, T // Q_BLOCK),
        in_specs=[q_spec, kv_spec(1), kv_spec(2)],
        out_specs=q_spec,
        out_shape=jax.ShapeDtypeStruct((B, T, D_SB), BF16),
        scratch_shapes=[pltpu.VMEM((N_HEADS, Q_BLOCK, LANES), F32), pltpu.VMEM((N_HEADS, Q_BLOCK, HEAD_DIM), F32)],
        compiler_params=_params("parallel", "arbitrary"),
        name="attn_prompt",
    )(qkv, qkv, qkv)


def _attn_sample_kernel(q_ref, k_ref, v_ref, ck_ref, cv_ref, o_ref, carry_ref, acc_ref):
    tq = q_ref.shape[0]
    carry_ref[...] = jnp.zeros_like(carry_ref)
    acc_ref[...] = jnp.zeros_like(acc_ref)
    hs = [_head_cols(h) for h in range(N_HEADS)]
    qs = [q_ref[:, c] for c in hs]
    sums = {n: _suffix_sum_matrix(n) for n in (tq, PAST_BLOCK)}
    _sb_blocks(qs, [([k_ref[:, c] for c in hs], [v_ref[:, c] for c in hs], True)], sums, carry_ref, acc_ref)

    def visit(start):
        rows = pl.ds(start, PAST_BLOCK)
        cached = ([ck_ref[h, rows, :].astype(BF16) for h in range(N_HEADS)],
                  [cv_ref[h, rows, :].astype(BF16) for h in range(N_HEADS)], False)
        _sb_blocks(qs, [cached], sums, carry_ref, acc_ref)

    past = ck_ref.shape[1]
    assert past % PAST_BLOCK == 0
    _walk_past(past // Q_BLOCK, jnp.float32(0.0), carry_ref, visit, None)
    for h in range(N_HEADS):
        o_ref[:, _head_cols(h)] = acc_ref[h].astype(o_ref.dtype)


def _attn_sample(qkv, cache_k, cache_v):
    B, T, _ = qkv.shape
    past = cache_k.shape[3]
    new_spec = lambda part: pl.BlockSpec((None, T, D_SB), lambda b: (b, 0, part))
    cache_spec = pl.BlockSpec((None, None, N_HEADS, past, HEAD_DIM), lambda b: (0, b, 0, 0, 0))
    return pl.pallas_call(
        _attn_sample_kernel,
        grid=(B,),
        in_specs=[new_spec(0), new_spec(1), new_spec(2), cache_spec, cache_spec],
        out_specs=new_spec(0),
        out_shape=jax.ShapeDtypeStruct((B, T, D_SB), BF16),
        scratch_shapes=[pltpu.VMEM((N_HEADS, T, LANES), F32), pltpu.VMEM((N_HEADS, T, HEAD_DIM), F32)],
        compiler_params=_params("parallel"),
        name="attn_sample",
    )(qkv, qkv, qkv, cache_k, cache_v)


def _gate_kernel(o_ref, p_ref, halo_ref, ga_ref, gb_ref, wsb_ref, wpool_ref, pscale_ref, m_ref,
                 *, pos0, zero_first_halo):
    seqs, rows = p_ref.shape[0], p_ref.shape[1]
    t0 = pl.program_id(1) * rows
    halo = halo_ref[...]
    if zero_first_halo:
        halo = jnp.where(t0 == 0, 0.0, halo)
    pos = pos0 + t0 + lax.broadcasted_iota(jnp.int32, (rows, LANES), 0)

    y_b = []
    for g, win in enumerate(POOL_WINDOWS):
        cols = slice(g * POOL_GROUP, (g + 1) * POOL_GROUP)
        inv_cnt = 1.0 / jnp.minimum(win, pos + 1).astype(F32)
        inv_cnt = jnp.concatenate([inv_cnt] * (POOL_GROUP // LANES), axis=1)
        diffs = []
        for s in range(seqs):
            p = p_ref[s, :, cols]
            acc = jnp.concatenate([halo[s, :, cols], p], axis=0)
            shift = 1
            while shift < win:
                acc = acc + pltpu.roll(acc, shift, 0)
                shift *= 2
            diffs.append(acc[POOL_HALO:, :] * inv_cnt - p)
        d = jnp.concatenate(diffs, axis=0).astype(BF16)
        y_b.append(_dot(d, wpool_ref[g]))
    y_b = jnp.concatenate(y_b, axis=1) * pscale_ref[...]

    gate_b = jax.nn.sigmoid(gb_ref[...].astype(F32)).reshape(seqs * rows, D_MODEL)
    gated_b = gate_b * y_b
    gate_a = jax.nn.sigmoid(ga_ref[...].astype(F32)).reshape(seqs * rows, D_MODEL)
    y_a = _dot(o_ref[...].reshape(seqs * rows, D_SB), wsb_ref[...])
    m_ref[...] = (gate_a * y_a + gated_b).astype(BF16).reshape(m_ref.shape)


def _gate(o, p, halo_src, gab, w_sb_out, w_pool, pool_scale, seqs, rows, pos0, halo_from_p):
    B, T, _ = p.shape
    row_block = lambda width, part=0: pl.BlockSpec((seqs, rows, width), lambda b, t: (b, t, part))
    if halo_from_p:
        per_tile = rows // POOL_HALO
        halo_spec = pl.BlockSpec((seqs, POOL_HALO, D_POOL),
                                 lambda b, t: (b, jnp.maximum(t * per_tile - 1, 0), 0))
    else:
        halo_spec = pl.BlockSpec((seqs, POOL_HALO, D_POOL), lambda b, t: (b, 0, 0))
    const = lambda shape: pl.BlockSpec(shape, lambda b, t: (0,) * len(shape))
    return pl.pallas_call(
        functools.partial(_gate_kernel, pos0=pos0, zero_first_halo=halo_from_p),
        grid=(B // seqs, T // rows),
        in_specs=[
            row_block(D_SB), row_block(D_POOL), halo_spec, row_block(D_MODEL, 0), row_block(D_MODEL, 1),
            const((D_SB, D_MODEL)), const((N_POOL_GROUPS, POOL_GROUP, POOL_OUT)), const((1, D_MODEL)),
        ],
        out_specs=row_block(D_MODEL),
        out_shape=jax.ShapeDtypeStruct((B, T, D_MODEL), BF16),
        compiler_params=_params("parallel", "arbitrary"),
        name="gate",
    )(o, p, halo_src, gab, gab, w_sb_out, w_pool, pool_scale)


def _mix_kernel(m_ref, x_ref, mod_ref, w_ref, g_ref, b_ref, o_ref):
    seqs, rows = x_ref.shape[0], x_ref.shape[1]
    for s, r in _row_chunks(seqs, rows):
        mix = _dot(m_ref[s, r, :].reshape(-1, D_MODEL), w_ref[...])
        o_ref[s, r, :] = _residual_ln(x_ref.at[s, r, :], mod_ref.at[s], 2, mix, g_ref, b_ref)


def _mix(m, x, mod, w_out, ln_g, ln_b, seqs, rows):
    B, T, _ = x.shape
    row_block = pl.BlockSpec((seqs, rows, D_MODEL), lambda b, t: (b, t, 0))
    vec = pl.BlockSpec((1, D_MODEL), lambda b, t: (0, 0))
    return pl.pallas_call(
        _mix_kernel,
        grid=(B // seqs, T // rows),
        in_specs=[
            row_block, row_block,
            pl.BlockSpec((seqs, 6, D_MODEL), lambda b, t: (b, 0, 0)),
            pl.BlockSpec((D_MODEL, D_MODEL), lambda b, t: (0, 0)),
            vec, vec,
        ],
        out_specs=row_block,
        out_shape=jax.ShapeDtypeStruct((B, T, D_MODEL), F32),
        compiler_params=_params("parallel", "arbitrary"),
        name="mix",
    )(m, x, mod, w_out, ln_g, ln_b)


def _ffn_kernel(x_ref, mod_ref, wg_ref, wu_ref, wd_ref, g_ref, b_ref, o_ref, u_ref, acc_ref):
    j = pl.program_id(2)
    last = pl.num_programs(2) - 1

    def chunk(u):
        h = jax.nn.silu(_dot(u, wg_ref[...])) * _dot(u, wu_ref[...])
        return _dot(h.astype(BF16), wd_ref[...])

    @pl.when(j == 0)
    def _():
        u = _modulated_ln(x_ref, mod_ref, 3, 4).astype(BF16)
        u_ref[...] = u
        acc_ref[...] = chunk(u)

    @pl.when((j > 0) & (j < last))
    def _():
        acc_ref[...] += chunk(u_ref[...])

    @pl.when(j == last)
    def _():
        o_ref[...] = _residual_ln(x_ref, mod_ref, 5, acc_ref[...] + chunk(u_ref[...]), g_ref, b_ref)


def _ffn(x, mod, w_gate, w_up, w_down, ln_g, ln_b, seqs, rows):
    B, T, _ = x.shape
    d_ff = w_gate.shape[1]
    row_block = pl.BlockSpec((seqs, rows, D_MODEL), lambda b, t, j: (b, t, 0))
    vec = pl.BlockSpec((1, D_MODEL), lambda b, t, j: (0, 0))
    return pl.pallas_call(
        _ffn_kernel,
        grid=(B // seqs, T // rows, d_ff // FF_COLS),
        in_specs=[
            row_block,
            pl.BlockSpec((seqs, 6, D_MODEL), lambda b, t, j: (b, 0, 0)),
            pl.BlockSpec((D_MODEL, FF_COLS), lambda b, t, j: (0, j)),
            pl.BlockSpec((D_MODEL, FF_COLS), lambda b, t, j: (0, j)),
            pl.BlockSpec((FF_COLS, D_MODEL), lambda b, t, j: (j, 0)),
            vec, vec,
        ],
        out_specs=row_block,
        out_shape=jax.ShapeDtypeStruct((B, T, D_MODEL), F32),
        scratch_shapes=[pltpu.VMEM((seqs * rows, D_MODEL), BF16), pltpu.VMEM((seqs * rows, D_MODEL), F32)],
        compiler_params=_params("parallel", "parallel", "arbitrary"),
        name="ffn",
    )(x, mod, w_gate, w_up, w_down, ln_g, ln_b)


def _tile(batch, seq, row_tile=ROW_TILE):
    if seq >= row_tile:
        return 1, row_tile
    return min(batch, row_tile // seq), seq


def _layer(x, mod, attn_fn, halo_src, pos0, halo_from_p, w):
    B, T, _ = x.shape
    seqs, rows = _tile(B, T)
    qkv, k, v, p, gab = _proj(x, mod, w["w_in"], seqs, rows)
    o = attn_fn(qkv)
    m = _gate(o, p, p if halo_from_p else halo_src, gab, w["w_sb_out"], w["w_pool"], w["pool_scale"],
              seqs, rows, pos0, halo_from_p)
    x = _mix(m, x, mod, w["w_out"], w["ln1_g"], w["ln1_b"], seqs, rows)
    x = _ffn(x, mod, w["w_gate"], w["w_up"], w["w_down"], w["ln2_g"], w["ln2_b"], seqs, rows)
    return x, k, v, p[None, :, T - POOL_PAST:, :]


def kernel(x_prompt, x_sample, c_prompt, c_sample, cache_k, cache_v, state_pool, w_ada, b_ada, w_in, w_sb_out, w_pool, pool_scale, w_out, ln1_g, ln1_b, w_gate, w_up, w_down, ln2_g, ln2_b):
    assert w_in.shape[0] == DEPTH
    n_prompt = c_prompt.shape[0]
    mod = _adaln(jnp.concatenate([c_prompt, c_sample], axis=0), w_ada[0], b_ada[0])
    mod = mod.reshape(mod.shape[0], 6, D_MODEL)
    w = {
        "w_in": w_in[0].astype(BF16), "w_sb_out": w_sb_out[0].astype(BF16), "w_pool": w_pool[0].astype(BF16),
        "w_out": w_out[0].astype(BF16), "w_gate": w_gate[0].astype(BF16), "w_up": w_up[0].astype(BF16),
        "w_down": w_down[0].astype(BF16), "pool_scale": pool_scale, "ln1_g": ln1_g, "ln1_b": ln1_b,
        "ln2_g": ln2_g, "ln2_b": ln2_b,
    }
    y_p, k_p, v_p, pool_p = _layer(x_prompt, mod[:n_prompt], _attn_prompt, None, 0, True, w)

    pool_halo = jnp.pad(state_pool[0], ((0, 0), (POOL_HALO - POOL_PAST, 0), (0, 0)))
    attn_sample = lambda qkv: _attn_sample(qkv, cache_k, cache_v)
    y_s, k_s, v_s, pool_s = _layer(x_sample, mod[n_prompt:], attn_sample, pool_halo, cache_k.shape[3], False, w)
    return y_p, y_s, k_p, v_p, pool_p, k_s, v_s, pool_s
```

```python
import functools
import math

import jax
import jax.numpy as jnp
from jax import lax
from jax.experimental import pallas as pl
from jax.experimental.pallas import tpu as pltpu

D_MODEL = 2048
N_HEADS = 8
HEAD_DIM = 128
D_SB = N_HEADS * HEAD_DIM
POOL_WINDOWS = (2, 4, 8, 16)
N_POOL_GROUPS = len(POOL_WINDOWS)
D_POOL = 1024
POOL_GROUP = D_POOL // N_POOL_GROUPS
POOL_OUT = D_MODEL // N_POOL_GROUPS
POOL_PAST = max(POOL_WINDOWS) - 1
POOL_HALO = POOL_PAST + 1
D_IN = 3 * D_SB + D_POOL + 2 * D_MODEL
DEPTH = 1
DN_ALPHA = (2 * DEPTH) ** 0.25
LN_EPS = 1e-5

LOG2_E = math.log2(math.e)
SB_SCALE_LOG2 = HEAD_DIM ** -0.5 * LOG2_E
UNDERFLOW_LOG2 = -110.0 * LOG2_E

V7X_VMEM_BYTES = 64 * 1024 * 1024
VMEM_LIMIT = V7X_VMEM_BYTES - 6 * 1024 * 1024
LANES = 128

ROW_TILE = 512
ROW_CHUNKS = 2
PROJ_COLS = 2048
ADA_COLS = 1024
FF_COLS = 512
Q_BLOCK = 128
PAST_BLOCK = 2 * Q_BLOCK
EAGER_PAST_BLOCKS = 2

BF16 = jnp.bfloat16
F32 = jnp.float32


def _params(*sem):
    return pltpu.CompilerParams(dimension_semantics=sem, vmem_limit_bytes=VMEM_LIMIT)


def _dot(a, b):
    return jnp.dot(a, b, preferred_element_type=F32)


def _normalize(x):
    mu = jnp.mean(x, axis=-1, keepdims=True)
    xc = x - mu
    var = jnp.mean(xc * xc, axis=-1, keepdims=True)
    return xc * lax.rsqrt(var + LN_EPS)


def _modulated_ln(x_ref, mod_ref, shift_idx, scale_idx):
    x = x_ref[...]
    u = _normalize(x) * (1.0 + mod_ref[:, scale_idx:scale_idx + 1, :]) + mod_ref[:, shift_idx:shift_idx + 1, :]
    return u.reshape(x.shape[0] * x.shape[1], x.shape[2])


def _row_chunks(seqs, rows):
    if seqs >= ROW_CHUNKS:
        n = seqs // ROW_CHUNKS
        return [(slice(i * n, (i + 1) * n), slice(0, rows)) for i in range(ROW_CHUNKS)]
    n = rows // ROW_CHUNKS
    return [(slice(0, seqs), slice(i * n, (i + 1) * n)) for i in range(ROW_CHUNKS)]


def _residual_ln(x_ref, mod_ref, gate_idx, branch, g_ref, b_ref):
    x = x_ref[...]
    y = DN_ALPHA * x + mod_ref[:, gate_idx:gate_idx + 1, :] * branch.reshape(x.shape)
    return _normalize(y) * g_ref[...] + b_ref[...]


def _adaln_kernel(c_ref, w_ref, b_ref, o_ref):
    a = jax.nn.silu(c_ref[...]).astype(BF16)
    o_ref[...] = _dot(a, w_ref[...].astype(BF16)) + b_ref[...]


def _adaln(c, w_ada, b_ada):
    n = c.shape[0]
    cols = w_ada.shape[1]
    return pl.pallas_call(
        _adaln_kernel,
        grid=(cols // ADA_COLS,),
        in_specs=[
            pl.BlockSpec((n, D_MODEL), lambda j: (0, 0)),
            pl.BlockSpec((D_MODEL, ADA_COLS), lambda j: (0, j)),
            pl.BlockSpec((1, ADA_COLS), lambda j: (0, j)),
        ],
        out_specs=pl.BlockSpec((n, ADA_COLS), lambda j: (0, j)),
        out_shape=jax.ShapeDtypeStruct((n, cols), F32),
        compiler_params=_params("arbitrary"),
        name="adaln",
    )(c, w_ada, b_ada.reshape(1, cols))


_QK_COL, _VP_COL, _GA_COL, _GB_COL = range(4)


def _proj_kernel(x_ref, mod_ref, w_ref, qkv_ref, k_ref, v_ref, p_ref, gab_ref, u_ref):
    j = pl.program_id(2)
    seqs, rows = x_ref.shape[0], x_ref.shape[1]

    @pl.when(j == 0)
    def _():
        u_ref[...] = _modulated_ln(x_ref, mod_ref, 0, 1).astype(BF16)

    def columns():
        return _dot(u_ref[...], w_ref[...])

    def store_heads(ref, res):
        for s in range(seqs):
            for h in range(N_HEADS):
                ref[s, h] = res[s * rows:(s + 1) * rows, h * HEAD_DIM:(h + 1) * HEAD_DIM]

    def store_bf16(ref, first_col, res):
        ref[:, :, first_col:first_col + res.shape[1]] = res.astype(BF16).reshape(seqs, rows, res.shape[1])

    @pl.when(j == _QK_COL)
    def _():
        res = columns()
        store_bf16(qkv_ref, 0, res)
        store_heads(k_ref, res[:, D_SB:])

    @pl.when(j == _VP_COL)
    def _():
        res = columns()
        store_bf16(qkv_ref, 2 * D_SB, res[:, :D_SB])
        store_heads(v_ref, res[:, :D_SB])
        p_ref[...] = res[:, D_SB:].reshape(p_ref.shape)

    @pl.when(j == _GA_COL)
    def _():
        store_bf16(gab_ref, 0, columns())

    @pl.when(j == _GB_COL)
    def _():
        store_bf16(gab_ref, D_MODEL, columns())


def _proj(x, mod, w_in, seqs, rows):
    B, T, _ = x.shape
    assert 2 * D_SB == D_SB + D_POOL == D_MODEL == PROJ_COLS
    per_seq = T // rows
    last_tile = (B // seqs) * per_seq - 1

    def tile_after(final_step):
        def index(b, t, j):
            tile = jnp.minimum(b * per_seq + t + (j > final_step).astype(jnp.int32), last_tile)
            return tile // per_seq, tile % per_seq
        return index

    def rows_spec(width, final_step=None):
        if final_step is None:
            return pl.BlockSpec((seqs, rows, width), lambda b, t, j: (b, t, 0))
        index = tile_after(final_step)
        return pl.BlockSpec((seqs, rows, width), lambda b, t, j: (*index(b, t, j), 0))

    def heads_spec(final_step):
        index = tile_after(final_step)

        def index_map(b, t, j):
            bb, tt = index(b, t, j)
            return 0, bb, 0, tt, 0
        return pl.BlockSpec((None, seqs, N_HEADS, rows, HEAD_DIM), index_map)

    head_shape = jax.ShapeDtypeStruct((DEPTH, B, N_HEADS, T, HEAD_DIM), F32)
    return pl.pallas_call(
        _proj_kernel,
        grid=(B // seqs, T // rows, D_IN // PROJ_COLS),
        in_specs=[
            rows_spec(D_MODEL),
            pl.BlockSpec((seqs, 6, D_MODEL), lambda b, t, j: (b, 0, 0)),
            pl.BlockSpec((D_MODEL, PROJ_COLS), lambda b, t, j: (0, j)),
        ],
        out_specs=[rows_spec(3 * D_SB, _VP_COL), heads_spec(_QK_COL), heads_spec(_VP_COL), rows_spec(D_POOL, _VP_COL),
                   rows_spec(2 * D_MODEL)],
        out_shape=[
            jax.ShapeDtypeStruct((B, T, 3 * D_SB), BF16),
            head_shape,
            head_shape,
            jax.ShapeDtypeStruct((B, T, D_POOL), F32),
            jax.ShapeDtypeStruct((B, T, 2 * D_MODEL), BF16),
        ],
        scratch_shapes=[pltpu.VMEM((seqs * rows, D_MODEL), BF16)],
        compiler_params=_params("arbitrary", "arbitrary", "arbitrary"),
        name="proj",
    )(x, mod, w_in)


def _suffix_sum_matrix(n):
    r = lax.broadcasted_iota(jnp.int32, (n, n), 0)
    c = lax.broadcasted_iota(jnp.int32, (n, n), 1)
    return jnp.where(r >= c, -1.0, 0.0).astype(BF16)


def _sb_blocks(qs, blocks, sums, carry_ref, acc_ref):
    heads, tq = len(qs), qs[0].shape[0]
    row = lax.broadcasted_iota(jnp.int32, (tq, tq), 0)
    col = lax.broadcasted_iota(jnp.int32, (tq, tq), 1)
    mask = col < row
    zs = [[lax.dot_general(q, k, (((1,), (1,)), ((), ())), preferred_element_type=F32) * SB_SCALE_LOG2
           for q, k in zip(qs, k_blks)] for k_blks, _, _ in blocks]
    sums_out = []
    for z_blk, (k_blks, _, diagonal) in zip(zs, blocks):
        terms = []
        for z in z_blk:
            neg_l = jnp.maximum(z, 0.0) + jnp.log2(1.0 + jnp.exp2(-jnp.abs(z)))
            if diagonal:
                neg_l = jnp.where(mask, neg_l, 0.0)
            hi = neg_l.astype(BF16)
            terms += [hi, (neg_l - hi.astype(F32)).astype(BF16)]
        sums_out.append(_dot(jnp.concatenate(terms, axis=0), sums[k_blks[0].shape[0]]))
    carries = [carry_ref[h] for h in range(heads)]
    pvs = []
    for b, (k_blks, v_blks, diagonal) in enumerate(blocks):
        n = k_blks[0].shape[0]
        ws = []
        for h in range(heads):
            at = 2 * h * tq
            incl = sums_out[b][at:at + tq] + sums_out[b][at + tq:at + 2 * tq]
            w = []
            for c in range(0, n, LANES):
                width = min(LANES, n - c)
                w.append(jnp.exp2(zs[b][h][:, c:c + width] + incl[:, c:c + width] + carries[h][:, :width]))
            w = w[0] if len(w) == 1 else jnp.concatenate(w, axis=1)
            if diagonal:
                w = jnp.where(mask, w, 0.0)
            ws.append(w.astype(BF16))
            carries[h] = carries[h] + jnp.broadcast_to(incl[:, :1], carries[h].shape)
        pvs.append([_dot(w, v) for w, v in zip(ws, v_blks)])
    for h in range(heads):
        carry_ref[h] = carries[h]
        acc_ref[h] += functools.reduce(lambda a, b: a + b, [pv[h] for pv in pvs])


def _walk_past(n_blocks, bound, carry_ref, wide_fn, narrow_fn):
    def cond(state):
        j, bound = state
        return (j >= 2) & (bound > UNDERFLOW_LOG2)

    def body(state):
        j, _ = state
        wide_fn(pl.multiple_of((j - 2) * Q_BLOCK, Q_BLOCK))
        return j - 2, jnp.max(carry_ref[...])

    j, bound = lax.while_loop(cond, body, (n_blocks, bound))
    if narrow_fn is not None:
        pl.when((j == 1) & (bound > UNDERFLOW_LOG2))(narrow_fn)


def _head_cols(h):
    return slice(h * HEAD_DIM, (h + 1) * HEAD_DIM)


def _attn_prompt_kernel(q_ref, k_ref, v_ref, o_ref, carry_ref, acc_ref):
    qi = pl.program_id(1)
    carry_ref[...] = jnp.zeros_like(carry_ref)
    acc_ref[...] = jnp.zeros_like(acc_ref)

    hs = [_head_cols(h) for h in range(N_HEADS)]

    def keys(start, n, diagonal):
        rows = pl.ds(pl.multiple_of(start, Q_BLOCK), n)
        return [k_ref[rows, c] for c in hs], [v_ref[rows, c] for c in hs], diagonal

    def visit(blocks, sums):
        _sb_blocks([q_ref[:, c] for c in hs], blocks, sums, carry_ref, acc_ref)

    def walk(n_blocks, bound, sums):
        _walk_past(n_blocks, bound, carry_ref,
                   lambda start: visit([keys(start, PAST_BLOCK, False)], sums),
                   lambda: visit([keys(0, Q_BLOCK, False)], sums))

    eager = EAGER_PAST_BLOCKS * (PAST_BLOCK // Q_BLOCK)

    @pl.when(qi >= eager)
    def _():
        sums = {n: _suffix_sum_matrix(n) for n in (Q_BLOCK, PAST_BLOCK)}
        visit([keys(qi * Q_BLOCK, Q_BLOCK, True)]
              + [keys(qi * Q_BLOCK - b * PAST_BLOCK, PAST_BLOCK, False) for b in range(1, EAGER_PAST_BLOCKS + 1)], sums)
        walk(qi - eager, jnp.max(carry_ref[...]), sums)

    @pl.when(qi < eager)
    def _():
        sums = {n: _suffix_sum_matrix(n) for n in (Q_BLOCK, PAST_BLOCK)}
        visit([keys(qi * Q_BLOCK, Q_BLOCK, True)], sums)
        walk(qi, jnp.float32(0.0), sums)

    for h in range(N_HEADS):
        o_ref[:, _head_cols(h)] = acc_ref[h].astype(o_ref.dtype)


def _attn_prompt(qkv):
    B, T, _ = qkv.shape
    q_spec = pl.BlockSpec((None, Q_BLOCK, D_SB), lambda b, i: (b, i, 0))
    kv_spec = lambda part: pl.BlockSpec((None, T, D_SB), lambda b, i: (b, 0, part))
    return pl.pallas_call(
        _attn_prompt_kernel,
        grid=(B, T // Q_BLOCK),
        in_specs=[q_spec, kv_spec(1), kv_spec(2)],
        out_specs=q_spec,
        out_shape=jax.ShapeDtypeStruct((B, T, D_SB), BF16),
        scratch_shapes=[pltpu.VMEM((N_HEADS, Q_BLOCK, LANES), F32), pltpu.VMEM((N_HEADS, Q_BLOCK, HEAD_DIM), F32)],
        compiler_params=_params("parallel", "arbitrary"),
        name="attn_prompt",
    )(qkv, qkv, qkv)


def _attn_sample_kernel(q_ref, k_ref, v_ref, ck_ref, cv_ref, o_ref, carry_ref, acc_ref):
    tq = q_ref.shape[0]
    carry_ref[...] = jnp.zeros_like(carry_ref)
    acc_ref[...] = jnp.zeros_like(acc_ref)
    hs = [_head_cols(h) for h in range(N_HEADS)]
    qs = [q_ref[:, c] for c in hs]
    sums = {n: _suffix_sum_matrix(n) for n in (tq, PAST_BLOCK)}
    _sb_blocks(qs, [([k_ref[:, c] for c in hs], [v_ref[:, c] for c in hs], True)], sums, carry_ref, acc_ref)

    def visit(start):
        rows = pl.ds(start, PAST_BLOCK)
        cached = ([ck_ref[h, rows, :].astype(BF16) for h in range(N_HEADS)],
                  [cv_ref[h, rows, :].astype(BF16) for h in range(N_HEADS)], False)
        _sb_blocks(qs, [cached], sums, carry_ref, acc_ref)

    past = ck_ref.shape[1]
    assert past % PAST_BLOCK == 0
    _walk_past(past // Q_BLOCK, jnp.float32(0.0), carry_ref, visit, None)
    for h in range(N_HEADS):
        o_ref[:, _head_cols(h)] = acc_ref[h].astype(o_ref.dtype)


def _attn_sample(qkv, cache_k, cache_v):
    B, T, _ = qkv.shape
    past = cache_k.shape[3]
    new_spec = lambda part: pl.BlockSpec((None, T, D_SB), lambda b: (b, 0, part))
    cache_spec = pl.BlockSpec((None, None, N_HEADS, past, HEAD_DIM), lambda b: (0, b, 0, 0, 0))
    return pl.pallas_call(
        _attn_sample_kernel,
        grid=(B,),
        in_specs=[new_spec(0), new_spec(1), new_spec(2), cache_spec, cache_spec],
        out_specs=new_spec(0),
        out_shape=jax.ShapeDtypeStruct((B, T, D_SB), BF16),
        scratch_shapes=[pltpu.VMEM((N_HEADS, T, LANES), F32), pltpu.VMEM((N_HEADS, T, HEAD_DIM), F32)],
        compiler_params=_params("parallel"),
        name="attn_sample",
    )(qkv, qkv, qkv, cache_k, cache_v)


def _gate_kernel(o_ref, p_ref, halo_ref, ga_ref, gb_ref, wsb_ref, wpool_ref, pscale_ref, m_ref,
                 *, pos0, zero_first_halo):
    seqs, rows = p_ref.shape[0], p_ref.shape[1]
    t0 = pl.program_id(1) * rows
    halo = halo_ref[...]
    if zero_first_halo:
        halo = jnp.where(t0 == 0, 0.0, halo)
    pos = pos0 + t0 + lax.broadcasted_iota(jnp.int32, (rows, LANES), 0)

    y_b = []
    for g, win in enumerate(POOL_WINDOWS):
        cols = slice(g * POOL_GROUP, (g + 1) * POOL_GROUP)
        inv_cnt = 1.0 / jnp.minimum(win, pos + 1).astype(F32)
        inv_cnt = jnp.concatenate([inv_cnt] * (POOL_GROUP // LANES), axis=1)
        diffs = []
        for s in range(seqs):
            p = p_ref[s, :, cols]
            acc = jnp.concatenate([halo[s, :, cols], p], axis=0)
            shift = 1
            while shift < win:
                acc = acc + pltpu.roll(acc, shift, 0)
                shift *= 2
            diffs.append(acc[POOL_HALO:, :] * inv_cnt - p)
        d = jnp.concatenate(diffs, axis=0).astype(BF16)
        y_b.append(_dot(d, wpool_ref[g]))
    y_b = jnp.concatenate(y_b, axis=1) * pscale_ref[...]

    gate_b = jax.nn.sigmoid(gb_ref[...].astype(F32)).reshape(seqs * rows, D_MODEL)
    gated_b = gate_b * y_b
    gate_a = jax.nn.sigmoid(ga_ref[...].astype(F32)).reshape(seqs * rows, D_MODEL)
    y_a = _dot(o_ref[...].reshape(seqs * rows, D_SB), wsb_ref[...])
    m_ref[...] = (gate_a * y_a + gated_b).astype(BF16).reshape(m_ref.shape)


def _gate(o, p, halo_src, gab, w_sb_out, w_pool, pool_scale, seqs, rows, pos0, halo_from_p):
    B, T, _ = p.shape
    row_block = lambda width, part=0: pl.BlockSpec((seqs, rows, width), lambda b, t: (b, t, part))
    if halo_from_p:
        per_tile = rows // POOL_HALO
        halo_spec = pl.BlockSpec((seqs, POOL_HALO, D_POOL),
                                 lambda b, t: (b, jnp.maximum(t * per_tile - 1, 0), 0))
    else:
        halo_spec = pl.BlockSpec((seqs, POOL_HALO, D_POOL), lambda b, t: (b, 0, 0))
    const = lambda shape: pl.BlockSpec(shape, lambda b, t: (0,) * len(shape))
    return pl.pallas_call(
        functools.partial(_gate_kernel, pos0=pos0, zero_first_halo=halo_from_p),
        grid=(B // seqs, T // rows),
        in_specs=[
            row_block(D_SB), row_block(D_POOL), halo_spec, row_block(D_MODEL, 0), row_block(D_MODEL, 1),
            const((D_SB, D_MODEL)), const((N_POOL_GROUPS, POOL_GROUP, POOL_OUT)), const((1, D_MODEL)),
        ],
        out_specs=row_block(D_MODEL),
        out_shape=jax.ShapeDtypeStruct((B, T, D_MODEL), BF16),
        compiler_params=_params("parallel", "arbitrary"),
        name="gate",
    )(o, p, halo_src, gab, gab, w_sb_out, w_pool, pool_scale)


def _mix_kernel(m_ref, x_ref, mod_ref, w_ref, g_ref, b_ref, o_ref):
    seqs, rows = x_ref.shape[0], x_ref.shape[1]
    for s, r in _row_chunks(seqs, rows):
        mix = _dot(m_ref[s, r, :].reshape(-1, D_MODEL), w_ref[...])
        o_ref[s, r, :] = _residual_ln(x_ref.at[s, r, :], mod_ref.at[s], 2, mix, g_ref, b_ref)


def _mix(m, x, mod, w_out, ln_g, ln_b, seqs, rows):
    B, T, _ = x.shape
    row_block = pl.BlockSpec((seqs, rows, D_MODEL), lambda b, t: (b, t, 0))
    vec = pl.BlockSpec((1, D_MODEL), lambda b, t: (0, 0))
    return pl.pallas_call(
        _mix_kernel,
        grid=(B // seqs, T // rows),
        in_specs=[
            row_block, row_block,
            pl.BlockSpec((seqs, 6, D_MODEL), lambda b, t: (b, 0, 0)),
            pl.BlockSpec((D_MODEL, D_MODEL), lambda b, t: (0, 0)),
            vec, vec,
        ],
        out_specs=row_block,
        out_shape=jax.ShapeDtypeStruct((B, T, D_MODEL), F32),
        compiler_params=_params("parallel", "arbitrary"),
        name="mix",
    )(m, x, mod, w_out, ln_g, ln_b)


def _ffn_kernel(x_ref, mod_ref, wg_ref, wu_ref, wd_ref, g_ref, b_ref, o_ref, u_ref, acc_ref):
    j = pl.program_id(2)
    last = pl.num_programs(2) - 1

    def chunk(u):
        h = jax.nn.silu(_dot(u, wg_ref[...])) * _dot(u, wu_ref[...])
        return _dot(h.astype(BF16), wd_ref[...])

    @pl.when(j == 0)
    def _():
        u = _modulated_ln(x_ref, mod_ref, 3, 4).astype(BF16)
        u_ref[...] = u
        acc_ref[...] = chunk(u)

    @pl.when((j > 0) & (j < last))
    def _():
        acc_ref[...] += chunk(u_ref[...])

    @pl.when(j == last)
    def _():
        o_ref[...] = _residual_ln(x_ref, mod_ref, 5, acc_ref[...] + chunk(u_ref[...]), g_ref, b_ref)


def _ffn(x, mod, w_gate, w_up, w_down, ln_g, ln_b, seqs, rows):
    B, T, _ = x.shape
    d_ff = w_gate.shape[1]
    row_block = pl.BlockSpec((seqs, rows, D_MODEL), lambda b, t, j: (b, t, 0))
    vec = pl.BlockSpec((1, D_MODEL), lambda b, t, j: (0, 0))
    return pl.pallas_call(
        _ffn_kernel,
        grid=(B // seqs, T // rows, d_ff // FF_COLS),
        in_specs=[
            row_block,
            pl.BlockSpec((seqs, 6, D_MODEL), lambda b, t, j: (b, 0, 0)),
            pl.BlockSpec((D_MODEL, FF_COLS), lambda b, t, j: (0, j)),
            pl.BlockSpec((D_MODEL, FF_COLS), lambda b, t, j: (0, j)),
            pl.BlockSpec((FF_COLS, D_MODEL), lambda b, t, j: (j, 0)),
            vec, vec,
        ],
        out_specs=row_block,
        out_shape=jax.ShapeDtypeStruct((B, T, D_MODEL), F32),
        scratch_shapes=[pltpu.VMEM((seqs * rows, D_MODEL), BF16), pltpu.VMEM((seqs * rows, D_MODEL), F32)],
        compiler_params=_params("parallel", "parallel", "arbitrary"),
        name="ffn",
    )(x, mod, w_gate, w_up, w_down, ln_g, ln_b)


def _tile(batch, seq, row_tile=ROW_TILE):
    if seq >= row_tile:
        return 1, row_tile
    return min(batch, row_tile // seq), seq


def _layer(x, mod, attn_fn, halo_src, pos0, halo_from_p, w):
    B, T, _ = x.shape
    seqs, rows = _tile(B, T)
    qkv, k, v, p, gab = _proj(x, mod, w["w_in"], seqs, rows)
    o = attn_fn(qkv)
    m = _gate(o, p, p if halo_from_p else halo_src, gab, w["w_sb_out"], w["w_pool"], w["pool_scale"],
              seqs, rows, pos0, halo_from_p)
    x = _mix(m, x, mod, w["w_out"], w["ln1_g"], w["ln1_b"], seqs, rows)
    x = _ffn(x, mod, w["w_gate"], w["w_up"], w["w_down"], w["ln2_g"], w["ln2_b"], seqs, rows)
    return x, k, v, p[None, :, T - POOL_PAST:, :]


def kernel(x_prompt, x_sample, c_prompt, c_sample, cache_k, cache_v, state_pool, w_ada, b_ada, w_in, w_sb_out, w_pool, pool_scale, w_out, ln1_g, ln1_b, w_gate, w_up, w_down, ln2_g, ln2_b):
    assert w_in.shape[0] == DEPTH
    n_prompt = c_prompt.shape[0]
    mod = _adaln(jnp.concatenate([c_prompt, c_sample], axis=0), w_ada[0], b_ada[0])
    mod = mod.reshape(mod.shape[0], 6, D_MODEL)
    w = {
        "w_in": w_in[0].astype(BF16), "w_sb_out": w_sb_out[0].astype(BF16), "w_pool": w_pool[0].astype(BF16),
        "w_out": w_out[0].astype(BF16), "w_gate": w_gate[0].astype(BF16), "w_up": w_up[0].astype(BF16),
        "w_down": w_down[0].astype(BF16), "pool_scale": pool_scale, "ln1_g": ln1_g, "ln1_b": ln1_b,
        "ln2_g": ln2_g, "ln2_b": ln2_b,
    }
    y_p, k_p, v_p, pool_p = _layer(x_prompt, mod[:n_prompt], _attn_prompt, None, 0, True, w)

    pool_halo = jnp.pad(state_pool[0], ((0, 0), (POOL_HALO - POOL_PAST, 0), (0, 0)))
    attn_sample = lambda qkv: _attn_sample(qkv, cache_k, cache_v)
    y_s, k_s, v_s, pool_s = _layer(x_sample, mod[n_prompt:], attn_sample, pool_halo, cache_k.shape[3], False, w)
    return y_p, y_s, k_p, v_p, pool_p, k_s, v_s, pool_s
```

```python
import functools
import math

import jax
import jax.numpy as jnp
from jax import lax
from jax.experimental import pallas as pl
from jax.experimental.pallas import tpu as pltpu

D_MODEL = 2048
N_HEADS = 8
HEAD_DIM = 128
D_SB = N_HEADS * HEAD_DIM
POOL_WINDOWS = (2, 4, 8, 16)
N_POOL_GROUPS = len(POOL_WINDOWS)
D_POOL = 1024
POOL_GROUP = D_POOL // N_POOL_GROUPS
POOL_OUT = D_MODEL // N_POOL_GROUPS
POOL_PAST = max(POOL_WINDOWS) - 1
POOL_HALO = POOL_PAST + 1
D_IN = 3 * D_SB + D_POOL + 2 * D_MODEL
DEPTH = 1
DN_ALPHA = (2 * DEPTH) ** 0.25
LN_EPS = 1e-5

LOG2_E = math.log2(math.e)
SB_SCALE_LOG2 = HEAD_DIM ** -0.5 * LOG2_E
UNDERFLOW_LOG2 = -110.0 * LOG2_E

V7X_VMEM_BYTES = 64 * 1024 * 1024
VMEM_LIMIT = V7X_VMEM_BYTES - 6 * 1024 * 1024
LANES = 128

ROW_TILE = 512
ROW_CHUNKS = 2
PROJ_COLS = 2048
ADA_COLS = 1024
CAST_ROWS = 256
FF_COLS = 512
Q_BLOCK = 128
PAST_BLOCK = 2 * Q_BLOCK
EAGER_PAST_BLOCKS = 2

BF16 = jnp.bfloat16
F32 = jnp.float32


def _params(*sem):
    return pltpu.CompilerParams(dimension_semantics=sem, vmem_limit_bytes=VMEM_LIMIT)


def _dot(a, b):
    return jnp.dot(a, b, preferred_element_type=F32)


def _normalize(x):
    mu = jnp.mean(x, axis=-1, keepdims=True)
    xc = x - mu
    var = jnp.mean(xc * xc, axis=-1, keepdims=True)
    return xc * lax.rsqrt(var + LN_EPS)


def _modulated_ln(x_ref, mod_ref, shift_idx, scale_idx):
    x = x_ref[...]
    u = _normalize(x) * (1.0 + mod_ref[:, scale_idx:scale_idx + 1, :]) + mod_ref[:, shift_idx:shift_idx + 1, :]
    return u.reshape(x.shape[0] * x.shape[1], x.shape[2])


def _row_chunks(seqs, rows):
    if seqs >= ROW_CHUNKS:
        n = seqs // ROW_CHUNKS
        return [(slice(i * n, (i + 1) * n), slice(0, rows)) for i in range(ROW_CHUNKS)]
    n = rows // ROW_CHUNKS
    return [(slice(0, seqs), slice(i * n, (i + 1) * n)) for i in range(ROW_CHUNKS)]


def _residual_ln(x_ref, mod_ref, gate_idx, branch, g_ref, b_ref):
    x = x_ref[...]
    y = DN_ALPHA * x + mod_ref[:, gate_idx:gate_idx + 1, :] * branch.reshape(x.shape)
    return _normalize(y) * g_ref[...] + b_ref[...]


def _cast_kernel(w_ref, o_ref):
    o_ref[...] = w_ref[...].astype(BF16)


def _to_bf16(w):
    _, R, C = w.shape
    rows = min(R, CAST_ROWS)
    return pl.pallas_call(
        _cast_kernel,
        grid=(R // rows,),
        in_specs=[pl.BlockSpec((None, rows, C), lambda i: (0, i, 0))],
        out_specs=pl.BlockSpec((rows, C), lambda i: (i, 0)),
        out_shape=jax.ShapeDtypeStruct((R, C), BF16),
        compiler_params=_params("parallel"),
        name="cast",
    )(w)


def _adaln_kernel(c_ref, w_ref, b_ref, o_ref):
    a = jax.nn.silu(c_ref[...]).astype(BF16)
    o_ref[...] = _dot(a, w_ref[...].astype(BF16)) + b_ref[...]


def _adaln(c, w_ada, b_ada):
    n = c.shape[0]
    cols = w_ada.shape[1]
    return pl.pallas_call(
        _adaln_kernel,
        grid=(cols // ADA_COLS,),
        in_specs=[
            pl.BlockSpec((n, D_MODEL), lambda j: (0, 0)),
            pl.BlockSpec((D_MODEL, ADA_COLS), lambda j: (0, j)),
            pl.BlockSpec((1, ADA_COLS), lambda j: (0, j)),
        ],
        out_specs=pl.BlockSpec((n, ADA_COLS), lambda j: (0, j)),
        out_shape=jax.ShapeDtypeStruct((n, cols), F32),
        compiler_params=_params("arbitrary"),
        name="adaln",
    )(c, w_ada, b_ada.reshape(1, cols))


_QK_COL, _VP_COL, _GA_COL, _GB_COL = range(4)


def _proj_kernel(x_ref, mod_ref, w_ref, qkv_ref, k_ref, v_ref, p_ref, gab_ref, u_ref):
    j = pl.program_id(2)
    seqs, rows = x_ref.shape[0], x_ref.shape[1]

    @pl.when(j == 0)
    def _():
        u_ref[...] = _modulated_ln(x_ref, mod_ref, 0, 1).astype(BF16)

    def columns():
        return _dot(u_ref[...], w_ref[...])

    def store_heads(ref, res):
        for s in range(seqs):
            for h in range(N_HEADS):
                ref[s, h] = res[s * rows:(s + 1) * rows, h * HEAD_DIM:(h + 1) * HEAD_DIM]

    def store_bf16(ref, first_col, res):
        ref[:, :, first_col:first_col + res.shape[1]] = res.astype(BF16).reshape(seqs, rows, res.shape[1])

    @pl.when(j == _QK_COL)
    def _():
        res = columns()
        store_bf16(qkv_ref, 0, res)
        store_heads(k_ref, res[:, D_SB:])

    @pl.when(j == _VP_COL)
    def _():
        res = columns()
        store_bf16(qkv_ref, 2 * D_SB, res[:, :D_SB])
        store_heads(v_ref, res[:, :D_SB])
        p_ref[...] = res[:, D_SB:].reshape(p_ref.shape)

    @pl.when(j == _GA_COL)
    def _():
        store_bf16(gab_ref, 0, columns())

    @pl.when(j == _GB_COL)
    def _():
        store_bf16(gab_ref, D_MODEL, columns())


def _proj(x, mod, w_in, seqs, rows):
    B, T, _ = x.shape
    assert 2 * D_SB == D_SB + D_POOL == D_MODEL == PROJ_COLS
    per_seq = T // rows
    last_tile = (B // seqs) * per_seq - 1

    def tile_after(final_step):
        def index(b, t, j):
            tile = jnp.minimum(b * per_seq + t + (j > final_step).astype(jnp.int32), last_tile)
            return tile // per_seq, tile % per_seq
        return index

    def rows_spec(width, final_step=None):
        if final_step is None:
            return pl.BlockSpec((seqs, rows, width), lambda b, t, j: (b, t, 0))
        index = tile_after(final_step)
        return pl.BlockSpec((seqs, rows, width), lambda b, t, j: (*index(b, t, j), 0))

    def heads_spec(final_step):
        index = tile_after(final_step)

        def index_map(b, t, j):
            bb, tt = index(b, t, j)
            return 0, bb, 0, tt, 0
        return pl.BlockSpec((None, seqs, N_HEADS, rows, HEAD_DIM), index_map)

    head_shape = jax.ShapeDtypeStruct((DEPTH, B, N_HEADS, T, HEAD_DIM), F32)
    return pl.pallas_call(
        _proj_kernel,
        grid=(B // seqs, T // rows, D_IN // PROJ_COLS),
        in_specs=[
            rows_spec(D_MODEL),
            pl.BlockSpec((seqs, 6, D_MODEL), lambda b, t, j: (b, 0, 0)),
            pl.BlockSpec((D_MODEL, PROJ_COLS), lambda b, t, j: (0, j)),
        ],
        out_specs=[rows_spec(3 * D_SB, _VP_COL), heads_spec(_QK_COL), heads_spec(_VP_COL), rows_spec(D_POOL, _VP_COL),
                   rows_spec(2 * D_MODEL)],
        out_shape=[
            jax.ShapeDtypeStruct((B, T, 3 * D_SB), BF16),
            head_shape,
            head_shape,
            jax.ShapeDtypeStruct((B, T, D_POOL), F32),
            jax.ShapeDtypeStruct((B, T, 2 * D_MODEL), BF16),
        ],
        scratch_shapes=[pltpu.VMEM((seqs * rows, D_MODEL), BF16)],
        compiler_params=_params("arbitrary", "arbitrary", "arbitrary"),
        name="proj",
    )(x, mod, w_in)


def _suffix_sum_matrix(n):
    r = lax.broadcasted_iota(jnp.int32, (n, n), 0)
    c = lax.broadcasted_iota(jnp.int32, (n, n), 1)
    return jnp.where(r >= c, -1.0, 0.0).astype(BF16)


def _sb_blocks(qs, blocks, sums, carry_ref, acc_ref):
    heads, tq = len(qs), qs[0].shape[0]
    row = lax.broadcasted_iota(jnp.int32, (tq, tq), 0)
    col = lax.broadcasted_iota(jnp.int32, (tq, tq), 1)
    mask = col < row
    zs = [[lax.dot_general(q, k, (((1,), (1,)), ((), ())), preferred_element_type=F32) * SB_SCALE_LOG2
           for q, k in zip(qs, k_blks)] for k_blks, _, _ in blocks]
    terms, first_term = {}, []
    for z_blk, (k_blks, _, diagonal) in zip(zs, blocks):
        of_width = terms.setdefault(k_blks[0].shape[0], [])
        first_term.append(len(of_width))
        for z in z_blk:
            neg_l = jnp.maximum(z, 0.0) + jnp.log2(1.0 + jnp.exp2(-jnp.abs(z)))
            if diagonal:
                neg_l = jnp.where(mask, neg_l, 0.0)
            hi = neg_l.astype(BF16)
            of_width += [hi, (neg_l - hi.astype(F32)).astype(BF16)]
    sums_out = {n: _dot(jnp.concatenate(t, axis=0), sums[n]) for n, t in terms.items()}
    ws = [[] for _ in blocks]
    for h in range(heads):
        carry = carry_ref[h]
        for b, (k_blks, _, diagonal) in enumerate(blocks):
            n = k_blks[0].shape[0]
            at = (first_term[b] + 2 * h) * tq
            incl = sums_out[n][at:at + tq] + sums_out[n][at + tq:at + 2 * tq]
            w = []
            for c in range(0, n, LANES):
                width = min(LANES, n - c)
                w.append(jnp.exp2(zs[b][h][:, c:c + width] + incl[:, c:c + width] + carry[:, :width]))
            w = w[0] if len(w) == 1 else jnp.concatenate(w, axis=1)
            if diagonal:
                w = jnp.where(mask, w, 0.0)
            ws[b].append(w.astype(BF16))
            carry = carry + jnp.broadcast_to(incl[:, :1], carry.shape)
        carry_ref[h] = carry
    pvs = [[_dot(w, v) for w, v in zip(ws[b], v_blks)] for b, (_, v_blks, _) in enumerate(blocks)]
    for h in range(heads):
        acc_ref[h] += functools.reduce(lambda a, b: a + b, [pv[h] for pv in pvs])


def _walk_past(n_blocks, bound, carry_ref, wide_fn, narrow_fn):
    def cond(state):
        j, bound = state
        return (j >= 2) & (bound > UNDERFLOW_LOG2)

    def body(state):
        j, _ = state
        wide_fn(pl.multiple_of((j - 2) * Q_BLOCK, Q_BLOCK))
        return j - 2, jnp.max(carry_ref[...])

    j, bound = lax.while_loop(cond, body, (n_blocks, bound))
    if narrow_fn is not None:
        pl.when((j == 1) & (bound > UNDERFLOW_LOG2))(narrow_fn)


def _head_cols(h):
    return slice(h * HEAD_DIM, (h + 1) * HEAD_DIM)


def _attn_prompt_kernel(q_ref, k_ref, v_ref, o_ref, carry_ref, acc_ref):
    qi = pl.program_id(1)
    carry_ref[...] = jnp.zeros_like(carry_ref)
    acc_ref[...] = jnp.zeros_like(acc_ref)

    hs = [_head_cols(h) for h in range(N_HEADS)]

    def keys(start, n, diagonal):
        rows = pl.ds(pl.multiple_of(start, Q_BLOCK), n)
        return [k_ref[rows, c] for c in hs], [v_ref[rows, c] for c in hs], diagonal

    def visit(blocks, sums):
        _sb_blocks([q_ref[:, c] for c in hs], blocks, sums, carry_ref, acc_ref)

    def walk(n_blocks, bound, sums):
        _walk_past(n_blocks, bound, carry_ref,
                   lambda start: visit([keys(start, PAST_BLOCK, False)], sums),
                   lambda: visit([keys(0, Q_BLOCK, False)], sums))

    eager = EAGER_PAST_BLOCKS * (PAST_BLOCK // Q_BLOCK)

    @pl.when(qi >= eager)
    def _():
        sums = {n: _suffix_sum_matrix(n) for n in (Q_BLOCK, PAST_BLOCK)}
        visit([keys(qi * Q_BLOCK, Q_BLOCK, True)]
              + [keys(qi * Q_BLOCK - b * PAST_BLOCK, PAST_BLOCK, False) for b in range(1, EAGER_PAST_BLOCKS + 1)], sums)
        walk(qi - eager, jnp.max(carry_ref[...]), sums)

    @pl.when(qi < eager)
    def _():
        sums = {n: _suffix_sum_matrix(n) for n in (Q_BLOCK, PAST_BLOCK)}
        visit([keys(qi * Q_BLOCK, Q_BLOCK, True)], sums)
        walk(qi, jnp.float32(0.0), sums)

    for h in range(N_HEADS):
        o_ref[:, _head_cols(h)] = acc_ref[h].astype(o_ref.dtype)


def _attn_prompt(qkv):
    B, T, _ = qkv.shape
    q_spec = pl.BlockSpec((None, Q_BLOCK, D_SB), lambda b, i: (b, i, 0))
    kv_spec = lambda part: pl.BlockSpec((None, T, D_SB), lambda b, i: (b, 0, part))
    return pl.pallas_call(
        _attn_prompt_kernel,
        grid=(B, T // Q_BLOCK),
        in_specs=[q_spec, kv_spec(1), kv_spec(2)],
        out_specs=q_spec,
        out_shape=jax.ShapeDtypeStruct((B, T, D_SB), BF16),
        scratch_shapes=[pltpu.VMEM((N_HEADS, Q_BLOCK, LANES), F32), pltpu.VMEM((N_HEADS, Q_BLOCK, HEAD_DIM), F32)],
        compiler_params=_params("parallel", "arbitrary"),
        name="attn_prompt",
    )(qkv, qkv, qkv)


def _attn_sample_kernel(q_ref, k_ref, v_ref, ck_ref, cv_ref, o_ref, carry_ref, acc_ref):
    tq = q_ref.shape[0]
    carry_ref[...] = jnp.zeros_like(carry_ref)
    acc_ref[...] = jnp.zeros_like(acc_ref)
    hs = [_head_cols(h) for h in range(N_HEADS)]
    qs = [q_ref[:, c] for c in hs]
    sums = {n: _suffix_sum_matrix(n) for n in (tq, PAST_BLOCK)}
    _sb_blocks(qs, [([k_ref[:, c] for c in hs], [v_ref[:, c] for c in hs], True)], sums, carry_ref, acc_ref)

    def visit(start):
        rows = pl.ds(start, PAST_BLOCK)
        cached = ([ck_ref[h, rows, :].astype(BF16) for h in range(N_HEADS)],
                  [cv_ref[h, rows, :].astype(BF16) for h in range(N_HEADS)], False)
        _sb_blocks(qs, [cached], sums, carry_ref, acc_ref)

    past = ck_ref.shape[1]
    assert past % PAST_BLOCK == 0
    _walk_past(past // Q_BLOCK, jnp.float32(0.0), carry_ref, visit, None)
    for h in range(N_HEADS):
        o_ref[:, _head_cols(h)] = acc_ref[h].astype(o_ref.dtype)


def _attn_sample(qkv, cache_k, cache_v):
    B, T, _ = qkv.shape
    past = cache_k.shape[3]
    new_spec = lambda part: pl.BlockSpec((None, T, D_SB), lambda b: (b, 0, part))
    cache_spec = pl.BlockSpec((None, None, N_HEADS, past, HEAD_DIM), lambda b: (0, b, 0, 0, 0))
    return pl.pallas_call(
        _attn_sample_kernel,
        grid=(B,),
        in_specs=[new_spec(0), new_spec(1), new_spec(2), cache_spec, cache_spec],
        out_specs=new_spec(0),
        out_shape=jax.ShapeDtypeStruct((B, T, D_SB), BF16),
        scratch_shapes=[pltpu.VMEM((N_HEADS, T, LANES), F32), pltpu.VMEM((N_HEADS, T, HEAD_DIM), F32)],
        compiler_params=_params("parallel"),
        name="attn_sample",
    )(qkv, qkv, qkv, cache_k, cache_v)


def _gate_kernel(o_ref, p_ref, halo_ref, ga_ref, gb_ref, wsb_ref, wpool_ref, pscale_ref, m_ref,
                 *, pos0, zero_first_halo):
    seqs, rows = p_ref.shape[0], p_ref.shape[1]
    t0 = pl.program_id(1) * rows
    halo = halo_ref[...]
    if zero_first_halo:
        halo = jnp.where(t0 == 0, 0.0, halo)
    pos = pos0 + t0 + lax.broadcasted_iota(jnp.int32, (rows, LANES), 0)

    y_b = []
    for g, win in enumerate(POOL_WINDOWS):
        cols = slice(g * POOL_GROUP, (g + 1) * POOL_GROUP)
        inv_cnt = 1.0 / jnp.minimum(win, pos + 1).astype(F32)
        inv_cnt = jnp.concatenate([inv_cnt] * (POOL_GROUP // LANES), axis=1)
        diffs = []
        for s in range(seqs):
            p = p_ref[s, :, cols]
            acc = jnp.concatenate([halo[s, :, cols], p], axis=0)
            shift = 1
            while shift < win:
                acc = acc + pltpu.roll(acc, shift, 0)
                shift *= 2
            diffs.append(acc[POOL_HALO:, :] * inv_cnt - p)
        d = jnp.concatenate(diffs, axis=0).astype(BF16)
        y_b.append(_dot(d, wpool_ref[g]))
    y_b = jnp.concatenate(y_b, axis=1) * pscale_ref[...]

    gate_b = jax.nn.sigmoid(gb_ref[...].astype(F32)).reshape(seqs * rows, D_MODEL)
    gated_b = gate_b * y_b
    gate_a = jax.nn.sigmoid(ga_ref[...].astype(F32)).reshape(seqs * rows, D_MODEL)
    y_a = _dot(o_ref[...].reshape(seqs * rows, D_SB), wsb_ref[...])
    m_ref[...] = (gate_a * y_a + gated_b).astype(BF16).reshape(m_ref.shape)


def _gate(o, p, halo_src, gab, w_sb_out, w_pool, pool_scale, seqs, rows, pos0, halo_from_p):
    B, T, _ = p.shape
    row_block = lambda width, part=0: pl.BlockSpec((seqs, rows, width), lambda b, t: (b, t, part))
    if halo_from_p:
        per_tile = rows // POOL_HALO
        halo_spec = pl.BlockSpec((seqs, POOL_HALO, D_POOL),
                                 lambda b, t: (b, jnp.maximum(t * per_tile - 1, 0), 0))
    else:
        halo_spec = pl.BlockSpec((seqs, POOL_HALO, D_POOL), lambda b, t: (b, 0, 0))
    const = lambda shape: pl.BlockSpec(shape, lambda b, t: (0,) * len(shape))
    return pl.pallas_call(
        functools.partial(_gate_kernel, pos0=pos0, zero_first_halo=halo_from_p),
        grid=(B // seqs, T // rows),
        in_specs=[
            row_block(D_SB), row_block(D_POOL), halo_spec, row_block(D_MODEL, 0), row_block(D_MODEL, 1),
            const((D_SB, D_MODEL)), const((N_POOL_GROUPS, POOL_GROUP, POOL_OUT)), const((1, D_MODEL)),
        ],
        out_specs=row_block(D_MODEL),
        out_shape=jax.ShapeDtypeStruct((B, T, D_MODEL), BF16),
        compiler_params=_params("parallel", "arbitrary"),
        name="gate",
    )(o, p, halo_src, gab, gab, w_sb_out, w_pool, pool_scale)


def _mix_kernel(m_ref, x_ref, mod_ref, w_ref, g_ref, b_ref, o_ref):
    seqs, rows = x_ref.shape[0], x_ref.shape[1]
    for s, r in _row_chunks(seqs, rows):
        mix = _dot(m_ref[s, r, :].reshape(-1, D_MODEL), w_ref[...])
        o_ref[s, r, :] = _residual_ln(x_ref.at[s, r, :], mod_ref.at[s], 2, mix, g_ref, b_ref)


def _mix(m, x, mod, w_out, ln_g, ln_b, seqs, rows):
    B, T, _ = x.shape
    row_block = pl.BlockSpec((seqs, rows, D_MODEL), lambda b, t: (b, t, 0))
    vec = pl.BlockSpec((1, D_MODEL), lambda b, t: (0, 0))
    return pl.pallas_call(
        _mix_kernel,
        grid=(B // seqs, T // rows),
        in_specs=[
            row_block, row_block,
            pl.BlockSpec((seqs, 6, D_MODEL), lambda b, t: (b, 0, 0)),
            pl.BlockSpec((D_MODEL, D_MODEL), lambda b, t: (0, 0)),
            vec, vec,
        ],
        out_specs=row_block,
        out_shape=jax.ShapeDtypeStruct((B, T, D_MODEL), F32),
        compiler_params=_params("parallel", "arbitrary"),
        name="mix",
    )(m, x, mod, w_out, ln_g, ln_b)


def _ffn_kernel(x_ref, mod_ref, wg_ref, wu_ref, wd_ref, g_ref, b_ref, o_ref, u_ref, acc_ref):
    j = pl.program_id(2)
    last = pl.num_programs(2) - 1

    def chunk(u):
        h = jax.nn.silu(_dot(u, wg_ref[...])) * _dot(u, wu_ref[...])
        return _dot(h.astype(BF16), wd_ref[...])

    @pl.when(j == 0)
    def _():
        u = _modulated_ln(x_ref, mod_ref, 3, 4).astype(BF16)
        u_ref[...] = u
        acc_ref[...] = chunk(u)

    @pl.when((j > 0) & (j < last))
    def _():
        acc_ref[...] += chunk(u_ref[...])

    @pl.when(j == last)
    def _():
        o_ref[...] = _residual_ln(x_ref, mod_ref, 5, acc_ref[...] + chunk(u_ref[...]), g_ref, b_ref)


def _ffn(x, mod, w_gate, w_up, w_down, ln_g, ln_b, seqs, rows):
    B, T, _ = x.shape
    d_ff = w_gate.shape[1]
    row_block = pl.BlockSpec((seqs, rows, D_MODEL), lambda b, t, j: (b, t, 0))
    vec = pl.BlockSpec((1, D_MODEL), lambda b, t, j: (0, 0))
    return pl.pallas_call(
        _ffn_kernel,
        grid=(B // seqs, T // rows, d_ff // FF_COLS),
        in_specs=[
            row_block,
            pl.BlockSpec((seqs, 6, D_MODEL), lambda b, t, j: (b, 0, 0)),
            pl.BlockSpec((D_MODEL, FF_COLS), lambda b, t, j: (0, j)),
            pl.BlockSpec((D_MODEL, FF_COLS), lambda b, t, j: (0, j)),
            pl.BlockSpec((FF_COLS, D_MODEL), lambda b, t, j: (j, 0)),
            vec, vec,
        ],
        out_specs=row_block,
        out_shape=jax.ShapeDtypeStruct((B, T, D_MODEL), F32),
        scratch_shapes=[pltpu.VMEM((seqs * rows, D_MODEL), BF16), pltpu.VMEM((seqs * rows, D_MODEL), F32)],
        compiler_params=_params("parallel", "parallel", "arbitrary"),
        name="ffn",
    )(x, mod, w_gate, w_up, w_down, ln_g, ln_b)


def _tile(batch, seq, row_tile=ROW_TILE):
    if seq >= row_tile:
        return 1, row_tile
    return min(batch, row_tile // seq), seq


def _layer(x, mod, attn_fn, halo_src, pos0, halo_from_p, w):
    B, T, _ = x.shape
    seqs, rows = _tile(B, T)
    qkv, k, v, p, gab = _proj(x, mod, w["w_in"], seqs, rows)
    o = attn_fn(qkv)
    m = _gate(o, p, p if halo_from_p else halo_src, gab, w["w_sb_out"], w["w_pool"], w["pool_scale"],
              seqs, rows, pos0, halo_from_p)
    x = _mix(m, x, mod, w["w_out"], w["ln1_g"], w["ln1_b"], seqs, rows)
    x = _ffn(x, mod, w["w_gate"], w["w_up"], w["w_down"], w["ln2_g"], w["ln2_b"], seqs, rows)
    return x, k, v, p[None, :, T - POOL_PAST:, :]


def kernel(x_prompt, x_sample, c_prompt, c_sample, cache_k, cache_v, state_pool, w_ada, b_ada, w_in, w_sb_out, w_pool, pool_scale, w_out, ln1_g, ln1_b, w_gate, w_up, w_down, ln2_g, ln2_b):
    assert w_in.shape[0] == DEPTH
    n_prompt = c_prompt.shape[0]
    mod = _adaln(jnp.concatenate([c_prompt, c_sample], axis=0), w_ada[0], b_ada[0])
    mod = mod.reshape(mod.shape[0], 6, D_MODEL)
    w = {
        "w_in": _to_bf16(w_in), "w_sb_out": _to_bf16(w_sb_out), "w_out": _to_bf16(w_out),
        "w_pool": _to_bf16(w_pool.reshape(DEPTH, -1, POOL_OUT)).reshape(w_pool.shape[1:]),
        "w_gate": _to_bf16(w_gate), "w_up": _to_bf16(w_up), "w_down": _to_bf16(w_down),
        "pool_scale": pool_scale, "ln1_g": ln1_g, "ln1_b": ln1_b, "ln2_g": ln2_g, "ln2_b": ln2_b,
    }
    y_p, k_p, v_p, pool_p = _layer(x_prompt, mod[:n_prompt], _attn_prompt, None, 0, True, w)

    pool_halo = jnp.pad(state_pool[0], ((0, 0), (POOL_HALO - POOL_PAST, 0), (0, 0)))
    attn_sample = lambda qkv: _attn_sample(qkv, cache_k, cache_v)
    y_s, k_s, v_s, pool_s = _layer(x_sample, mod[n_prompt:], attn_sample, pool_halo, cache_k.shape[3], False, w)
    return y_p, y_s, k_p, v_p, pool_p, k_s, v_s, pool_s
```
